```python
import math
import jax
import jax.numpy as jnp
from jax import lax
import numpy as np

D_MODEL = 1024
BATCH = 2
SEQ = 8192
DEPTH = 1
DEC_BATCH = 32
DEC_SEQ = 1
PAST_LEN = 8192
PAGE_SIZE = 128

N_META = 16
H_A = 4
DK_A = 128
DV_A = 128
CONV_W = 4
CHUNK_A = 64
H_B = 4
DH_B = 128
Q_BLOCK = 128
N_GROUPS = 4
EXPERTS_PER_GROUP = 4
N_EXPERTS = N_GROUPS * EXPERTS_PER_GROUP
TOP_K_IN_GROUP = 2
D_FF_E = 512
N_BRANCH = 2
W_A = H_A * DV_A
W_B = H_B * DH_B
CONV_CH = 2 * H_A * DK_A + H_A * DV_A
PROJ_WIDTHS = (H_A * DK_A, H_A * DK_A, H_A * DV_A, H_A * DV_A, H_A, H_A,
               H_B * DH_B, H_B * DH_B, H_B * DH_B, H_B, N_BRANCH * D_MODEL)
D_PROJ = sum(PROJ_WIDTHS)
DEEP_ALPHA = (2 * DEPTH) ** 0.25
DEEP_BETA = (8 * DEPTH) ** -0.25
LN_EPS = 1e-5
RMS_EPS = 1e-6
NEG_INF = -1e30

kernel_name = 'hybrid_gdn_fox_hmoe_decode_step'


def layer_norm(x, g, b):
    xf = x.astype(jnp.float32)
    mu = jnp.mean(xf, -1, keepdims=True)
    var = jnp.mean(jnp.square(xf - mu), -1, keepdims=True)
    return ((xf - mu) * lax.rsqrt(var + LN_EPS) * g.astype(jnp.float32) + b.astype(jnp.float32)).astype(x.dtype)


def l2_normalize(x):
    xf = x.astype(jnp.float32)
    return xf * lax.rsqrt(jnp.sum(xf * xf, -1, keepdims=True) + RMS_EPS)


def gated_rmsnorm(o, z, w):
    o = o.astype(jnp.float32)
    o = o * lax.rsqrt(jnp.mean(o * o, -1, keepdims=True) + RMS_EPS) * w.astype(jnp.float32)
    return o * jax.nn.silu(z.astype(jnp.float32))


def split_projection(p):
    parts, start = [], 0
    for width in PROJ_WIDTHS:
        parts.append(p[..., start:start + width])
        start += width
    return parts


def front_pad(a, pad):
    return jnp.pad(a, [(0, 0), (pad, 0)] + [(0, 0)] * (a.ndim - 2))


def causal_depthwise_conv(xpad, w):
    t = xpad.shape[1] - (CONV_W - 1)
    return sum(xpad[:, i:i + t] * w[i] for i in range(CONV_W))


def mixer_front(xn, conv_prefix, w_in, conv_w, a_log, dt_bias, f_bias):
    f32 = jnp.float32
    bsz, t = xn.shape[:2]
    proj = jnp.einsum('btd,dp->btp', xn, w_in)
    gq, gk, gv, gz, ga, gb, fq, fk, fv, ff, gates = split_projection(proj)
    qkv_pad = jnp.concatenate([conv_prefix.astype(proj.dtype), jnp.concatenate([gq, gk, gv], -1)], 1)
    qkv = jax.nn.silu(causal_depthwise_conv(qkv_pad, conv_w)).astype(f32)
    new_conv = qkv_pad[:, -(CONV_W - 1):]
    hk = H_A * DK_A
    q_a = l2_normalize(qkv[..., :hk].reshape(bsz, t, H_A, DK_A)) * DK_A ** -0.5
    k_a = l2_normalize(qkv[..., hk:2 * hk].reshape(bsz, t, H_A, DK_A))
    v_a = qkv[..., 2 * hk:].reshape(bsz, t, H_A, DV_A)
    g_a = -jnp.exp(a_log.astype(f32)) * jax.nn.softplus(ga.astype(f32) + dt_bias.astype(f32))
    beta_a = jax.nn.sigmoid(gb.astype(f32))
    z_a = gz.reshape(bsz, t, H_A, DV_A)
    q_b = fq.reshape(bsz, t, H_B, DH_B)
    k_b = fk.reshape(bsz, t, H_B, DH_B)
    v_b = fv.reshape(bsz, t, H_B, DH_B)
    logf_b = jax.nn.log_sigmoid(ff.astype(f32) + f_bias.astype(f32))
    gates = gates.reshape(bsz, t, N_BRANCH, D_MODEL)
    return q_a, k_a, v_a, g_a, beta_a, z_a, q_b, k_b, v_b, logf_b, gates, new_conv


def gdn_chunked(q, k, v, g, beta, s0):
    bsz, length, nh, dk = q.shape
    dv = v.shape[-1]
    n = length // CHUNK_A

    def to_chunks(a):
        return jnp.moveaxis(a.reshape((bsz, n, CHUNK_A, nh) + a.shape[3:]), 3, 1)

    q, k, v, g, beta = (to_chunks(a) for a in (q, k, v, g, beta))
    gc = jnp.cumsum(g, -1)
    idx = jnp.arange(CHUNK_A)
    incl = idx[:, None] >= idx[None, :]
    strict = idx[:, None] > idx[None, :]
    decay = jnp.exp(jnp.where(incl, gc[..., :, None] - gc[..., None, :], NEG_INF))
    kb = k * beta[..., None]
    vb = v * beta[..., None]
    m = jnp.where(strict, jnp.einsum('bhnik,bhnjk->bhnij', kb, k) * decay, 0.0)
    a = m + jnp.eye(CHUNK_A, dtype=m.dtype)
    rhs = jnp.concatenate([vb, kb * jnp.exp(gc)[..., None]], -1)
    sol = lax.linalg.triangular_solve(a, rhs, left_side=True, lower=True, unit_diagonal=True)
    u, w = sol[..., :dv], sol[..., dv:]
    attn = jnp.where(incl, jnp.einsum('bhnik,bhnjk->bhnij', q, k) * decay, 0.0)
    q_dec = q * jnp.exp(gc)[..., None]
    k_dec = k * jnp.exp(gc[..., -1:] - gc)[..., None]
    g_tot = jnp.exp(gc[..., -1])

    def step(s, xs):
        u_n, w_n, attn_n, q_n, k_n, gt_n = xs
        v_new = u_n - jnp.einsum('bhck,bhkv->bhcv', w_n, s)
        o_n = jnp.einsum('bhck,bhkv->bhcv', q_n, s) + jnp.einsum('bhij,bhjv->bhiv', attn_n, v_new)
        s = s * gt_n[..., None, None] + jnp.einsum('bhck,bhcv->bhkv', k_n, v_new)
        return s, o_n

    xs = tuple(jnp.moveaxis(arr, 2, 0) for arr in (u, w, attn, q_dec, k_dec, g_tot))
    s, o = lax.scan(step, s0, xs)
    o = jnp.transpose(o, (1, 0, 3, 2, 4)).reshape(bsz, length, nh, dv)
    return o, s


def gdn_prompt(q, k, v, g, beta, s0):
    pad = (-N_META) % CHUNK_A
    q, k, v, g, beta = (front_pad(a, pad) for a in (q, k, v, g, beta))
    o, s = gdn_chunked(q, k, v, g, beta, s0)
    return o[:, pad:], s


def gdn_recurrent(q, k, v, g, beta, s0):
    def step(s, xs):
        q_t, k_t, v_t, g_t, b_t = xs
        s = s * jnp.exp(g_t)[..., None, None]
        v_t = (v_t - jnp.einsum('bhk,bhkv->bhv', k_t, s)) * b_t[..., None]
        s = s + jnp.einsum('bhk,bhv->bhkv', k_t, v_t)
        return s, jnp.einsum('bhk,bhkv->bhv', q_t, s)

    xs = tuple(jnp.moveaxis(arr, 1, 0) for arr in (q, k, v, g, beta))
    s, o = lax.scan(step, s0, xs)
    return jnp.moveaxis(o, 0, 1), s


def fox_prompt(q, k, v, logf):
    bsz, t = q.shape[:2]
    pad = (-N_META) % Q_BLOCK
    q, k, v, logf = (front_pad(a, pad) for a in (q, k, v, logf))
    lp = t + pad
    c_t = jnp.transpose(jnp.cumsum(logf, axis=1), (0, 2, 1))
    kpos = jnp.arange(lp)
    key_ok = kpos >= pad

    def one_block(i):
        start = i * Q_BLOCK
        qb = lax.dynamic_slice_in_dim(q, start, Q_BLOCK, axis=1)
        cq = lax.dynamic_slice_in_dim(c_t, start, Q_BLOCK, axis=2)
        s = jnp.einsum('bqhd,bkhd->bhqk', qb, k).astype(jnp.float32) * DH_B ** -0.5
        s = s + cq[..., :, None] - c_t[..., None, :]
        qpos = start + jnp.arange(Q_BLOCK)
        ok = (kpos[None, :] <= qpos[:, None]) & key_ok[None, :]
        p = jax.nn.softmax(jnp.where(ok, s, NEG_INF), axis=-1)
        return jnp.einsum('bhqk,bkhd->bqhd', p.astype(v.dtype), v)

    o = lax.map(one_block, jnp.arange(lp // Q_BLOCK))
    return jnp.moveaxis(o, 0, 1).reshape(bsz, lp, H_B, DH_B)[:, pad:]


def fox_sample(q, k, v, logf, kp, vp, logf_p):
    t = q.shape[1]
    p_len = kp.shape[1]
    cp = jnp.cumsum(logf_p, axis=1)
    cn = cp[:, -1:] + jnp.cumsum(logf, axis=1)
    cp_t = jnp.transpose(cp, (0, 2, 1))
    cn_t = jnp.transpose(cn, (0, 2, 1))
    scale = DH_B ** -0.5
    s_p = jnp.einsum('bqhd,bkhd->bhqk', q, kp).astype(jnp.float32) * scale + cn_t[..., :, None] - cp_t[..., None, :]
    s_n = jnp.einsum('bqhd,bkhd->bhqk', q, k).astype(jnp.float32) * scale + cn_t[..., :, None] - cn_t[..., None, :]
    causal = jnp.arange(t)[:, None] >= jnp.arange(t)[None, :]
    s_n = jnp.where(causal, s_n, NEG_INF)
    p = jax.nn.softmax(jnp.concatenate([s_p, s_n], -1), axis=-1)
    return (jnp.einsum('bhqk,bkhd->bqhd', p[..., :p_len].astype(vp.dtype), vp)
            + jnp.einsum('bhqk,bkhd->bqhd', p[..., p_len:].astype(v.dtype), v))


def hier_moe(h, w_rg, b_rg, w_re, b_re, w_gate, w_up, w_down):
    bsz, t, d = h.shape
    x = h.reshape(bsz * t, d)
    n = x.shape[0]
    gp = jax.nn.softmax((x @ w_rg).astype(jnp.float32) + b_rg.astype(jnp.float32), axis=-1)
    gw, gsel = lax.top_k(gp, 1)
    elog = ((x @ w_re).astype(jnp.float32) + b_re.astype(jnp.float32)).reshape(n, N_GROUPS, EXPERTS_PER_GROUP)
    elog = jnp.take_along_axis(elog, jnp.broadcast_to(gsel[:, :, None], (n, 1, EXPERTS_PER_GROUP)), axis=1)[:, 0]
    ep = jax.nn.softmax(elog, axis=-1)
    ew, esel = lax.top_k(ep, TOP_K_IN_GROUP)
    ew = ew / jnp.sum(ew, -1, keepdims=True) * gw
    eid = gsel * EXPERTS_PER_GROUP + esel
    combine = jnp.sum(jax.nn.one_hot(eid, N_EXPERTS, dtype=jnp.float32) * ew[..., None], 1)
    y = jnp.zeros((n, d), jnp.float32)
    for e in range(N_EXPERTS):
        hid = jax.nn.silu(x @ w_gate[e]) * (x @ w_up[e])
        y = y + combine[:, e:e + 1] * (hid @ w_down[e])
    return y.astype(h.dtype).reshape(bsz, t, d)


def layer_back(x, o_a, z_a, o_b, gates, gdn_norm_w, w_branch, w_out, ln1_g, ln1_b,
               w_rg, b_rg, w_re, b_re, w_gate, w_up, w_down, ln2_g, ln2_b):
    bsz, t = x.shape[:2]
    o_a = gated_rmsnorm(o_a, z_a, gdn_norm_w).reshape(bsz, t, W_A)
    o_b = o_b.astype(jnp.float32).reshape(bsz, t, W_B)
    branches = jnp.stack([o_a, o_b], 2)
    proj = jnp.einsum('btnc,ncd->btnd', branches, w_branch)
    merged = jnp.sum(jax.nn.sigmoid(gates.astype(jnp.float32)) * proj, 2)
    mix = jnp.einsum('btd,de->bte', merged, w_out).astype(x.dtype)
    h = layer_norm(DEEP_ALPHA * x + mix, ln1_g, ln1_b)
    ffn = hier_moe(h, w_rg, b_rg, w_re, b_re, w_gate, w_up, w_down)
    return layer_norm(DEEP_ALPHA * h + ffn, ln2_g, ln2_b)


def setup_inputs(seed: int = 0) -> dict:
    key = jax.random.key(seed)
    ks = jax.random.split(key, 32)
    f32 = jnp.float32
    n_pages = PAST_LEN // PAGE_SIZE
    n_pool = (DEC_BATCH * n_pages * 5) // 4

    def nrm(k, shape, scale=1.0):
        return jax.random.normal(k, shape, f32) * scale

    dt = jnp.exp(jax.random.uniform(ks[14], (DEPTH, H_A), f32, math.log(1e-3), math.log(1e-1)))
    return {
        'x_prompt': nrm(ks[0], (BATCH, SEQ, D_MODEL)),
        'x_sample': nrm(ks[1], (DEC_BATCH, DEC_SEQ, D_MODEL)),
        'cache_k': nrm(ks[2], (DEPTH, n_pool, PAGE_SIZE, H_B, DH_B)),
        'cache_v': nrm(ks[3], (DEPTH, n_pool, PAGE_SIZE, H_B, DH_B)),
        'cache_logf': jax.nn.log_sigmoid(nrm(ks[4], (DEPTH, n_pool, PAGE_SIZE, H_B)) + 3.0),
        'state_gdn': nrm(ks[5], (DEPTH, DEC_BATCH, H_A, DK_A, DV_A), 0.1),
        'state_conv': nrm(ks[6], (DEPTH, DEC_BATCH, CONV_W - 1, CONV_CH)),
        'page_table': jax.random.permutation(ks[7], n_pool)[:DEC_BATCH * n_pages].reshape(DEC_BATCH, n_pages).astype(jnp.int32),
        'meta_tokens': nrm(ks[8], (N_META, D_MODEL)),
        'ln_in_g': 1.0 + nrm(ks[9], (D_MODEL,), 0.02),
        'ln_in_b': nrm(ks[10], (D_MODEL,), 0.02),
        'w_in': nrm(ks[11], (DEPTH, D_MODEL, D_PROJ), D_MODEL ** -0.5),
        'conv_w': nrm(ks[12], (DEPTH, CONV_W, CONV_CH), CONV_W ** -0.5),
        'a_log': jnp.log(jax.random.uniform(ks[13], (DEPTH, H_A), f32, 1.0, 16.0)),
        'dt_bias': dt + jnp.log(-jnp.expm1(-dt)),
        'f_bias': jax.random.uniform(ks[15], (DEPTH, H_B), f32, 1.0, 6.0),
        'gdn_norm_w': 1.0 + nrm(ks[16], (DEPTH, DV_A), 0.02),
        'w_branch': nrm(ks[17], (DEPTH, N_BRANCH, W_A, D_MODEL), W_A ** -0.5),
        'w_out': nrm(ks[18], (DEPTH, D_MODEL, D_MODEL), D_MODEL ** -0.5 * DEEP_BETA),
        'ln1_g': 1.0 + nrm(ks[19], (DEPTH, D_MODEL), 0.02),
        'ln1_b': nrm(ks[20], (DEPTH, D_MODEL), 0.02),
        'w_router_group': nrm(ks[21], (DEPTH, D_MODEL, N_GROUPS), D_MODEL ** -0.5),
        'b_router_group': nrm(ks[22], (DEPTH, N_GROUPS), 0.01),
        'w_router_expert': nrm(ks[23], (DEPTH, D_MODEL, N_EXPERTS), D_MODEL ** -0.5),
        'b_router_expert': nrm(ks[24], (DEPTH, N_EXPERTS), 0.01),
        'w_gate': nrm(ks[25], (DEPTH, N_EXPERTS, D_MODEL, D_FF_E), D_MODEL ** -0.5),
        'w_up': nrm(ks[26], (DEPTH, N_EXPERTS, D_MODEL, D_FF_E), D_MODEL ** -0.5),
        'w_down': nrm(ks[27], (DEPTH, N_EXPERTS, D_FF_E, D_MODEL), D_FF_E ** -0.5 * DEEP_BETA),
        'ln2_g': 1.0 + nrm(ks[28], (DEPTH, D_MODEL), 0.02),
        'ln2_b': nrm(ks[29], (DEPTH, D_MODEL), 0.02),
    }


def reference(x_prompt, x_sample, cache_k, cache_v, cache_logf, state_gdn, state_conv, page_table,
              meta_tokens, ln_in_g, ln_in_b, w_in, conv_w, a_log, dt_bias, f_bias, gdn_norm_w,
              w_branch, w_out, ln1_g, ln1_b, w_router_group, b_router_group, w_router_expert,
              b_router_expert, w_gate, w_up, w_down, ln2_g, ln2_b):
    bsz = x_prompt.shape[0]
    dbsz = x_sample.shape[0]
    meta = jnp.broadcast_to(meta_tokens.astype(x_prompt.dtype)[None], (bsz, N_META, D_MODEL))
    xp = layer_norm(jnp.concatenate([meta, x_prompt], 1), ln_in_g, ln_in_b)
    xs = layer_norm(x_sample, ln_in_g, ln_in_b)
    zero_conv = jnp.zeros((bsz, CONV_W - 1, CONV_CH), xp.dtype)
    zero_state = jnp.zeros((bsz, H_A, DK_A, DV_A), jnp.float32)
    kp_l, vp_l, lfp_l, sp_l, cp_l = [], [], [], [], []
    ks_l, vs_l, lfs_l, ss_l, cs_l = [], [], [], [], []
    for l in range(DEPTH):
        back_w = (gdn_norm_w[l], w_branch[l], w_out[l], ln1_g[l], ln1_b[l], w_router_group[l],
                  b_router_group[l], w_router_expert[l], b_router_expert[l], w_gate[l], w_up[l],
                  w_down[l], ln2_g[l], ln2_b[l])
        (q_a, k_a, v_a, g_a, beta_a, z_a, q_b, k_b, v_b, logf_b, gates, conv_new) = mixer_front(
            xp, zero_conv, w_in[l], conv_w[l], a_log[l], dt_bias[l], f_bias[l])
        o_a, s_new = gdn_prompt(q_a, k_a, v_a, g_a, beta_a, zero_state)
        o_b = fox_prompt(q_b, k_b, v_b, logf_b)
        xp = layer_back(xp, o_a, z_a, o_b, gates, *back_w)
        kp_l.append(k_b)
        vp_l.append(v_b)
        lfp_l.append(logf_b)
        sp_l.append(s_new)
        cp_l.append(conv_new)
        (q_a, k_a, v_a, g_a, beta_a, z_a, q_b, k_b, v_b, logf_b, gates, conv_new) = mixer_front(
            xs, state_conv[l], w_in[l], conv_w[l], a_log[l], dt_bias[l], f_bias[l])
        o_a, s_new = gdn_recurrent(q_a, k_a, v_a, g_a, beta_a, state_gdn[l].astype(jnp.float32))
        k_past = cache_k[l, page_table].reshape(dbsz, -1, H_B, DH_B)
        v_past = cache_v[l, page_table].reshape(dbsz, -1, H_B, DH_B)
        lf_past = cache_logf[l, page_table].reshape(dbsz, -1, H_B).astype(jnp.float32)
        o_b = fox_sample(q_b, k_b, v_b, logf_b, k_past, v_past, lf_past)
        xs = layer_back(xs, o_a, z_a, o_b, gates, *back_w)
        ks_l.append(k_b)
        vs_l.append(v_b)
        lfs_l.append(logf_b)
        ss_l.append(s_new)
        cs_l.append(conv_new)
    return (xp[:, N_META:], xs,
            jnp.stack(kp_l), jnp.stack(vp_l), jnp.stack(lfp_l), jnp.stack(sp_l), jnp.stack(cp_l),
            jnp.stack(ks_l), jnp.stack(vs_l), jnp.stack(lfs_l), jnp.stack(ss_l), jnp.stack(cs_l))
```

```python
import functools

import jax
import jax.numpy as jnp
from jax import lax
from jax.experimental import pallas as pl
from jax.experimental.pallas import tpu as pltpu

F32 = jnp.float32
BF16 = jnp.bfloat16
HIGHEST = lax.Precision.HIGHEST

N_META = 16
H_A = 4
DK_A = 128
DV_A = 128
CONV_W = 4
CHUNK_A = 64
H_B = 4
DH_B = 128
N_GROUPS = 4
EXPERTS_PER_GROUP = 4
N_EXPERTS = N_GROUPS * EXPERTS_PER_GROUP
D_FF_E = 512
DEPTH = 1
DEEP_ALPHA = (2 * DEPTH) ** 0.25
LN_EPS = 1e-5
RMS_EPS = 1e-6
NEG_INF = -1e30

LANES = 128
W_A = H_A * DV_A
QKV = 3 * W_A
C_QKV = 0
C_Z = C_QKV + QKV
C_FQ = C_Z + W_A
C_FK = C_FQ + W_A
C_FV = C_FK + W_A
C_GATES = C_FV + W_A
C_SMALL = C_GATES + 2 * 1024
P_COLS = C_SMALL + LANES
L_G, L_BETA, L_LOGF, L_C = 0, 4, 8, 12
L_GRP, L_EXP = 0, 4

VMEM_LIMIT = 56 * 1024 * 1024


def _cparams(sem, vmem=VMEM_LIMIT):
    return pltpu.CompilerParams(dimension_semantics=sem, vmem_limit_bytes=vmem)


def _layer_norm(x, g, b):
    mu = jnp.mean(x, -1, keepdims=True)
    xc = x - mu
    var = jnp.mean(xc * xc, -1, keepdims=True)
    return xc * lax.rsqrt(var + LN_EPS) * g + b


def _softplus(x):
    return jnp.maximum(x, 0.0) + jnp.log1p(jnp.exp(-jnp.abs(x)))


def _sigmoid(x):
    return 1.0 / (1.0 + jnp.exp(-x))


def _dot(a, b):
    return jnp.dot(a, b, preferred_element_type=F32)


def _dot_nt(a, b):
    return lax.dot_general(a, b, (((1,), (1,)), ((), ())), preferred_element_type=F32)


def _dot_tn(a, b):
    return lax.dot_general(a, b, (((0,), (0,)), ((), ())), preferred_element_type=F32)


def _gate_values(sm, sp, rows):
    lane = lax.broadcasted_iota(jnp.int32, (rows, LANES), 1)
    g = -jnp.exp(sp[0:1]) * _softplus(sm + sp[1:2])
    beta = _sigmoid(sm)
    logf = -_softplus(-(sm + sp[2:3]))
    return jnp.where(lane < L_BETA, g, jnp.where(lane < L_LOGF, beta, jnp.where(lane < L_C + 4, logf, 0.0)))


def _front_kernel(x_ref, lng_ref, lnb_ref, w_ref, convw_ref, halo_ref, sp_ref, c0_ref,
                  qa_ref, ka_ref, va_ref, z_ref, fq_ref, fk32_ref, fv32_ref, fkb_ref, fvb_ref,
                  gates_ref, small_ref, smallt_ref, tail_ref, buf_ref, carry_ref, *, tm, blk):
    j = pl.program_id(1)
    nj = pl.num_programs(1)

    @pl.when(j == 0)
    def _():
        buf_ref[0:8, :] = halo_ref[0]
        carry_ref[...] = c0_ref[...]

    xn = _layer_norm(x_ref[0], lng_ref[...], lnb_ref[...]).astype(BF16)

    def proj(c0, width):
        return _dot(xn, w_ref[:, c0:c0 + width])

    buf_ref[8:8 + tm, :] = proj(C_QKV, QKV)
    outs = (qa_ref, ka_ref, va_ref)
    for grp in range(QKV // LANES):
        c = grp * LANES
        w = convw_ref[:, c:c + LANES]
        y = (buf_ref[5:5 + tm, c:c + LANES] * w[0:1] + buf_ref[6:6 + tm, c:c + LANES] * w[1:2]
             + buf_ref[7:7 + tm, c:c + LANES] * w[2:3] + buf_ref[8:8 + tm, c:c + LANES] * w[3:4])
        y = y * _sigmoid(y)
        which, head = divmod(grp, H_A)
        if which < 2:
            y = y * lax.rsqrt(jnp.sum(y * y, -1, keepdims=True) + RMS_EPS)
            if which == 0:
                y = y * DK_A ** -0.5
        outs[which][0, :, head * LANES:(head + 1) * LANES] = y.astype(BF16)

    @pl.when(j == nj - 1)
    def _():
        tail_ref[0] = buf_ref[tm:tm + 8, :]

    buf_ref[0:8, :] = buf_ref[tm:tm + 8, :]

    z_ref[0] = proj(C_Z, W_A).astype(BF16)
    fq_ref[0] = (proj(C_FQ, W_A) * DH_B ** -0.5).astype(BF16)
    fk = proj(C_FK, W_A)
    fk32_ref[0] = fk
    fkb_ref[0] = fk.astype(BF16)
    fv = proj(C_FV, W_A)
    fv32_ref[0] = fv
    fvb_ref[0] = fv.astype(BF16)
    gates_ref[0] = proj(C_GATES, 2 * 1024).astype(BF16)

    vals = _gate_values(proj(C_SMALL, LANES), sp_ref[...], tm)
    r = lax.broadcasted_iota(jnp.int32, (tm, tm), 0)
    c = lax.broadcasted_iota(jnp.int32, (tm, tm), 1)
    same_blk = c >= (r // blk) * blk
    tri = jnp.where(c <= r, jnp.where(same_blk, 1.0, 0.0), 0.0)
    cs = jnp.dot(tri, vals, precision=HIGHEST, preferred_element_type=F32)
    run = carry_ref[...]
    pieces = []
    for b in range(tm // blk):
        piece = cs[b * blk:(b + 1) * blk, :] + run
        pieces.append(piece)
        run = piece[blk - 1:blk, :]
    carry_ref[...] = run
    cfull = pieces[0] if len(pieces) == 1 else jnp.concatenate(pieces, 0)
    lane = lax.broadcasted_iota(jnp.int32, (tm, LANES), 1)
    small = jnp.where(lane < L_BETA, cs, jnp.where(lane < L_C, vals, cfull))
    small_ref[0] = small
    if smallt_ref is not None:
        smallt_ref[0] = small.T[0:16, :]


def _front_call(x, lng, lnb, w_big, conv_w, halo, sp, c0, *, tm, emit_t):
    bsz, seq, d = x.shape
    blk = min(CHUNK_A, tm)
    nj = seq // tm
    tok = lambda width, dt: jax.ShapeDtypeStruct((bsz, seq, width), dt)
    tspec = lambda width: pl.BlockSpec((1, tm, width), lambda b, j: (b, j, 0))
    full2 = lambda a: pl.BlockSpec(a.shape, lambda b, j: (0, 0))
    out_shape = [tok(W_A, BF16), tok(W_A, BF16), tok(W_A, BF16), tok(W_A, BF16), tok(W_A, BF16),
                 tok(W_A, F32), tok(W_A, F32), tok(W_A, BF16), tok(W_A, BF16),
                 tok(2 * 1024, BF16), tok(LANES, F32)]
    out_specs = [tspec(W_A)] * 9 + [tspec(2 * 1024), tspec(LANES)]
    if emit_t:
        out_shape.append(jax.ShapeDtypeStruct((bsz, 16, seq), F32))
        out_specs.append(pl.BlockSpec((1, 16, tm), lambda b, j: (b, 0, j)))
    out_shape.append(jax.ShapeDtypeStruct((bsz, 8, QKV), F32))
    out_specs.append(pl.BlockSpec((1, 8, QKV), lambda b, j: (b, 0, 0)))

    def body(*refs):
        ins, rest = refs[:8], refs[8:]
        if emit_t:
            outs, scratch = rest[:13], rest[13:]
            return _front_kernel(*ins, *outs, *scratch, tm=tm, blk=blk)
        outs, scratch = rest[:12], rest[12:]
        return _front_kernel(*ins, *outs[:11], None, outs[11], *scratch, tm=tm, blk=blk)

    res = pl.pallas_call(
        body,
        grid=(bsz, nj),
        in_specs=[pl.BlockSpec((1, tm, d), lambda b, j: (b, j, 0)), full2(lng), full2(lnb),
                  pl.BlockSpec(w_big.shape, lambda b, j: (0, 0), pipeline_mode=pl.Buffered(1)),
                  full2(conv_w), pl.BlockSpec((1, 8, QKV), lambda b, j: (0, 0, 0)), full2(sp), full2(c0)],
        out_specs=out_specs,
        out_shape=out_shape,
        scratch_shapes=[pltpu.VMEM((tm + 8, QKV), F32), pltpu.VMEM((1, LANES), F32)],
        compiler_params=_cparams(("arbitrary", "arbitrary")),
        name="front",
    )(x, lng, lnb, w_big, conv_w, halo, sp, c0)
    names = ["qa", "ka", "va", "z", "fq", "fk32", "fv32", "fkb", "fvb", "gates", "small"]
    names += (["small_t"] if emit_t else []) + ["tail"]
    return dict(zip(names, res))


def _sample_front_kernel(x_ref, lng_ref, lnb_ref, w_ref, convw_ref, st_ref, sp_ref,
                         q_ref, k_ref, v_ref, z_ref, fq_ref, fk_ref, fv_ref, gates_ref, small_ref, new_ref):
    rows = x_ref.shape[0]
    xn = _layer_norm(x_ref[...], lng_ref[...], lnb_ref[...]).astype(BF16)

    def proj(c0, width):
        return _dot(xn, w_ref[:, c0:c0 + width])

    new = proj(C_QKV, QKV)
    new_ref[...] = new
    w = convw_ref[...]
    y = st_ref[0] * w[0:1] + st_ref[1] * w[1:2] + st_ref[2] * w[2:3] + new * w[3:4]
    y = y * _sigmoid(y)
    outs = (q_ref, k_ref, v_ref)
    for grp in range(QKV // LANES):
        which, head = divmod(grp, H_A)
        yy = y[:, grp * LANES:(grp + 1) * LANES]
        if which < 2:
            yy = yy * lax.rsqrt(jnp.sum(yy * yy, -1, keepdims=True) + RMS_EPS)
            if which == 0:
                yy = yy * DK_A ** -0.5
        outs[which][:, head * LANES:(head + 1) * LANES] = yy
    z_ref[...] = proj(C_Z, W_A)
    fq_ref[...] = proj(C_FQ, W_A) * DH_B ** -0.5
    fk_ref[...] = proj(C_FK, W_A)
    fv_ref[...] = proj(C_FV, W_A)
    gates_ref[...] = proj(C_GATES, 2 * 1024)
    small_ref[...] = _gate_values(proj(C_SMALL, LANES), sp_ref[...], rows)


def _sample_front_call(x, lng, lnb, w_big, conv_w, st, sp):
    rows = x.shape[0]
    o = lambda width: jax.ShapeDtypeStruct((rows, width), F32)
    res = pl.pallas_call(
        _sample_front_kernel,
        out_shape=[o(W_A), o(W_A), o(W_A), o(W_A), o(W_A), o(W_A), o(W_A), o(2 * 1024), o(LANES), o(QKV)],
        compiler_params=pltpu.CompilerParams(vmem_limit_bytes=VMEM_LIMIT),
        name="sample_front",
    )(x, lng, lnb, w_big, conv_w, st, sp)
    return dict(zip(["q", "k", "v", "z", "fq", "fk", "fv", "gates", "small", "new"], res))


def _gdn_kernel(q_ref, k_ref, v_ref, sm_ref, smt_ref, s0_ref, o_ref, sout_ref, state_ref, *, lt):
    j = pl.program_id(1)
    nj = pl.num_programs(1)
    ch = CHUNK_A

    @pl.when(j == 0)
    def _():
        state_ref[...] = s0_ref[0]

    ri = lax.broadcasted_iota(jnp.int32, (ch, ch), 0)
    ci = lax.broadcasted_iota(jnp.int32, (ch, ch), 1)
    incl = ri >= ci
    strict = ri > ci
    sm = sm_ref[0]
    esm = jnp.exp(sm)
    local = {}
    for c in range(lt // ch):
        rows = slice(c * ch, (c + 1) * ch)
        for h in range(H_A):
            cols = slice(h * DK_A, (h + 1) * DK_A)
            kc = k_ref[0, rows, cols]
            qc = q_ref[0, rows, cols]
            vc = v_ref[0, rows, cols]
            gc_col = sm[rows, L_G + h:L_G + h + 1]
            beta = sm[rows, L_BETA + h:L_BETA + h + 1]
            gc_row = smt_ref[0, L_G + h:L_G + h + 1, rows]
            gc_last = sm[c * ch + ch - 1:c * ch + ch, L_G + h:L_G + h + 1]
            decay = jnp.exp(jnp.where(incl, gc_col - gc_row, NEG_INF))
            kf = kc.astype(F32)
            kb = kf * beta
            a = _dot_nt(jnp.concatenate([kb.astype(BF16), qc], 0), kc)
            m = jnp.where(strict, a[:ch] * decay, 0.0)
            attn = a[ch:] * decay
            x = -m
            p = m
            for _ in range(5):
                pb = p.astype(BF16)
                p = _dot(pb, pb)
                x = x + p + _dot(x.astype(BF16), p.astype(BF16))
            eg = esm[rows, L_G + h:L_G + h + 1]
            rhs = jnp.concatenate([vc.astype(F32) * beta, kb * eg], 1)
            sol = rhs + _dot(x.astype(BF16), rhs.astype(BF16))
            local[c, h] = dict(
                u=sol[:, :DV_A],
                wq=jnp.concatenate([sol[:, DV_A:].astype(BF16), (qc.astype(F32) * eg).astype(BF16)], 0),
                attn=attn.astype(BF16),
                k_dec=(kf * jnp.exp(gc_last - gc_col)).astype(BF16),
                g_tot=jnp.exp(gc_last))
    for h in range(H_A):
        cols = slice(h * DK_A, (h + 1) * DK_A)
        s = state_ref[h]
        for c in range(lt // ch):
            d = local[c, h]
            ws = _dot(d["wq"], s.astype(BF16))
            v_new = (d["u"] - ws[:ch]).astype(BF16)
            o = ws[ch:] + _dot(d["attn"], v_new)
            s = s * d["g_tot"] + _dot_tn(d["k_dec"], v_new)
            o_ref[0, c * ch:(c + 1) * ch, cols] = o.astype(o_ref.dtype)
        state_ref[h] = s

    @pl.when(j == nj - 1)
    def _():
        sout_ref[0] = state_ref[...]


def _gdn_call(q, k, v, small, small_t, s0, *, lt):
    bsz, seq, _ = q.shape
    nj = seq // lt
    tspec = pl.BlockSpec((1, lt, W_A), lambda b, j: (b, j, 0))
    sspec = pl.BlockSpec((1, H_A, DK_A, DV_A), lambda b, j: (b, 0, 0, 0))
    s0spec = sspec if s0.shape[0] == bsz else pl.BlockSpec((1, H_A, DK_A, DV_A), lambda b, j: (0, 0, 0, 0))
    return pl.pallas_call(
        functools.partial(_gdn_kernel, lt=lt),
        grid=(bsz, nj),
        in_specs=[tspec, tspec, tspec, pl.BlockSpec((1, lt, LANES), lambda b, j: (b, j, 0)),
                  pl.BlockSpec((1, 16, lt), lambda b, j: (b, 0, j)), s0spec],
        out_specs=[tspec, sspec],
        out_shape=[jax.ShapeDtypeStruct((bsz, seq, W_A), BF16), jax.ShapeDtypeStruct((bsz, H_A, DK_A, DV_A), F32)],
        scratch_shapes=[pltpu.VMEM((H_A, DK_A, DV_A), F32)],
        compiler_params=_cparams(("arbitrary", "arbitrary")),
        name="gdn_chunk",
    )(q, k, v, small, small_t, s0)


def _gdn_step_kernel(q_ref, k_ref, v_ref, sm_ref, s_ref, o_ref, sout_ref):
    ri = lax.broadcasted_iota(jnp.int32, (DK_A, DK_A), 0)
    ci = lax.broadcasted_iota(jnp.int32, (DK_A, DK_A), 1)
    eye = ri == ci

    def column(row):
        return jnp.sum(jnp.where(eye, jnp.broadcast_to(row, (DK_A, DK_A)), 0.0), -1, keepdims=True)

    sm = sm_ref[0]
    for h in range(H_A):
        cols = slice(h * DK_A, (h + 1) * DK_A)
        kcol = column(k_ref[0, :, cols])
        qcol = column(q_ref[0, :, cols])
        s = s_ref[0, h] * jnp.exp(sm[:, L_G + h:L_G + h + 1])
        v_t = (v_ref[0, :, cols] - jnp.sum(kcol * s, 0, keepdims=True)) * sm[:, L_BETA + h:L_BETA + h + 1]
        s = s + kcol * v_t
        sout_ref[0, h] = s
        o_ref[0, :, cols] = jnp.sum(qcol * s, 0, keepdims=True)


def _gdn_step_call(q, k, v, small, state):
    n = q.shape[0]
    row = lambda a: a.reshape(n, 1, a.shape[-1])
    rspec = lambda width: pl.BlockSpec((1, 1, width), lambda b: (b, 0, 0))
    sspec = pl.BlockSpec((1, H_A, DK_A, DV_A), lambda b: (b, 0, 0, 0))
    o, s = pl.pallas_call(
        _gdn_step_kernel,
        grid=(n,),
        in_specs=[rspec(W_A), rspec(W_A), rspec(W_A), rspec(LANES), sspec],
        out_specs=[rspec(W_A), sspec],
        out_shape=[jax.ShapeDtypeStruct((n, 1, W_A), F32), jax.ShapeDtypeStruct(state.shape, F32)],
        compiler_params=_cparams(("arbitrary",)),
        name="gdn_step",
    )(row(q), row(k), row(v), row(small), state)
    return o.reshape(n, W_A), s


def _fox_kernel(q_ref, k_ref, v_ref, ct_ref, km_ref, vm_ref, cm_ref, o_ref, m_sc, l_sc, acc_sc, *, tq):
    qi = pl.program_id(2)
    q = q_ref[0]
    qstart = pl.multiple_of(qi * tq, tq)
    c_first = ct_ref[0, 0, :, pl.ds(qstart, LANES)][:, 0:1]

    s = _dot_nt(q, km_ref[...]) + (c_first - cm_ref[0])
    m0 = jnp.max(s, -1, keepdims=True)
    p = jnp.exp(s - m0)
    m_sc[...] = m0
    l_sc[...] = jnp.sum(p, -1, keepdims=True)
    acc_sc[...] = _dot(p.astype(BF16), vm_ref[...])

    def update(start, masked):
        kb = k_ref[0, pl.ds(start, tq), :]
        vb = v_ref[0, pl.ds(start, tq), :]
        s = _dot_nt(q, kb) + (c_first - ct_ref[0, 0, :, pl.ds(start, tq)])
        if masked:
            ri = lax.broadcasted_iota(jnp.int32, (tq, tq), 0)
            ci = lax.broadcasted_iota(jnp.int32, (tq, tq), 1)
            s = jnp.where(ci <= ri, s, NEG_INF)
        m_prev = m_sc[...]
        m_new = jnp.maximum(m_prev, jnp.max(s, -1, keepdims=True))
        alpha = jnp.exp(m_prev - m_new)
        p = jnp.exp(s - m_new)
        l_sc[...] = alpha * l_sc[...] + jnp.sum(p, -1, keepdims=True)
        acc_sc[...] = alpha * acc_sc[...] + _dot(p.astype(BF16), vb)
        m_sc[...] = m_new

    def body(jb, carry):
        update(pl.multiple_of(jb * tq, tq), False)
        return carry

    lax.fori_loop(0, qi, body, 0)
    update(qstart, True)
    o_ref[0] = (acc_sc[...] / l_sc[...]).astype(o_ref.dtype)


def _fox_call(fq, fkb, fvb, c_t, kmeta, vmeta, cmeta, *, tq):
    bsz, seq, _ = fq.shape
    kvspec = pl.BlockSpec((1, seq, DH_B), lambda b, h, i: (b, 0, h))
    mspec = pl.BlockSpec((LANES, DH_B), lambda b, h, i: (0, h))
    qspec = pl.BlockSpec((1, tq, DH_B), lambda b, h, i: (b, i, h))
    return pl.pallas_call(
        functools.partial(_fox_kernel, tq=tq),
        grid=(bsz, H_B, seq // tq),
        in_specs=[qspec, kvspec, kvspec, pl.BlockSpec((1, 1, 1, seq), lambda b, h, i: (b, h, 0, 0)),
                  mspec, mspec, pl.BlockSpec((1, 1, LANES), lambda b, h, i: (h, 0, 0))],
        out_specs=qspec,
        out_shape=jax.ShapeDtypeStruct((bsz, seq, W_A), BF16),
        scratch_shapes=[pltpu.VMEM((tq, 1), F32), pltpu.VMEM((tq, 1), F32), pltpu.VMEM((tq, DH_B), F32)],
        compiler_params=_cparams(("arbitrary", "arbitrary", "arbitrary")),
        name="fox_prompt",
    )(fq, fkb, fvb, c_t, kmeta, vmeta, cmeta)


def _fox_sample_kernel(pt_ref, *refs, pps):
    k_refs = refs[:pps]
    v_refs = refs[pps:2 * pps]
    lf_refs = refs[2 * pps:3 * pps]
    q_ref, kn_ref, vn_ref, sm_ref, wsuf_ref, o_ref, m_sc, l_sc, run_sc, acc_sc = refs[3 * pps:]
    j = pl.program_id(1)
    nj = pl.num_programs(1)
    hd = H_B * DH_B
    row = lax.broadcasted_iota(jnp.int32, (8, hd), 0)
    lane = lax.broadcasted_iota(jnp.int32, (8, hd), 1)
    own_block = (lane // DH_B) == row
    own_lane4 = (lane % H_B) == row
    qbd = jnp.where(own_block, jnp.broadcast_to(q_ref[0], (8, hd)), 0.0)

    @pl.when(j == 0)
    def _():
        r8 = lax.broadcasted_iota(jnp.int32, (8, LANES), 0)
        l8 = lax.broadcasted_iota(jnp.int32, (8, LANES), 1)
        lf_new = jnp.sum(jnp.where(l8 == L_LOGF + r8, jnp.broadcast_to(sm_ref[0], (8, LANES)), 0.0), -1, keepdims=True)
        run_sc[...] = jnp.broadcast_to(lf_new, (8, LANES))
        m_sc[...] = jnp.broadcast_to(jnp.sum(qbd * kn_ref[0], -1, keepdims=True), (8, LANES))
        l_sc[...] = jnp.ones((8, LANES), F32)
        acc_sc[...] = jnp.broadcast_to(vn_ref[0], (8, hd))

    qb = qbd.astype(BF16)
    run = run_sc[...][:, 0:1]
    scores = []
    for i in range(pps):
        lf = jnp.where(own_lane4, jnp.broadcast_to(lf_refs[i][0], (8, hd)), 0.0)
        within = jnp.dot(lf, wsuf_ref[...], precision=HIGHEST, preferred_element_type=F32)
        scores.append(_dot_nt(qb, k_refs[i][0].astype(BF16)) + within + run)
        run = run + jnp.sum(lf, -1, keepdims=True)
    run_sc[...] = jnp.broadcast_to(run, (8, LANES))
    s = jnp.concatenate(scores, 1)
    m_prev = m_sc[...][:, 0:1]
    m_new = jnp.maximum(m_prev, jnp.max(s, -1, keepdims=True))
    alpha = jnp.exp(m_prev - m_new)
    p = jnp.exp(s - m_new)
    l_sc[...] = jnp.broadcast_to(alpha * l_sc[...][:, 0:1] + jnp.sum(p, -1, keepdims=True), (8, LANES))
    m_sc[...] = jnp.broadcast_to(m_new, (8, LANES))
    acc = alpha * acc_sc[...]
    page = k_refs[0].shape[1]
    for i in range(pps):
        acc = acc + _dot(p[:, i * page:(i + 1) * page].astype(BF16), v_refs[i][0].astype(BF16))
    acc_sc[...] = acc

    @pl.when(j == nj - 1)
    def _():
        o = acc / l_sc[...][:, 0:1]
        o_ref[0] = jnp.sum(jnp.where(own_block, o, 0.0), 0, keepdims=True)


def _fox_sample_call(page_table, cache_k, cache_v, cache_logf, fq, fk, fv, small, *, pps):
    n, n_pages = page_table.shape
    n_pool, page = cache_k.shape[0], cache_k.shape[1]
    hd = H_B * DH_B
    ck = cache_k.reshape(n_pool, page, hd)
    cv = cache_v.reshape(n_pool, page, hd)
    clf = cache_logf.reshape(n_pool, 1, page * H_B)
    tok = lax.broadcasted_iota(jnp.int32, (page * H_B, page), 0) // H_B
    wsuf = (tok > lax.broadcasted_iota(jnp.int32, (page * H_B, page), 1)).astype(F32)

    def page_map(i):
        return lambda b, j, pt: (pt[b * n_pages + n_pages - 1 - (j * pps + i)], 0, 0)

    kv_specs = [pl.BlockSpec((1, page, hd), page_map(i)) for i in range(pps)]
    lf_specs = [pl.BlockSpec((1, 1, page * H_B), page_map(i)) for i in range(pps)]
    rspec = lambda width: pl.BlockSpec((1, 1, width), lambda b, j, pt: (b, 0, 0))
    row = lambda a: a.reshape(n, 1, a.shape[-1])
    grid_spec = pltpu.PrefetchScalarGridSpec(
        num_scalar_prefetch=1,
        grid=(n, n_pages // pps),
        in_specs=kv_specs + kv_specs + lf_specs + [rspec(hd), rspec(hd), rspec(hd), rspec(LANES),
                                                  pl.BlockSpec(wsuf.shape, lambda b, j, pt: (0, 0))],
        out_specs=rspec(hd),
        scratch_shapes=[pltpu.VMEM((8, LANES), F32), pltpu.VMEM((8, LANES), F32), pltpu.VMEM((8, LANES), F32),
                        pltpu.VMEM((8, hd), F32)],
    )
    o = pl.pallas_call(
        functools.partial(_fox_sample_kernel, pps=pps),
        grid_spec=grid_spec,
        out_shape=jax.ShapeDtypeStruct((n, 1, hd), F32),
        compiler_params=_cparams(("arbitrary", "arbitrary")),
        name="fox_sample",
    )(page_table.reshape(-1), *([ck] * pps), *([cv] * pps), *([clf] * pps), row(fq), row(fk), row(fv), row(small), wsuf)
    return o.reshape(n, hd)


def _back_kernel(x_ref, oa_ref, z_ref, ob_ref, gates_ref, lng_ref, lnb_ref, nw_ref, wbr_ref, wout_ref,
                 l1g_ref, l1b_ref, wr_ref, br_ref, h32_ref, hb_ref, comb_ref):
    tm = x_ref.shape[0]
    xn = _layer_norm(x_ref[...], lng_ref[...], lnb_ref[...])
    nw = nw_ref[...]
    parts = []
    for h in range(H_A):
        cols = slice(h * DV_A, (h + 1) * DV_A)
        o = oa_ref[:, cols].astype(F32)
        o = o * lax.rsqrt(jnp.mean(o * o, -1, keepdims=True) + RMS_EPS) * nw
        zz = z_ref[:, cols].astype(F32)
        parts.append((o * (zz * _sigmoid(zz))).astype(BF16))
    oa = jnp.concatenate(parts, 1)
    d = wout_ref.shape[0]
    merged = (_sigmoid(gates_ref[:, 0:d].astype(F32)) * _dot(oa, wbr_ref[0])
              + _sigmoid(gates_ref[:, d:2 * d].astype(F32)) * _dot(ob_ref[...].astype(BF16), wbr_ref[1]))
    mix = _dot(merged.astype(BF16), wout_ref[...])
    hh = _layer_norm(DEEP_ALPHA * xn + mix, l1g_ref[...], l1b_ref[...])
    h32_ref[...] = hh
    hb_ref[...] = hh.astype(BF16)

    logits = jnp.dot(hh, wr_ref[...], precision=HIGHEST, preferred_element_type=F32) + br_ref[...]
    lane = lax.broadcasted_iota(jnp.int32, (tm, LANES), 1)
    big = jnp.int32(LANES)
    is_grp = lane < L_EXP
    gl = jnp.where(is_grp, logits, NEG_INF)
    gmax = jnp.max(gl, -1, keepdims=True)
    gsel = jnp.min(jnp.where(jnp.logical_and(is_grp, gl == gmax), lane, big), -1, keepdims=True)
    gw = 1.0 / jnp.sum(jnp.exp(gl - gmax), -1, keepdims=True)
    lo = L_EXP + gsel * EXPERTS_PER_GROUP
    in_grp = jnp.logical_and(lane >= lo, lane < lo + EXPERTS_PER_GROUP)
    el = jnp.where(in_grp, logits, NEG_INF)
    m1 = jnp.max(el, -1, keepdims=True)
    i1 = jnp.min(jnp.where(el == m1, lane, big), -1, keepdims=True)
    el2 = jnp.where(lane == i1, NEG_INF, el)
    m2 = jnp.max(el2, -1, keepdims=True)
    i2 = jnp.min(jnp.where(el2 == m2, lane, big), -1, keepdims=True)
    e2 = jnp.exp(m2 - m1)
    w1 = gw / (1.0 + e2)
    w2 = gw * e2 / (1.0 + e2)
    comb_ref[...] = jnp.where(lane == i1, w1, jnp.where(lane == i2, w2, 0.0))


def _back_call(x, oa, z, ob, gates, lng, lnb, nw, wbr, wout, l1g, l1b, wr, br, *, tm):
    rows, d = x.shape
    tspec = lambda width: pl.BlockSpec((tm, width), lambda i: (i, 0))
    full = lambda a: pl.BlockSpec(a.shape, lambda i: (0,) * a.ndim)
    return pl.pallas_call(
        _back_kernel,
        grid=(rows // tm,),
        in_specs=[tspec(d), tspec(W_A), tspec(W_A), tspec(W_A), tspec(2 * d), full(lng), full(lnb), full(nw),
                  full(wbr), full(wout), full(l1g), full(l1b), full(wr), full(br)],
        out_specs=[tspec(d), tspec(d), tspec(LANES)],
        out_shape=[jax.ShapeDtypeStruct((rows, d), F32), jax.ShapeDtypeStruct((rows, d), BF16),
                   jax.ShapeDtypeStruct((rows, LANES), F32)],
        compiler_params=_cparams(("arbitrary",)),
        name="back",
    )(x, oa, z, ob, gates, lng, lnb, nw, wbr, wout, l1g, l1b, wr, br)


def _moe_kernel(hb_ref, h32_ref, comb_ref, wg_ref, wu_ref, wd_ref, l2g_ref, l2b_ref, y_ref, acc_ref):
    e = pl.program_id(1)
    ne = pl.num_programs(1)

    @pl.when(e == 0)
    def _():
        acc_ref[...] = jnp.zeros_like(acc_ref)

    x = hb_ref[...]
    g = _dot(x, wg_ref[0])
    hid = (g * _sigmoid(g)) * _dot(x, wu_ref[0])
    lane = lax.broadcasted_iota(jnp.int32, comb_ref.shape, 1)
    wgt = jnp.sum(jnp.where(lane == L_EXP + e, comb_ref[...], 0.0), -1, keepdims=True)
    acc_ref[...] += wgt * _dot(hid.astype(BF16), wd_ref[0])

    @pl.when(e == ne - 1)
    def _():
        y_ref[...] = _layer_norm(DEEP_ALPHA * h32_ref[...] + acc_ref[...], l2g_ref[...], l2b_ref[...])


def _moe_call(hb, h32, comb, wg, wu, wd, l2g, l2b, *, tm):
    rows, d = h32.shape
    ne, _, dff = wg.shape
    tspec = lambda width: pl.BlockSpec((tm, width), lambda i, e: (i, 0))
    full = lambda a: pl.BlockSpec(a.shape, lambda i, e: (0,) * a.ndim)
    return pl.pallas_call(
        _moe_kernel,
        grid=(rows // tm, ne),
        in_specs=[tspec(d), tspec(d), tspec(LANES),
                  pl.BlockSpec((1, d, dff), lambda i, e: (e, 0, 0)), pl.BlockSpec((1, d, dff), lambda i, e: (e, 0, 0)),
                  pl.BlockSpec((1, dff, d), lambda i, e: (e, 0, 0)), full(l2g), full(l2b)],
        out_specs=tspec(d),
        out_shape=jax.ShapeDtypeStruct((rows, d), F32),
        scratch_shapes=[pltpu.VMEM((tm, d), F32)],
        compiler_params=_cparams(("arbitrary", "arbitrary")),
        name="moe",
    )(hb, h32, comb, wg, wu, wd, l2g, l2b)


def _pick(n, prefs):
    for t in prefs:
        if n % t == 0:
            return t
    return n


def _rearranged_w_in(w_in):
    hk = H_A * DK_A
    widths = (hk, hk, W_A, W_A, H_A, H_A, W_A, W_A, W_A, H_B, w_in.shape[1] - (4 * hk + 2 * H_A + 3 * W_A + H_B))
    offs = [0]
    for wd in widths:
        offs.append(offs[-1] + wd)
    sl = lambda i: w_in[:, offs[i]:offs[i + 1]]
    gq, gk, gv, gz, ga, gb, fq, fk, fv, ff, gates = (sl(i) for i in range(11))
    pad = jnp.zeros((w_in.shape[0], LANES - 4 * H_A), w_in.dtype)
    return jnp.concatenate([gq, gk, gv, gz, fq, fk, fv, gates, ga, gb, ff, ff, pad], 1).astype(BF16)


def kernel(x_prompt, x_sample, cache_k, cache_v, cache_logf, state_gdn, state_conv, page_table, meta_tokens, ln_in_g, ln_in_b, w_in, conv_w, a_log, dt_bias, f_bias, gdn_norm_w, w_branch, w_out, ln1_g, ln1_b, w_router_group, b_router_group, w_router_expert, b_router_expert, w_gate, w_up, w_down, ln2_g, ln2_b):
    bsz, seq, d = x_prompt.shape
    n_dec = x_sample.shape[0]
    row2 = lambda a: a.reshape(1, -1).astype(F32)
    lng, lnb = row2(ln_in_g), row2(ln_in_b)
    w_big = _rearranged_w_in(w_in[0])
    convw = conv_w[0].astype(F32)
    zpad = lambda a, n: jnp.pad(a.astype(F32), (0, n - a.shape[0]))
    sp = jnp.zeros((8, LANES), F32)
    sp = sp.at[0].set(zpad(a_log[0], LANES)).at[1].set(zpad(dt_bias[0], LANES))
    sp = sp.at[2].set(zpad(jnp.concatenate([jnp.zeros((L_LOGF,), F32), f_bias[0], f_bias[0]]), LANES))

    meta = _front_call(meta_tokens.astype(F32)[None], lng, lnb, w_big, convw, jnp.zeros((1, 8, QKV), F32), sp,
                       jnp.zeros((1, LANES), F32), tm=N_META, emit_t=False)
    pad_rows = CHUNK_A - N_META
    front_pad = lambda a: jnp.pad(a, ((0, 0), (pad_rows, 0), (0, 0)))
    sm_meta = front_pad(meta["small"])
    _, s_meta = _gdn_call(front_pad(meta["qa"]), front_pad(meta["ka"]), front_pad(meta["va"]), sm_meta,
                          jnp.swapaxes(sm_meta, 1, 2)[:, :16], jnp.zeros((1, H_A, DK_A, DV_A), F32), lt=CHUNK_A)
    c_meta = meta["small"][0, :, L_C:L_C + H_B]
    c0 = jnp.zeros((1, LANES), F32).at[0, L_C:L_C + H_B].set(c_meta[-1])
    kmeta = jnp.pad(meta["fkb"][0], ((0, LANES - N_META), (0, 0)))
    vmeta = jnp.pad(meta["fvb"][0], ((0, LANES - N_META), (0, 0)))
    cmeta = jnp.full((H_B, 1, LANES), -NEG_INF, F32).at[:, 0, :N_META].set(c_meta.T)

    tm = _pick(seq, (512, 256, 128, 64))
    pf = _front_call(x_prompt, lng, lnb, w_big, convw, meta["tail"], sp, c0, tm=tm, emit_t=True)
    o_a, s_prompt = _gdn_call(pf["qa"], pf["ka"], pf["va"], pf["small"], pf["small_t"], s_meta,
                              lt=_pick(seq, (256, 128, 64)))
    c_t = pf["small_t"][:, L_C:L_C + H_B].reshape(bsz, H_B, 1, seq)
    o_b = _fox_call(pf["fq"], pf["fkb"], pf["fvb"], c_t, kmeta, vmeta, cmeta, tq=_pick(seq, (512, 256, 128)))

    nw = row2(gdn_norm_w[0])
    wbr = w_branch[0].astype(BF16)
    wout = w_out[0].astype(BF16)
    l1g, l1b, l2g, l2b = row2(ln1_g[0]), row2(ln1_b[0]), row2(ln2_g[0]), row2(ln2_b[0])
    wr = jnp.pad(jnp.concatenate([w_router_group[0], w_router_expert[0]], 1).astype(F32),
                 ((0, 0), (0, LANES - N_GROUPS - N_EXPERTS)))
    br = jnp.pad(jnp.concatenate([b_router_group[0], b_router_expert[0]]).astype(F32),
                 (0, LANES - N_GROUPS - N_EXPERTS)).reshape(1, LANES)
    wg, wu, wd = w_gate[0].astype(BF16), w_up[0].astype(BF16), w_down[0].astype(BF16)

    def layer_back(x2, oa2, z2, ob2, gates2, tile, moe_tile):
        h32, hb, comb = _back_call(x2, oa2, z2, ob2, gates2, lng, lnb, nw, wbr, wout, l1g, l1b, wr, br, tm=tile)
        return _moe_call(hb, h32, comb, wg, wu, wd, l2g, l2b, tm=moe_tile)

    flat = lambda a: a.reshape(bsz * seq, a.shape[-1])
    n_tok = bsz * seq
    y_prompt = layer_back(flat(x_prompt), flat(o_a), flat(pf["z"]), flat(o_b), flat(pf["gates"]),
                          _pick(n_tok, (512, 256, 128, 64)), _pick(n_tok, (1024, 512, 256, 128, 64))).reshape(bsz, seq, d)

    st = jnp.swapaxes(state_conv[0].astype(F32), 0, 1)
    sf = _sample_front_call(x_sample.reshape(n_dec, d), lng, lnb, w_big, convw, st, sp)
    o_a_s, s_sample = _gdn_step_call(sf["q"], sf["k"], sf["v"], sf["small"], state_gdn[0].astype(F32))
    n_pages = page_table.shape[1]
    o_b_s = _fox_sample_call(page_table, cache_k[0], cache_v[0], cache_logf[0], sf["fq"], sf["fk"], sf["fv"],
                             sf["small"], pps=_pick(n_pages, (8, 4, 2, 1)))
    y_sample = layer_back(x_sample.reshape(n_dec, d), o_a_s, sf["z"], o_b_s, sf["gates"], n_dec, n_dec)

    heads = lambda a: a.reshape(a.shape[:-1] + (H_B, DH_B))
    with_meta = lambda m, p: jnp.concatenate([jnp.broadcast_to(m, (bsz,) + m.shape[1:]), p], 1)
    k_prompt = heads(with_meta(meta["fk32"], pf["fk32"]))[None]
    v_prompt = heads(with_meta(meta["fv32"], pf["fv32"]))[None]
    logf_prompt = with_meta(meta["small"][:, :, L_LOGF:L_LOGF + H_B], pf["small"][:, :, L_LOGF:L_LOGF + H_B])[None]
    conv_prompt = pf["tail"][:, 8 - (CONV_W - 1):][None]
    conv_sample = jnp.concatenate([state_conv[0][:, 1:].astype(F32), sf["new"][:, None]], 1)[None]
    return (y_prompt, y_sample.reshape(n_dec, 1, d), k_prompt, v_prompt, logf_prompt, s_prompt[None], conv_prompt,
            heads(sf["fk"])[None, :, None], heads(sf["fv"])[None, :, None],
            sf["small"][:, L_LOGF:L_LOGF + H_B][None, :, None], s_sample[None], conv_sample)
```

```python
import functools

import jax
import jax.numpy as jnp
from jax import lax
from jax.experimental import pallas as pl
from jax.experimental.pallas import tpu as pltpu

F32 = jnp.float32
BF16 = jnp.bfloat16
HIGHEST = lax.Precision.HIGHEST

N_META = 16
H_A = 4
DK_A = 128
DV_A = 128
CONV_W = 4
CHUNK_A = 64
H_B = 4
DH_B = 128
N_GROUPS = 4
EXPERTS_PER_GROUP = 4
N_EXPERTS = N_GROUPS * EXPERTS_PER_GROUP
D_FF_E = 512
DEPTH = 1
DEEP_ALPHA = (2 * DEPTH) ** 0.25
LN_EPS = 1e-5
RMS_EPS = 1e-6
NEG_INF = -1e30

LANES = 128
W_A = H_A * DV_A
QKV = 3 * W_A
C_QKV = 0
C_Z = C_QKV + QKV
C_FQ = C_Z + W_A
C_FK = C_FQ + W_A
C_FV = C_FK + W_A
C_GATES = C_FV + W_A
C_SMALL = C_GATES + 2 * 1024
P_COLS = C_SMALL + LANES
L_G, L_BETA, L_LOGF, L_C = 0, 4, 8, 12
L_GRP, L_EXP = 0, 4

VMEM_LIMIT = 56 * 1024 * 1024


def _cparams(sem, vmem=VMEM_LIMIT):
    return pltpu.CompilerParams(dimension_semantics=sem, vmem_limit_bytes=vmem)


def _layer_norm(x, g, b):
    mu = jnp.mean(x, -1, keepdims=True)
    xc = x - mu
    var = jnp.mean(xc * xc, -1, keepdims=True)
    return xc * lax.rsqrt(var + LN_EPS) * g + b


def _softplus(x):
    return jnp.maximum(x, 0.0) + jnp.log1p(jnp.exp(-jnp.abs(x)))


def _sigmoid(x):
    return 1.0 / (1.0 + jnp.exp(-x))


def _dot(a, b):
    return jnp.dot(a, b, preferred_element_type=F32)


def _dot_nt(a, b):
    return lax.dot_general(a, b, (((1,), (1,)), ((), ())), preferred_element_type=F32)


def _dot_tn(a, b):
    return lax.dot_general(a, b, (((0,), (0,)), ((), ())), preferred_element_type=F32)


def _gate_values(sm, sp, rows):
    lane = lax.broadcasted_iota(jnp.int32, (rows, LANES), 1)
    g = -jnp.exp(sp[0:1]) * _softplus(sm + sp[1:2])
    beta = _sigmoid(sm)
    logf = -_softplus(-(sm + sp[2:3]))
    return jnp.where(lane < L_BETA, g, jnp.where(lane < L_LOGF, beta, jnp.where(lane < L_C + 4, logf, 0.0)))


def _front_kernel(x_ref, lng_ref, lnb_ref, w_ref, convw_ref, halo_ref, sp_ref, c0_ref,
                  qa_ref, ka_ref, va_ref, z_ref, fq_ref, fk32_ref, fv32_ref, fkb_ref, fvb_ref,
                  gates_ref, small_ref, smallt_ref, tail_ref, buf_ref, carry_ref, *, tm, blk):
    j = pl.program_id(1)
    nj = pl.num_programs(1)

    @pl.when(j == 0)
    def _():
        buf_ref[0:8, :] = halo_ref[0]
        carry_ref[...] = c0_ref[...]

    xn = _layer_norm(x_ref[0], lng_ref[...], lnb_ref[...]).astype(BF16)

    def proj(c0, width):
        return _dot(xn, w_ref[:, c0:c0 + width])

    buf_ref[8:8 + tm, :] = proj(C_QKV, QKV)
    outs = (qa_ref, ka_ref, va_ref)
    for grp in range(QKV // LANES):
        c = grp * LANES
        w = convw_ref[:, c:c + LANES]
        y = (buf_ref[5:5 + tm, c:c + LANES] * w[0:1] + buf_ref[6:6 + tm, c:c + LANES] * w[1:2]
             + buf_ref[7:7 + tm, c:c + LANES] * w[2:3] + buf_ref[8:8 + tm, c:c + LANES] * w[3:4])
        y = y * _sigmoid(y)
        which, head = divmod(grp, H_A)
        if which < 2:
            y = y * lax.rsqrt(jnp.sum(y * y, -1, keepdims=True) + RMS_EPS)
            if which == 0:
                y = y * DK_A ** -0.5
        outs[which][0, :, head * LANES:(head + 1) * LANES] = y.astype(BF16)

    @pl.when(j == nj - 1)
    def _():
        tail_ref[0] = buf_ref[tm:tm + 8, :]

    buf_ref[0:8, :] = buf_ref[tm:tm + 8, :]

    z_ref[0] = proj(C_Z, W_A).astype(BF16)
    fq_ref[0] = (proj(C_FQ, W_A) * DH_B ** -0.5).astype(BF16)
    fk = proj(C_FK, W_A)
    fk32_ref[0] = fk
    fkb_ref[0] = fk.astype(BF16)
    fv = proj(C_FV, W_A)
    fv32_ref[0] = fv
    fvb_ref[0] = fv.astype(BF16)
    gates_ref[0] = proj(C_GATES, 2 * 1024).astype(BF16)

    vals = _gate_values(proj(C_SMALL, LANES), sp_ref[...], tm)
    span = min(LANES, tm)
    r = lax.broadcasted_iota(jnp.int32, (span, span), 0)
    c = lax.broadcasted_iota(jnp.int32, (span, span), 1)
    same_blk = c >= (r // blk) * blk
    tri = jnp.where(c <= r, jnp.where(same_blk, 1.0, 0.0), 0.0)
    parts = [jnp.dot(tri, vals[i * span:(i + 1) * span], precision=HIGHEST, preferred_element_type=F32)
             for i in range(tm // span)]
    cs = parts[0] if len(parts) == 1 else jnp.concatenate(parts, 0)
    run = carry_ref[...]
    pieces = []
    for b in range(tm // blk):
        piece = cs[b * blk:(b + 1) * blk, :] + run
        pieces.append(piece)
        run = piece[blk - 1:blk, :]
    carry_ref[...] = run
    cfull = pieces[0] if len(pieces) == 1 else jnp.concatenate(pieces, 0)
    lane = lax.broadcasted_iota(jnp.int32, (tm, LANES), 1)
    small = jnp.where(lane < L_BETA, cs, jnp.where(lane < L_C, vals, cfull))
    small_ref[0] = small
    if smallt_ref is not None:
        smallt_ref[0] = small.T[0:16, :]


def _front_call(x, lng, lnb, w_big, conv_w, halo, sp, c0, *, tm, emit_t):
    bsz, seq, d = x.shape
    blk = min(CHUNK_A, tm)
    nj = seq // tm
    tok = lambda width, dt: jax.ShapeDtypeStruct((bsz, seq, width), dt)
    tspec = lambda width: pl.BlockSpec((1, tm, width), lambda b, j: (b, j, 0))
    full2 = lambda a: pl.BlockSpec(a.shape, lambda b, j: (0, 0))
    out_shape = [tok(W_A, BF16), tok(W_A, BF16), tok(W_A, BF16), tok(W_A, BF16), tok(W_A, BF16),
                 tok(W_A, F32), tok(W_A, F32), tok(W_A, BF16), tok(W_A, BF16),
                 tok(2 * 1024, BF16), tok(LANES, F32)]
    out_specs = [tspec(W_A)] * 9 + [tspec(2 * 1024), tspec(LANES)]
    if emit_t:
        out_shape.append(jax.ShapeDtypeStruct((bsz, 16, seq), F32))
        out_specs.append(pl.BlockSpec((1, 16, tm), lambda b, j: (b, 0, j)))
    out_shape.append(jax.ShapeDtypeStruct((bsz, 8, QKV), F32))
    out_specs.append(pl.BlockSpec((1, 8, QKV), lambda b, j: (b, 0, 0)))

    def body(*refs):
        ins, rest = refs[:8], refs[8:]
        if emit_t:
            outs, scratch = rest[:13], rest[13:]
            return _front_kernel(*ins, *outs, *scratch, tm=tm, blk=blk)
        outs, scratch = rest[:12], rest[12:]
        return _front_kernel(*ins, *outs[:11], None, outs[11], *scratch, tm=tm, blk=blk)

    res = pl.pallas_call(
        body,
        grid=(bsz, nj),
        in_specs=[pl.BlockSpec((1, tm, d), lambda b, j: (b, j, 0)), full2(lng), full2(lnb),
                  pl.BlockSpec(w_big.shape, lambda b, j: (0, 0), pipeline_mode=pl.Buffered(1)),
                  full2(conv_w), pl.BlockSpec((1, 8, QKV), lambda b, j: (0, 0, 0)), full2(sp), full2(c0)],
        out_specs=out_specs,
        out_shape=out_shape,
        scratch_shapes=[pltpu.VMEM((tm + 8, QKV), F32), pltpu.VMEM((1, LANES), F32)],
        compiler_params=_cparams(("arbitrary", "arbitrary")),
        name="front",
    )(x, lng, lnb, w_big, conv_w, halo, sp, c0)
    names = ["qa", "ka", "va", "z", "fq", "fk32", "fv32", "fkb", "fvb", "gates", "small"]
    names += (["small_t"] if emit_t else []) + ["tail"]
    return dict(zip(names, res))


def _sample_front_kernel(x_ref, lng_ref, lnb_ref, w_ref, convw_ref, st_ref, sp_ref,
                         q_ref, k_ref, v_ref, z_ref, fq_ref, fk_ref, fv_ref, gates_ref, small_ref, new_ref):
    rows = x_ref.shape[0]
    xn = _layer_norm(x_ref[...], lng_ref[...], lnb_ref[...]).astype(BF16)

    def proj(c0, width):
        return _dot(xn, w_ref[:, c0:c0 + width])

    new = proj(C_QKV, QKV)
    new_ref[...] = new
    w = convw_ref[...]
    y = st_ref[0] * w[0:1] + st_ref[1] * w[1:2] + st_ref[2] * w[2:3] + new * w[3:4]
    y = y * _sigmoid(y)
    outs = (q_ref, k_ref, v_ref)
    for grp in range(QKV // LANES):
        which, head = divmod(grp, H_A)
        yy = y[:, grp * LANES:(grp + 1) * LANES]
        if which < 2:
            yy = yy * lax.rsqrt(jnp.sum(yy * yy, -1, keepdims=True) + RMS_EPS)
            if which == 0:
                yy = yy * DK_A ** -0.5
        outs[which][:, head * LANES:(head + 1) * LANES] = yy
    z_ref[...] = proj(C_Z, W_A)
    fq_ref[...] = proj(C_FQ, W_A) * DH_B ** -0.5
    fk_ref[...] = proj(C_FK, W_A)
    fv_ref[...] = proj(C_FV, W_A)
    gates_ref[...] = proj(C_GATES, 2 * 1024)
    small_ref[...] = _gate_values(proj(C_SMALL, LANES), sp_ref[...], rows)


def _sample_front_call(x, lng, lnb, w_big, conv_w, st, sp):
    rows = x.shape[0]
    o = lambda width: jax.ShapeDtypeStruct((rows, width), F32)
    res = pl.pallas_call(
        _sample_front_kernel,
        out_shape=[o(W_A), o(W_A), o(W_A), o(W_A), o(W_A), o(W_A), o(W_A), o(2 * 1024), o(LANES), o(QKV)],
        compiler_params=pltpu.CompilerParams(vmem_limit_bytes=VMEM_LIMIT),
        name="sample_front",
    )(x, lng, lnb, w_big, conv_w, st, sp)
    return dict(zip(["q", "k", "v", "z", "fq", "fk", "fv", "gates", "small", "new"], res))


def _gdn_kernel(q_ref, k_ref, v_ref, sm_ref, smt_ref, s0_ref, o_ref, sout_ref, state_ref, *, lt):
    j = pl.program_id(0)
    nj = pl.num_programs(0)
    ch = CHUNK_A
    bsz = q_ref.shape[0]

    @pl.when(j == 0)
    def _():
        for b in range(bsz):
            state_ref[b] = s0_ref[0]

    ri = lax.broadcasted_iota(jnp.int32, (ch, ch), 0)
    ci = lax.broadcasted_iota(jnp.int32, (ch, ch), 1)
    incl = ri >= ci
    strict = ri > ci
    units = [(b, c, h) for b in range(bsz) for c in range(lt // ch) for h in range(H_A)]
    t = {}
    for u in units:
        b, c, h = u
        rows = slice(c * ch, (c + 1) * ch)
        cols = slice(h * DK_A, (h + 1) * DK_A)
        kc = k_ref[b, rows, cols]
        qc = q_ref[b, rows, cols]
        gc_col = sm_ref[b, rows, L_G + h:L_G + h + 1]
        beta = sm_ref[b, rows, L_BETA + h:L_BETA + h + 1]
        gc_row = smt_ref[b, L_G + h:L_G + h + 1, rows]
        gc_last = sm_ref[b, c * ch + ch - 1:c * ch + ch, L_G + h:L_G + h + 1]
        eg = jnp.exp(gc_col)
        kf = kc.astype(F32)
        kb = kf * beta
        t[u] = dict(kc=kc, decay=jnp.exp(jnp.where(incl, gc_col - gc_row, NEG_INF)),
                    lhs=jnp.concatenate([kb.astype(BF16), qc], 0),
                    rhs=jnp.concatenate([v_ref[b, rows, cols].astype(F32) * beta, kb * eg], 1),
                    q_dec=(qc.astype(F32) * eg).astype(BF16),
                    k_dec=(kf * jnp.exp(gc_last - gc_col)).astype(BF16), g_tot=jnp.exp(gc_last))
    for u in units:
        d = t[u]
        a = _dot_nt(d["lhs"], d["kc"])
        d["m"] = jnp.where(strict, a[:ch] * d["decay"], 0.0)
        d["attn"] = (a[ch:] * d["decay"]).astype(BF16)
    for u in units:
        d = t[u]
        mb = d["m"].astype(BF16)
        d["x"] = -d["m"]
        d["p"] = _dot(mb, mb)
    for _ in range(4):
        for u in units:
            d = t[u]
            pb = d["p"].astype(BF16)
            r = _dot(jnp.concatenate([pb, d["x"].astype(BF16)], 0), pb)
            d["x"] = d["x"] + d["p"] + r[ch:]
            d["p"] = r[:ch]
    for u in units:
        d = t[u]
        d["x"] = d["x"] + d["p"] + _dot(d["x"].astype(BF16), d["p"].astype(BF16))
    for u in units:
        d = t[u]
        sol = d["rhs"] + _dot(d["x"].astype(BF16), d["rhs"].astype(BF16))
        d["u"] = sol[:, :DV_A]
        d["wq"] = jnp.concatenate([sol[:, DV_A:].astype(BF16), d["q_dec"]], 0)
    heads = [(b, h) for b in range(bsz) for h in range(H_A)]
    s = {bh: state_ref[bh[0], bh[1]] for bh in heads}
    for c in range(lt // ch):
        ws = {bh: _dot(t[bh[0], c, bh[1]]["wq"], s[bh].astype(BF16)) for bh in heads}
        v_new = {bh: (t[bh[0], c, bh[1]]["u"] - ws[bh][:ch]).astype(BF16) for bh in heads}
        for bh in heads:
            b, h = bh
            d = t[b, c, h]
            o = ws[bh][ch:] + _dot(d["attn"], v_new[bh])
            o_ref[b, c * ch:(c + 1) * ch, h * DV_A:(h + 1) * DV_A] = o.astype(o_ref.dtype)
        for bh in heads:
            d = t[bh[0], c, bh[1]]
            s[bh] = s[bh] * d["g_tot"] + _dot_tn(d["k_dec"], v_new[bh])
    for bh in heads:
        state_ref[bh[0], bh[1]] = s[bh]

    @pl.when(j == nj - 1)
    def _():
        sout_ref[...] = state_ref[...]


def _gdn_call(q, k, v, small, small_t, s0, *, lt):
    bsz, seq, _ = q.shape
    nj = seq // lt
    tspec = pl.BlockSpec((bsz, lt, W_A), lambda j: (0, j, 0))
    sshape = (bsz, H_A, DK_A, DV_A)
    return pl.pallas_call(
        functools.partial(_gdn_kernel, lt=lt),
        grid=(nj,),
        in_specs=[tspec, tspec, tspec, pl.BlockSpec((bsz, lt, LANES), lambda j: (0, j, 0)),
                  pl.BlockSpec((bsz, 16, lt), lambda j: (0, 0, j)),
                  pl.BlockSpec((1, H_A, DK_A, DV_A), lambda j: (0, 0, 0, 0))],
        out_specs=[tspec, pl.BlockSpec(sshape, lambda j: (0, 0, 0, 0))],
        out_shape=[jax.ShapeDtypeStruct((bsz, seq, W_A), BF16), jax.ShapeDtypeStruct(sshape, F32)],
        scratch_shapes=[pltpu.VMEM(sshape, F32)],
        compiler_params=_cparams(("arbitrary",)),
        name="gdn_chunk",
    )(q, k, v, small, small_t, s0)


def _gdn_step_kernel(q_ref, k_ref, v_ref, sm_ref, s_ref, o_ref, sout_ref):
    ri = lax.broadcasted_iota(jnp.int32, (DK_A, DK_A), 0)
    ci = lax.broadcasted_iota(jnp.int32, (DK_A, DK_A), 1)
    eye = ri == ci

    def column(row):
        return jnp.sum(jnp.where(eye, jnp.broadcast_to(row, (DK_A, DK_A)), 0.0), -1, keepdims=True)

    sm = sm_ref[0]
    for h in range(H_A):
        cols = slice(h * DK_A, (h + 1) * DK_A)
        kcol = column(k_ref[0, :, cols])
        qcol = column(q_ref[0, :, cols])
        s = s_ref[0, h] * jnp.exp(sm[:, L_G + h:L_G + h + 1])
        v_t = (v_ref[0, :, cols] - jnp.sum(kcol * s, 0, keepdims=True)) * sm[:, L_BETA + h:L_BETA + h + 1]
        s = s + kcol * v_t
        sout_ref[0, h] = s
        o_ref[0, :, cols] = jnp.sum(qcol * s, 0, keepdims=True)


def _gdn_step_call(q, k, v, small, state):
    n = q.shape[0]
    row = lambda a: a.reshape(n, 1, a.shape[-1])
    rspec = lambda width: pl.BlockSpec((1, 1, width), lambda b: (b, 0, 0))
    sspec = pl.BlockSpec((1, H_A, DK_A, DV_A), lambda b: (b, 0, 0, 0))
    o, s = pl.pallas_call(
        _gdn_step_kernel,
        grid=(n,),
        in_specs=[rspec(W_A), rspec(W_A), rspec(W_A), rspec(LANES), sspec],
        out_specs=[rspec(W_A), sspec],
        out_shape=[jax.ShapeDtypeStruct((n, 1, W_A), F32), jax.ShapeDtypeStruct(state.shape, F32)],
        compiler_params=_cparams(("arbitrary",)),
        name="gdn_step",
    )(row(q), row(k), row(v), row(small), state)
    return o.reshape(n, W_A), s


def _fox_kernel(q_ref, k_ref, v_ref, ct_ref, km_ref, vm_ref, cm_ref, o_ref, m_sc, l_sc, acc_sc, *, tq):
    qi = pl.program_id(1)
    qstart = pl.multiple_of(qi * tq, tq)
    heads = range(H_B)
    hs = lambda h: slice(h * DH_B, (h + 1) * DH_B)
    q = [q_ref[0, :, hs(h)] for h in heads]
    c_first = [ct_ref[0, h, :, pl.ds(qstart, LANES)][:, 0:1] for h in heads]

    s = [_dot_nt(q[h], km_ref[:, hs(h)]) + (c_first[h] - cm_ref[h]) for h in heads]
    for h in heads:
        m0 = jnp.max(s[h], -1, keepdims=True)
        p = jnp.exp(s[h] - m0)
        m_sc[h] = m0
        l_sc[h] = jnp.sum(p, -1, keepdims=True)
        s[h] = p.astype(BF16)
    for h in heads:
        acc_sc[h] = _dot(s[h], vm_ref[:, hs(h)])

    def update(start, masked):
        s = [_dot_nt(q[h], k_ref[0, pl.ds(start, tq), hs(h)]) + (c_first[h] - ct_ref[0, h, :, pl.ds(start, tq)])
             for h in heads]
        p, alpha = [], []
        for h in heads:
            sh = s[h]
            if masked:
                ri = lax.broadcasted_iota(jnp.int32, (tq, tq), 0)
                ci = lax.broadcasted_iota(jnp.int32, (tq, tq), 1)
                sh = jnp.where(ci <= ri, sh, NEG_INF)
            m_prev = m_sc[h]
            m_new = jnp.maximum(m_prev, jnp.max(sh, -1, keepdims=True))
            a = jnp.exp(m_prev - m_new)
            ph = jnp.exp(sh - m_new)
            l_sc[h] = a * l_sc[h] + jnp.sum(ph, -1, keepdims=True)
            m_sc[h] = m_new
            p.append(ph.astype(BF16))
            alpha.append(a)
        for h in heads:
            acc_sc[h] = alpha[h] * acc_sc[h] + _dot(p[h], v_ref[0, pl.ds(start, tq), hs(h)])

    def body(jb, carry):
        update(pl.multiple_of(jb * tq, tq), False)
        return carry

    lax.fori_loop(0, qi, body, 0)
    update(qstart, True)
    for h in heads:
        o_ref[0, :, hs(h)] = (acc_sc[h] / l_sc[h]).astype(o_ref.dtype)


def _fox_call(fq, fkb, fvb, c_t, kmeta, vmeta, cmeta, *, tq):
    bsz, seq, hd = fq.shape
    kvspec = pl.BlockSpec((1, seq, hd), lambda b, i: (b, 0, 0), pipeline_mode=pl.Buffered(1))
    mspec = pl.BlockSpec((LANES, hd), lambda b, i: (0, 0))
    qspec = pl.BlockSpec((1, tq, hd), lambda b, i: (b, i, 0))
    return pl.pallas_call(
        functools.partial(_fox_kernel, tq=tq),
        grid=(bsz, seq // tq),
        in_specs=[qspec, kvspec, kvspec, pl.BlockSpec((1, H_B, 1, seq), lambda b, i: (b, 0, 0, 0)),
                  mspec, mspec, pl.BlockSpec((H_B, 1, LANES), lambda b, i: (0, 0, 0))],
        out_specs=qspec,
        out_shape=jax.ShapeDtypeStruct((bsz, seq, hd), BF16),
        scratch_shapes=[pltpu.VMEM((H_B, tq, 1), F32), pltpu.VMEM((H_B, tq, 1), F32), pltpu.VMEM((H_B, tq, DH_B), F32)],
        compiler_params=_cparams(("arbitrary", "arbitrary")),
        name="fox_prompt",
    )(fq, fkb, fvb, c_t, kmeta, vmeta, cmeta)


def _fox_sample_kernel(pt_ref, *refs, pps):
    k_refs = refs[:pps]
    v_refs = refs[pps:2 * pps]
    lf_refs = refs[2 * pps:3 * pps]
    q_ref, kn_ref, vn_ref, lfn_ref, wsuf_ref, o_ref, m_sc, l_sc, run_sc, acc_sc = refs[3 * pps:]
    j = pl.program_id(1)
    nj = pl.num_programs(1)
    rows = k_refs[0].shape[1]
    q8 = q_ref[0]

    @pl.when(j == 0)
    def _():
        run_sc[...] = lfn_ref[0]
        m_sc[...] = jnp.broadcast_to(jnp.sum(q8 * kn_ref[0], -1, keepdims=True), (8, LANES))
        l_sc[...] = jnp.ones((8, LANES), F32)
        acc_sc[...] = vn_ref[0]

    qb = q8.astype(BF16)
    lf = jnp.concatenate([lf_refs[i][0] for i in range(pps)], 0)
    within = jnp.dot(lf, wsuf_ref[...], precision=HIGHEST, preferred_element_type=F32)
    tot = jnp.broadcast_to(jnp.sum(lf, -1, keepdims=True), (8 * pps, LANES))
    run = run_sc[...]
    valid = (lax.broadcasted_iota(jnp.int32, (8, rows), 1) % H_B) == lax.broadcasted_iota(jnp.int32, (8, rows), 0)
    scores = []
    for i in range(pps):
        sl = slice(8 * i, 8 * i + 8)
        bias = within[sl] + run[:, 0:1]
        scores.append(jnp.where(valid, _dot_nt(qb, k_refs[i][0].astype(BF16)) + bias, NEG_INF))
        run = run + tot[sl]
    run_sc[...] = run
    s = jnp.concatenate(scores, 1)
    m_prev = m_sc[...][:, 0:1]
    m_new = jnp.maximum(m_prev, jnp.max(s, -1, keepdims=True))
    alpha = jnp.exp(m_prev - m_new)
    p = jnp.exp(s - m_new)
    l_sc[...] = jnp.broadcast_to(alpha * l_sc[...][:, 0:1] + jnp.sum(p, -1, keepdims=True), (8, LANES))
    m_sc[...] = jnp.broadcast_to(m_new, (8, LANES))
    acc = alpha * acc_sc[...]
    for i in range(pps):
        acc = acc + _dot(p[:, i * rows:(i + 1) * rows].astype(BF16), v_refs[i][0].astype(BF16))
    acc_sc[...] = acc

    @pl.when(j == nj - 1)
    def _():
        o_ref[0] = acc / l_sc[...][:, 0:1]


def _fox_sample_call(page_table, cache_k, cache_v, cache_logf, fq, fk, fv, logf_new, *, pps):
    n, n_pages = page_table.shape
    n_pool, page = cache_k.shape[0], cache_k.shape[1]
    rows = page * H_B
    ck = cache_k.reshape(n_pool, rows, DH_B)
    cv = cache_v.reshape(n_pool, rows, DH_B)
    clf = jnp.pad(jnp.swapaxes(cache_logf, 1, 2), ((0, 0), (0, 8 - H_B), (0, 0)))
    tile8 = lambda a: jnp.pad(a.reshape(n, H_B, DH_B), ((0, 0), (0, 8 - H_B), (0, 0)))
    lfn = jnp.broadcast_to(jnp.pad(logf_new, ((0, 0), (0, 8 - H_B)))[:, :, None], (n, 8, LANES))
    wsuf = (lax.broadcasted_iota(jnp.int32, (page, rows), 0)
            > lax.broadcasted_iota(jnp.int32, (page, rows), 1) // H_B).astype(F32)

    def page_map(i):
        return lambda b, j, pt: (pt[b * n_pages + n_pages - 1 - (j * pps + i)], 0, 0)

    kv_specs = [pl.BlockSpec((1, rows, DH_B), page_map(i)) for i in range(pps)]
    lf_specs = [pl.BlockSpec((1, 8, page), page_map(i)) for i in range(pps)]
    tspec = pl.BlockSpec((1, 8, LANES), lambda b, j, pt: (b, 0, 0))
    full = lambda a: pl.BlockSpec(a.shape, lambda b, j, pt: (0, 0))
    grid_spec = pltpu.PrefetchScalarGridSpec(
        num_scalar_prefetch=1,
        grid=(n, n_pages // pps),
        in_specs=kv_specs + kv_specs + lf_specs + [tspec, tspec, tspec, tspec, full(wsuf)],
        out_specs=tspec,
        scratch_shapes=[pltpu.VMEM((8, LANES), F32)] * 4,
    )
    o = pl.pallas_call(
        functools.partial(_fox_sample_kernel, pps=pps),
        grid_spec=grid_spec,
        out_shape=jax.ShapeDtypeStruct((n, 8, DH_B), F32),
        compiler_params=_cparams(("arbitrary", "arbitrary")),
        name="fox_sample",
    )(page_table.reshape(-1), *([ck] * pps), *([cv] * pps), *([clf] * pps), tile8(fq), tile8(fk), tile8(fv), lfn,
      wsuf)
    return o[:, :H_B].reshape(n, H_B * DH_B)


def _back_kernel(x_ref, oa_ref, z_ref, ob_ref, gates_ref, lng_ref, lnb_ref, nw_ref, wbr_ref, wout_ref,
                 l1g_ref, l1b_ref, wr_ref, br_ref, h32_ref, hb_ref, comb_ref):
    tm = x_ref.shape[0]
    xn = _layer_norm(x_ref[...], lng_ref[...], lnb_ref[...])
    nw = nw_ref[...]
    parts = []
    for h in range(H_A):
        cols = slice(h * DV_A, (h + 1) * DV_A)
        o = oa_ref[:, cols].astype(F32)
        o = o * lax.rsqrt(jnp.mean(o * o, -1, keepdims=True) + RMS_EPS) * nw
        zz = z_ref[:, cols].astype(F32)
        parts.append((o * (zz * _sigmoid(zz))).astype(BF16))
    oa = jnp.concatenate(parts, 1)
    d = wout_ref.shape[0]
    merged = (_sigmoid(gates_ref[:, 0:d].astype(F32)) * _dot(oa, wbr_ref[0])
              + _sigmoid(gates_ref[:, d:2 * d].astype(F32)) * _dot(ob_ref[...].astype(BF16), wbr_ref[1]))
    mix = _dot(merged.astype(BF16), wout_ref[...])
    hh = _layer_norm(DEEP_ALPHA * xn + mix, l1g_ref[...], l1b_ref[...])
    h32_ref[...] = hh
    hb_ref[...] = hh.astype(BF16)

    logits = jnp.dot(hh, wr_ref[...], precision=HIGHEST, preferred_element_type=F32) + br_ref[...]
    lane = lax.broadcasted_iota(jnp.int32, (tm, LANES), 1)
    big = jnp.int32(LANES)
    is_grp = lane < L_EXP
    gl = jnp.where(is_grp, logits, NEG_INF)
    gmax = jnp.max(gl, -1, keepdims=True)
    gsel = jnp.min(jnp.where(jnp.logical_and(is_grp, gl == gmax), lane, big), -1, keepdims=True)
    gw = 1.0 / jnp.sum(jnp.exp(gl - gmax), -1, keepdims=True)
    lo = L_EXP + gsel * EXPERTS_PER_GROUP
    in_grp = jnp.logical_and(lane >= lo, lane < lo + EXPERTS_PER_GROUP)
    el = jnp.where(in_grp, logits, NEG_INF)
    m1 = jnp.max(el, -1, keepdims=True)
    i1 = jnp.min(jnp.where(el == m1, lane, big), -1, keepdims=True)
    el2 = jnp.where(lane == i1, NEG_INF, el)
    m2 = jnp.max(el2, -1, keepdims=True)
    i2 = jnp.min(jnp.where(el2 == m2, lane, big), -1, keepdims=True)
    e2 = jnp.exp(m2 - m1)
    w1 = gw / (1.0 + e2)
    w2 = gw * e2 / (1.0 + e2)
    comb_ref[...] = jnp.where(lane == i1, w1, jnp.where(lane == i2, w2, 0.0))


def _back_call(x, oa, z, ob, gates, lng, lnb, nw, wbr, wout, l1g, l1b, wr, br, *, tm):
    rows, d = x.shape
    tspec = lambda width: pl.BlockSpec((tm, width), lambda i: (i, 0))
    full = lambda a: pl.BlockSpec(a.shape, lambda i: (0,) * a.ndim)
    return pl.pallas_call(
        _back_kernel,
        grid=(rows // tm,),
        in_specs=[tspec(d), tspec(W_A), tspec(W_A), tspec(W_A), tspec(2 * d), full(lng), full(lnb), full(nw),
                  full(wbr), full(wout), full(l1g), full(l1b), full(wr), full(br)],
        out_specs=[tspec(d), tspec(d), tspec(LANES)],
        out_shape=[jax.ShapeDtypeStruct((rows, d), F32), jax.ShapeDtypeStruct((rows, d), BF16),
                   jax.ShapeDtypeStruct((rows, LANES), F32)],
        compiler_params=_cparams(("arbitrary",)),
        name="back",
    )(x, oa, z, ob, gates, lng, lnb, nw, wbr, wout, l1g, l1b, wr, br)


def _moe_kernel(hb_ref, h32_ref, comb_ref, wg_ref, wu_ref, wd_ref, l2g_ref, l2b_ref, y_ref, acc_ref):
    e = pl.program_id(1)
    ne = pl.num_programs(1)

    @pl.when(e == 0)
    def _():
        acc_ref[...] = jnp.zeros_like(acc_ref)

    x = hb_ref[...]
    g = _dot(x, wg_ref[0])
    hid = (g * _sigmoid(g)) * _dot(x, wu_ref[0])
    lane = lax.broadcasted_iota(jnp.int32, comb_ref.shape, 1)
    wgt = jnp.sum(jnp.where(lane == L_EXP + e, comb_ref[...], 0.0), -1, keepdims=True)
    acc_ref[...] += wgt * _dot(hid.astype(BF16), wd_ref[0])

    @pl.when(e == ne - 1)
    def _():
        y_ref[...] = _layer_norm(DEEP_ALPHA * h32_ref[...] + acc_ref[...], l2g_ref[...], l2b_ref[...])


def _moe_call(hb, h32, comb, wg, wu, wd, l2g, l2b, *, tm):
    rows, d = h32.shape
    ne, _, dff = wg.shape
    tspec = lambda width: pl.BlockSpec((tm, width), lambda i, e: (i, 0))
    full = lambda a: pl.BlockSpec(a.shape, lambda i, e: (0,) * a.ndim)
    return pl.pallas_call(
        _moe_kernel,
        grid=(rows // tm, ne),
        in_specs=[tspec(d), tspec(d), tspec(LANES),
                  pl.BlockSpec((1, d, dff), lambda i, e: (e, 0, 0)), pl.BlockSpec((1, d, dff), lambda i, e: (e, 0, 0)),
                  pl.BlockSpec((1, dff, d), lambda i, e: (e, 0, 0)), full(l2g), full(l2b)],
        out_specs=tspec(d),
        out_shape=jax.ShapeDtypeStruct((rows, d), F32),
        scratch_shapes=[pltpu.VMEM((tm, d), F32)],
        compiler_params=_cparams(("arbitrary", "arbitrary")),
        name="moe",
    )(hb, h32, comb, wg, wu, wd, l2g, l2b)


def _pick(n, prefs):
    for t in prefs:
        if n % t == 0:
            return t
    return n


def _rearranged_w_in(w_in):
    hk = H_A * DK_A
    widths = (hk, hk, W_A, W_A, H_A, H_A, W_A, W_A, W_A, H_B, w_in.shape[1] - (4 * hk + 2 * H_A + 3 * W_A + H_B))
    offs = [0]
    for wd in widths:
        offs.append(offs[-1] + wd)
    sl = lambda i: w_in[:, offs[i]:offs[i + 1]]
    gq, gk, gv, gz, ga, gb, fq, fk, fv, ff, gates = (sl(i) for i in range(11))
    pad = jnp.zeros((w_in.shape[0], LANES - 4 * H_A), w_in.dtype)
    return jnp.concatenate([gq, gk, gv, gz, fq, fk, fv, gates, ga, gb, ff, ff, pad], 1).astype(BF16)


def kernel(x_prompt, x_sample, cache_k, cache_v, cache_logf, state_gdn, state_conv, page_table, meta_tokens, ln_in_g, ln_in_b, w_in, conv_w, a_log, dt_bias, f_bias, gdn_norm_w, w_branch, w_out, ln1_g, ln1_b, w_router_group, b_router_group, w_router_expert, b_router_expert, w_gate, w_up, w_down, ln2_g, ln2_b):
    bsz, seq, d = x_prompt.shape
    n_dec = x_sample.shape[0]
    row2 = lambda a: a.reshape(1, -1).astype(F32)
    lng, lnb = row2(ln_in_g), row2(ln_in_b)
    w_big = _rearranged_w_in(w_in[0])
    convw = conv_w[0].astype(F32)
    zpad = lambda a, n: jnp.pad(a.astype(F32), (0, n - a.shape[0]))
    sp = jnp.zeros((8, LANES), F32)
    sp = sp.at[0].set(zpad(a_log[0], LANES)).at[1].set(zpad(dt_bias[0], LANES))
    sp = sp.at[2].set(zpad(jnp.concatenate([jnp.zeros((L_LOGF,), F32), f_bias[0], f_bias[0]]), LANES))

    meta = _front_call(meta_tokens.astype(F32)[None], lng, lnb, w_big, convw, jnp.zeros((1, 8, QKV), F32), sp,
                       jnp.zeros((1, LANES), F32), tm=N_META, emit_t=False)
    pad_rows = CHUNK_A - N_META
    front_pad = lambda a: jnp.pad(a, ((0, 0), (pad_rows, 0), (0, 0)))
    sm_meta = front_pad(meta["small"])
    _, s_meta = _gdn_call(front_pad(meta["qa"]), front_pad(meta["ka"]), front_pad(meta["va"]), sm_meta,
                          jnp.swapaxes(sm_meta, 1, 2)[:, :16], jnp.zeros((1, H_A, DK_A, DV_A), F32), lt=CHUNK_A)
    c_meta = meta["small"][0, :, L_C:L_C + H_B]
    c0 = jnp.zeros((1, LANES), F32).at[0, L_C:L_C + H_B].set(c_meta[-1])
    kmeta = jnp.pad(meta["fkb"][0], ((0, LANES - N_META), (0, 0)))
    vmeta = jnp.pad(meta["fvb"][0], ((0, LANES - N_META), (0, 0)))
    cmeta = jnp.full((H_B, 1, LANES), -NEG_INF, F32).at[:, 0, :N_META].set(c_meta.T)

    tm = _pick(seq, (512, 256, 128, 64))
    pf = _front_call(x_prompt, lng, lnb, w_big, convw, meta["tail"], sp, c0, tm=tm, emit_t=True)
    o_a, s_prompt = _gdn_call(pf["qa"], pf["ka"], pf["va"], pf["small"], pf["small_t"], s_meta,
                              lt=_pick(seq, (256, 128, 64)))
    c_t = pf["small_t"][:, L_C:L_C + H_B].reshape(bsz, H_B, 1, seq)
    o_b = _fox_call(pf["fq"], pf["fkb"], pf["fvb"], c_t, kmeta, vmeta, cmeta, tq=_pick(seq, (512, 256, 128)))

    nw = row2(gdn_norm_w[0])
    wbr = w_branch[0].astype(BF16)
    wout = w_out[0].astype(BF16)
    l1g, l1b, l2g, l2b = row2(ln1_g[0]), row2(ln1_b[0]), row2(ln2_g[0]), row2(ln2_b[0])
    wr = jnp.pad(jnp.concatenate([w_router_group[0], w_router_expert[0]], 1).astype(F32),
                 ((0, 0), (0, LANES - N_GROUPS - N_EXPERTS)))
    br = jnp.pad(jnp.concatenate([b_router_group[0], b_router_expert[0]]).astype(F32),
                 (0, LANES - N_GROUPS - N_EXPERTS)).reshape(1, LANES)
    wg, wu, wd = w_gate[0].astype(BF16), w_up[0].astype(BF16), w_down[0].astype(BF16)

    def layer_back(x2, oa2, z2, ob2, gates2, tile, moe_tile):
        h32, hb, comb = _back_call(x2, oa2, z2, ob2, gates2, lng, lnb, nw, wbr, wout, l1g, l1b, wr, br, tm=tile)
        return _moe_call(hb, h32, comb, wg, wu, wd, l2g, l2b, tm=moe_tile)

    flat = lambda a: a.reshape(bsz * seq, a.shape[-1])
    n_tok = bsz * seq
    y_prompt = layer_back(flat(x_prompt), flat(o_a), flat(pf["z"]), flat(o_b), flat(pf["gates"]),
                          _pick(n_tok, (512, 256, 128, 64)), _pick(n_tok, (1024, 512, 256, 128, 64))).reshape(bsz, seq, d)

    st = jnp.swapaxes(state_conv[0].astype(F32), 0, 1)
    sf = _sample_front_call(x_sample.reshape(n_dec, d), lng, lnb, w_big, convw, st, sp)
    o_a_s, s_sample = _gdn_step_call(sf["q"], sf["k"], sf["v"], sf["small"], state_gdn[0].astype(F32))
    n_pages = page_table.shape[1]
    o_b_s = _fox_sample_call(page_table, cache_k[0], cache_v[0], cache_logf[0], sf["fq"], sf["fk"], sf["fv"],
                             sf["small"][:, L_LOGF:L_LOGF + H_B], pps=_pick(n_pages, (8, 4, 2, 1)))
    y_sample = layer_back(x_sample.reshape(n_dec, d), o_a_s, sf["z"], o_b_s, sf["gates"], n_dec, n_dec)

    heads = lambda a: a.reshape(a.shape[:-1] + (H_B, DH_B))
    with_meta = lambda m, p: jnp.concatenate([jnp.broadcast_to(m, (bsz,) + m.shape[1:]), p], 1)
    k_prompt = heads(with_meta(meta["fk32"], pf["fk32"]))[None]
    v_prompt = heads(with_meta(meta["fv32"], pf["fv32"]))[None]
    logf_prompt = with_meta(meta["small"][:, :, L_LOGF:L_LOGF + H_B], pf["small"][:, :, L_LOGF:L_LOGF + H_B])[None]
    conv_prompt = pf["tail"][:, 8 - (CONV_W - 1):][None]
    conv_sample = jnp.concatenate([state_conv[0][:, 1:].astype(F32), sf["new"][:, None]], 1)[None]
    return (y_prompt, y_sample.reshape(n_dec, 1, d), k_prompt, v_prompt, logf_prompt, s_prompt[None], conv_prompt,
            heads(sf["fk"])[None, :, None], heads(sf["fv"])[None, :, None],
            sf["small"][:, L_LOGF:L_LOGF + H_B][None, :, None], s_sample[None], conv_sample)
```

```python
import functools

import jax
import jax.numpy as jnp
from jax import lax
from jax.experimental import pallas as pl
from jax.experimental.pallas import tpu as pltpu

F32 = jnp.float32
BF16 = jnp.bfloat16
HIGHEST = lax.Precision.HIGHEST

N_META = 16
H_A = 4
DK_A = 128
DV_A = 128
CONV_W = 4
CHUNK_A = 64
H_B = 4
DH_B = 128
N_GROUPS = 4
EXPERTS_PER_GROUP = 4
N_EXPERTS = N_GROUPS * EXPERTS_PER_GROUP
D_FF_E = 512
DEPTH = 1
DEEP_ALPHA = (2 * DEPTH) ** 0.25
LN_EPS = 1e-5
RMS_EPS = 1e-6
NEG_INF = -1e30
LOG2E = 1.4426950408889634

LANES = 128
W_A = H_A * DV_A
QKV = 3 * W_A
C_QKV = 0
C_Z = C_QKV + QKV
C_FQ = C_Z + W_A
C_FK = C_FQ + W_A
C_FV = C_FK + W_A
C_GATES = C_FV + W_A
C_SMALL = C_GATES + 2 * 1024
P_COLS = C_SMALL + LANES
L_G, L_BETA, L_LOGF, L_C = 0, 4, 8, 12
L_GRP, L_EXP = 0, 4

VMEM_LIMIT = 56 * 1024 * 1024


def _cparams(sem, vmem=VMEM_LIMIT):
    return pltpu.CompilerParams(dimension_semantics=sem, vmem_limit_bytes=vmem)


def _layer_norm(x, g, b):
    mu = jnp.mean(x, -1, keepdims=True)
    xc = x - mu
    var = jnp.mean(xc * xc, -1, keepdims=True)
    return xc * lax.rsqrt(var + LN_EPS) * g + b


def _softplus(x):
    return jnp.maximum(x, 0.0) + jnp.log1p(jnp.exp(-jnp.abs(x)))


def _sigmoid(x):
    return 1.0 / (1.0 + jnp.exp(-x))


def _dot(a, b):
    return jnp.dot(a, b, preferred_element_type=F32)


def _dot_nt(a, b):
    return lax.dot_general(a, b, (((1,), (1,)), ((), ())), preferred_element_type=F32)


def _dot_tn(a, b):
    return lax.dot_general(a, b, (((0,), (0,)), ((), ())), preferred_element_type=F32)


def _gate_values(sm, sp, rows):
    lane = lax.broadcasted_iota(jnp.int32, (rows, LANES), 1)
    g = -jnp.exp(sp[0:1]) * _softplus(sm + sp[1:2])
    beta = _sigmoid(sm)
    logf = -_softplus(-(sm + sp[2:3]))
    return jnp.where(lane < L_BETA, g, jnp.where(lane < L_LOGF, beta, jnp.where(lane < L_C + 4, logf, 0.0)))


def _front_kernel(x_ref, lng_ref, lnb_ref, w_ref, convw_ref, halo_ref, sp_ref, c0_ref, *rest, names, tm, blk):
    o = dict(zip(names, rest))
    buf_ref, carry_ref = rest[len(names):]
    qa_ref, ka_ref, va_ref, tail_ref, small_ref = o["qa"], o["ka"], o["va"], o["tail"], o["small"]
    j = pl.program_id(1)
    nj = pl.num_programs(1)

    @pl.when(j == 0)
    def _():
        buf_ref[0:8, :] = halo_ref[0]
        carry_ref[...] = c0_ref[...]

    xn = _layer_norm(x_ref[0], lng_ref[...], lnb_ref[...]).astype(BF16)

    def proj(c0, width):
        return _dot(xn, w_ref[:, c0:c0 + width])

    buf_ref[8:8 + tm, :] = proj(C_QKV, QKV)
    outs = (qa_ref, ka_ref, va_ref)
    for grp in range(QKV // LANES):
        c = grp * LANES
        w = convw_ref[:, c:c + LANES]
        y = (buf_ref[5:5 + tm, c:c + LANES] * w[0:1] + buf_ref[6:6 + tm, c:c + LANES] * w[1:2]
             + buf_ref[7:7 + tm, c:c + LANES] * w[2:3] + buf_ref[8:8 + tm, c:c + LANES] * w[3:4])
        y = y * _sigmoid(y)
        which, head = divmod(grp, H_A)
        if which < 2:
            y = y * lax.rsqrt(jnp.sum(y * y, -1, keepdims=True) + RMS_EPS)
            if which == 0:
                y = y * DK_A ** -0.5
        outs[which][0, :, head * LANES:(head + 1) * LANES] = y.astype(BF16)

    @pl.when(j == nj - 1)
    def _():
        tail_ref[0] = buf_ref[tm:tm + 8, :]

    buf_ref[0:8, :] = buf_ref[tm:tm + 8, :]

    fk = proj(C_FK, W_A)
    o["fk32"][0] = fk
    fv = proj(C_FV, W_A)
    o["fv32"][0] = fv
    if "z" in o:
        o["z"][0] = proj(C_Z, W_A).astype(BF16)
        o["gates"][0] = proj(C_GATES, 2 * 1024).astype(BF16)
        o["fqt"][0] = (proj(C_FQ, W_A) * (DH_B ** -0.5 * LOG2E)).T.astype(BF16)
        o["fvt"][0] = fv.T.astype(BF16)

    vals = _gate_values(proj(C_SMALL, LANES), sp_ref[...], tm)
    span = min(LANES, tm)
    r = lax.broadcasted_iota(jnp.int32, (span, span), 0)
    c = lax.broadcasted_iota(jnp.int32, (span, span), 1)
    same_blk = c >= (r // blk) * blk
    tri = jnp.where(c <= r, jnp.where(same_blk, 1.0, 0.0), 0.0)
    parts = [jnp.dot(tri, vals[i * span:(i + 1) * span], precision=HIGHEST, preferred_element_type=F32)
             for i in range(tm // span)]
    cs = parts[0] if len(parts) == 1 else jnp.concatenate(parts, 0)
    run = carry_ref[...]
    pieces = []
    for b in range(tm // blk):
        piece = cs[b * blk:(b + 1) * blk, :] + run
        pieces.append(piece)
        run = piece[blk - 1:blk, :]
    carry_ref[...] = run
    cfull = pieces[0] if len(pieces) == 1 else jnp.concatenate(pieces, 0)
    lane = lax.broadcasted_iota(jnp.int32, (tm, LANES), 1)
    small = jnp.where(lane < L_BETA, cs, jnp.where(lane < L_C, vals, cfull))
    small_ref[0] = small
    if "small_t" in o:
        o["small_t"][0] = small.T[0:16, :]
        rnd = lambda a: a.astype(BF16).astype(F32)
        bias = (cfull[0:1, :] - cfull) * LOG2E
        hi = rnd(bias)
        mid = rnd(bias - hi)
        lo = bias - hi - mid
        for h in range(H_B):
            col = slice(L_C + h, L_C + h + 1)
            aug = jnp.where(lane == 0, hi[:, col], jnp.where(lane == 1, mid[:, col], jnp.where(lane == 2, lo[:, col], 0.0)))
            o["fkaug"][0, :, 2 * h * DH_B:(2 * h + 1) * DH_B] = fk[:, h * DH_B:(h + 1) * DH_B].astype(BF16)
            o["fkaug"][0, :, (2 * h + 1) * DH_B:(2 * h + 2) * DH_B] = aug.astype(BF16)


def _front_call(x, lng, lnb, w_big, conv_w, halo, sp, c0, *, tm, emit_t):
    bsz, seq, d = x.shape
    blk = min(CHUNK_A, tm)
    nj = seq // tm
    full2 = lambda a: pl.BlockSpec(a.shape, lambda b, j: (0, 0))
    tok = lambda width, dt: (jax.ShapeDtypeStruct((bsz, seq, width), dt),
                             pl.BlockSpec((1, tm, width), lambda b, j: (b, j, 0)))
    tok_t = lambda rows, dt: (jax.ShapeDtypeStruct((bsz, rows, seq), dt),
                              pl.BlockSpec((1, rows, tm), lambda b, j: (b, 0, j)))
    outs = dict(qa=tok(W_A, BF16), ka=tok(W_A, BF16), va=tok(W_A, BF16), fk32=tok(W_A, F32), fv32=tok(W_A, F32),
                small=tok(LANES, F32),
                tail=(jax.ShapeDtypeStruct((bsz, 8, QKV), F32), pl.BlockSpec((1, 8, QKV), lambda b, j: (b, 0, 0))))
    if emit_t:
        outs.update(z=tok(W_A, BF16), gates=tok(2 * 1024, BF16), small_t=tok_t(16, F32), fqt=tok_t(W_A, BF16),
                    fvt=tok_t(W_A, BF16), fkaug=tok(2 * W_A, BF16))
    names = tuple(outs)

    res = pl.pallas_call(
        functools.partial(_front_kernel, names=names, tm=tm, blk=blk),
        grid=(bsz, nj),
        in_specs=[pl.BlockSpec((1, tm, d), lambda b, j: (b, j, 0)), full2(lng), full2(lnb),
                  pl.BlockSpec(w_big.shape, lambda b, j: (0, 0), pipeline_mode=pl.Buffered(1)),
                  full2(conv_w), pl.BlockSpec((1, 8, QKV), lambda b, j: (0, 0, 0)), full2(sp), full2(c0)],
        out_specs=[outs[n][1] for n in names],
        out_shape=[outs[n][0] for n in names],
        scratch_shapes=[pltpu.VMEM((tm + 8, QKV), F32), pltpu.VMEM((1, LANES), F32)],
        compiler_params=_cparams(("arbitrary", "arbitrary")),
        name="front",
    )(x, lng, lnb, w_big, conv_w, halo, sp, c0)
    return dict(zip(names, res))


def _sample_front_kernel(x_ref, lng_ref, lnb_ref, w_ref, convw_ref, st_ref, sp_ref,
                         q_ref, k_ref, v_ref, z_ref, fq_ref, fk_ref, fv_ref, gates_ref, small_ref, new_ref):
    rows = x_ref.shape[0]
    xn = _layer_norm(x_ref[...], lng_ref[...], lnb_ref[...]).astype(BF16)

    def proj(c0, width):
        return _dot(xn, w_ref[:, c0:c0 + width])

    new = proj(C_QKV, QKV)
    new_ref[...] = new
    w = convw_ref[...]
    y = st_ref[0] * w[0:1] + st_ref[1] * w[1:2] + st_ref[2] * w[2:3] + new * w[3:4]
    y = y * _sigmoid(y)
    outs = (q_ref, k_ref, v_ref)
    for grp in range(QKV // LANES):
        which, head = divmod(grp, H_A)
        yy = y[:, grp * LANES:(grp + 1) * LANES]
        if which < 2:
            yy = yy * lax.rsqrt(jnp.sum(yy * yy, -1, keepdims=True) + RMS_EPS)
            if which == 0:
                yy = yy * DK_A ** -0.5
        outs[which][:, head * LANES:(head + 1) * LANES] = yy
    z_ref[...] = proj(C_Z, W_A)
    fq_ref[...] = proj(C_FQ, W_A) * DH_B ** -0.5
    fk_ref[...] = proj(C_FK, W_A)
    fv_ref[...] = proj(C_FV, W_A)
    gates_ref[...] = proj(C_GATES, 2 * 1024)
    small_ref[...] = _gate_values(proj(C_SMALL, LANES), sp_ref[...], rows)


def _sample_front_call(x, lng, lnb, w_big, conv_w, st, sp):
    rows = x.shape[0]
    o = lambda width: jax.ShapeDtypeStruct((rows, width), F32)
    res = pl.pallas_call(
        _sample_front_kernel,
        out_shape=[o(W_A), o(W_A), o(W_A), o(W_A), o(W_A), o(W_A), o(W_A), o(2 * 1024), o(LANES), o(QKV)],
        compiler_params=pltpu.CompilerParams(vmem_limit_bytes=VMEM_LIMIT),
        name="sample_front",
    )(x, lng, lnb, w_big, conv_w, st, sp)
    return dict(zip(["q", "k", "v", "z", "fq", "fk", "fv", "gates", "small", "new"], res))


def _gdn_kernel(q_ref, k_ref, v_ref, sm_ref, smt_ref, s0_ref, o_ref, sout_ref, state_ref, *, lt):
    j = pl.program_id(0)
    nj = pl.num_programs(0)
    ch = CHUNK_A
    bsz = q_ref.shape[0]

    @pl.when(j == 0)
    def _():
        for b in range(bsz):
            state_ref[b] = s0_ref[0]

    ri = lax.broadcasted_iota(jnp.int32, (ch, ch), 0)
    ci = lax.broadcasted_iota(jnp.int32, (ch, ch), 1)
    incl = ri >= ci
    strict = ri > ci
    units = [(b, c, h) for b in range(bsz) for c in range(lt // ch) for h in range(H_A)]
    t = {}
    for u in units:
        b, c, h = u
        rows = slice(c * ch, (c + 1) * ch)
        cols = slice(h * DK_A, (h + 1) * DK_A)
        kc = k_ref[b, rows, cols]
        qc = q_ref[b, rows, cols]
        gc_col = sm_ref[b, rows, L_G + h:L_G + h + 1]
        beta = sm_ref[b, rows, L_BETA + h:L_BETA + h + 1]
        gc_row = smt_ref[b, L_G + h:L_G + h + 1, rows]
        gc_last = sm_ref[b, c * ch + ch - 1:c * ch + ch, L_G + h:L_G + h + 1]
        eg = jnp.exp(gc_col)
        kf = kc.astype(F32)
        kb = kf * beta
        t[u] = dict(kc=kc, decay=jnp.exp(jnp.where(incl, gc_col - gc_row, NEG_INF)),
                    lhs=jnp.concatenate([kb.astype(BF16), qc], 0),
                    rhs=jnp.concatenate([v_ref[b, rows, cols].astype(F32) * beta, kb * eg], 1),
                    q_dec=(qc.astype(F32) * eg).astype(BF16),
                    k_dec=(kf * jnp.exp(gc_last - gc_col)).astype(BF16), g_tot=jnp.exp(gc_last))
    for u in units:
        d = t[u]
        a = _dot_nt(d["lhs"], d["kc"])
        d["m"] = jnp.where(strict, a[:ch] * d["decay"], 0.0)
        d["attn"] = (a[ch:] * d["decay"]).astype(BF16)
    for u in units:
        d = t[u]
        mb = d["m"].astype(BF16)
        d["x"] = -d["m"]
        d["p"] = _dot(mb, mb)
    for _ in range(4):
        for u in units:
            d = t[u]
            pb = d["p"].astype(BF16)
            r = _dot(jnp.concatenate([pb, d["x"].astype(BF16)], 0), pb)
            d["x"] = d["x"] + d["p"] + r[ch:]
            d["p"] = r[:ch]
    for u in units:
        d = t[u]
        d["x"] = d["x"] + d["p"] + _dot(d["x"].astype(BF16), d["p"].astype(BF16))
    for u in units:
        d = t[u]
        sol = d["rhs"] + _dot(d["x"].astype(BF16), d["rhs"].astype(BF16))
        d["u"] = sol[:, :DV_A]
        d["wq"] = jnp.concatenate([sol[:, DV_A:].astype(BF16), d["q_dec"]], 0)
    heads = [(b, h) for b in range(bsz) for h in range(H_A)]
    s = {bh: state_ref[bh[0], bh[1]] for bh in heads}
    for c in range(lt // ch):
        ws = {bh: _dot(t[bh[0], c, bh[1]]["wq"], s[bh].astype(BF16)) for bh in heads}
        v_new = {bh: (t[bh[0], c, bh[1]]["u"] - ws[bh][:ch]).astype(BF16) for bh in heads}
        for bh in heads:
            b, h = bh
            d = t[b, c, h]
            o = ws[bh][ch:] + _dot(d["attn"], v_new[bh])
            o_ref[b, c * ch:(c + 1) * ch, h * DV_A:(h + 1) * DV_A] = o.astype(o_ref.dtype)
        for bh in heads:
            d = t[bh[0], c, bh[1]]
            s[bh] = s[bh] * d["g_tot"] + _dot_tn(d["k_dec"], v_new[bh])
    for bh in heads:
        state_ref[bh[0], bh[1]] = s[bh]

    @pl.when(j == nj - 1)
    def _():
        sout_ref[...] = state_ref[...]


def _gdn_call(q, k, v, small, small_t, s0, *, lt):
    bsz, seq, _ = q.shape
    nj = seq // lt
    tspec = pl.BlockSpec((bsz, lt, W_A), lambda j: (0, j, 0))
    sshape = (bsz, H_A, DK_A, DV_A)
    return pl.pallas_call(
        functools.partial(_gdn_kernel, lt=lt),
        grid=(nj,),
        in_specs=[tspec, tspec, tspec, pl.BlockSpec((bsz, lt, LANES), lambda j: (0, j, 0)),
                  pl.BlockSpec((bsz, 16, lt), lambda j: (0, 0, j)),
                  pl.BlockSpec((1, H_A, DK_A, DV_A), lambda j: (0, 0, 0, 0))],
        out_specs=[tspec, pl.BlockSpec(sshape, lambda j: (0, 0, 0, 0))],
        out_shape=[jax.ShapeDtypeStruct((bsz, seq, W_A), BF16), jax.ShapeDtypeStruct(sshape, F32)],
        scratch_shapes=[pltpu.VMEM(sshape, F32)],
        compiler_params=_cparams(("arbitrary",)),
        name="gdn_chunk",
    )(q, k, v, small, small_t, s0)


def _gdn_step_kernel(q_ref, k_ref, v_ref, sm_ref, s_ref, o_ref, sout_ref):
    ri = lax.broadcasted_iota(jnp.int32, (DK_A, DK_A), 0)
    ci = lax.broadcasted_iota(jnp.int32, (DK_A, DK_A), 1)
    eye = ri == ci

    def column(row):
        return jnp.sum(jnp.where(eye, jnp.broadcast_to(row, (DK_A, DK_A)), 0.0), -1, keepdims=True)

    sm = sm_ref[0]
    for h in range(H_A):
        cols = slice(h * DK_A, (h + 1) * DK_A)
        kcol = column(k_ref[0, :, cols])
        qcol = column(q_ref[0, :, cols])
        s = s_ref[0, h] * jnp.exp(sm[:, L_G + h:L_G + h + 1])
        v_t = (v_ref[0, :, cols] - jnp.sum(kcol * s, 0, keepdims=True)) * sm[:, L_BETA + h:L_BETA + h + 1]
        s = s + kcol * v_t
        sout_ref[0, h] = s
        o_ref[0, :, cols] = jnp.sum(qcol * s, 0, keepdims=True)


def _gdn_step_call(q, k, v, small, state):
    n = q.shape[0]
    row = lambda a: a.reshape(n, 1, a.shape[-1])
    rspec = lambda width: pl.BlockSpec((1, 1, width), lambda b: (b, 0, 0))
    sspec = pl.BlockSpec((1, H_A, DK_A, DV_A), lambda b: (b, 0, 0, 0))
    o, s = pl.pallas_call(
        _gdn_step_kernel,
        grid=(n,),
        in_specs=[rspec(W_A), rspec(W_A), rspec(W_A), rspec(LANES), sspec],
        out_specs=[rspec(W_A), sspec],
        out_shape=[jax.ShapeDtypeStruct((n, 1, W_A), F32), jax.ShapeDtypeStruct(state.shape, F32)],
        compiler_params=_cparams(("arbitrary",)),
        name="gdn_step",
    )(row(q), row(k), row(v), row(small), state)
    return o.reshape(n, W_A), s


def _foxt_kernel(qt_ref, ka_ref, vt_ref, ct_ref, kma_ref, vmt_ref, o_ref, m_sc, l_sc, acc_sc, *, tq, sub):
    qi = pl.program_id(1)
    qstart = pl.multiple_of(qi * tq, tq)
    heads = range(H_B)
    hs = lambda h: slice(h * DH_B, (h + 1) * DH_B)
    ha = lambda h: slice(2 * h * DH_B, (2 * h + 2) * DH_B)
    ones = jnp.where(lax.broadcasted_iota(jnp.int32, (DH_B, tq), 0) < 3, 1.0, 0.0).astype(BF16)
    qa = [jnp.concatenate([qt_ref[0, hs(h), :], ones], 0) for h in heads]
    c_first = [ct_ref[0, h, :, pl.ds(qstart, LANES)][:, 0:1] for h in heads]

    def step(keys, vt, offs, mask, first):
        s = [_dot(keys[h], qa[h]) for h in heads]
        p, alpha = [], []
        for h in heads:
            sh = s[h] if mask is None else jnp.where(mask, s[h], NEG_INF)
            blk_max = jnp.max(sh, 0, keepdims=True) + offs[h]
            m_new = blk_max if first else jnp.maximum(m_sc[h], blk_max)
            ph = jnp.exp2(sh - (m_new - offs[h]))
            row_sum = jnp.sum(ph, 0, keepdims=True)
            if first:
                l_sc[h] = row_sum
            else:
                a = jnp.exp2(m_sc[h] - m_new)
                l_sc[h] = a * l_sc[h] + row_sum
                alpha.append(a)
            m_sc[h] = m_new
            p.append(ph.astype(BF16))
        for h in heads:
            pv = _dot(vt[h], p[h])
            acc_sc[h] = pv if first else alpha[h] * acc_sc[h] + pv

    step([kma_ref[:, ha(h)] for h in heads], [vmt_ref[hs(h), :] for h in heads],
         [c_first[h] * LOG2E for h in heads], None, True)

    def block(start, masked):
        offs = [(c_first[h] - ct_ref[0, h, :, pl.ds(start, LANES)][:, 0:1]) * LOG2E for h in heads]
        for part in range(tq // sub):
            lo = start + part * sub
            mask = None
            if masked:
                key = lax.broadcasted_iota(jnp.int32, (sub, tq), 0) + part * sub
                mask = key <= lax.broadcasted_iota(jnp.int32, (sub, tq), 1)
            step([ka_ref[0, pl.ds(lo, sub), ha(h)] for h in heads],
                 [vt_ref[0, hs(h), pl.ds(lo, sub)] for h in heads], offs, mask, False)

    def body(jb, carry):
        block(pl.multiple_of(jb * tq, tq), False)
        return carry

    lax.fori_loop(0, qi, body, 0)
    block(qstart, True)
    for h in heads:
        o_ref[0, :, hs(h)] = (acc_sc[h] / l_sc[h]).T.astype(o_ref.dtype)


def _foxt_call(fqt, fkaug, fvt, c_t, kmaug, vmt, *, tq):
    bsz, hd, seq = fqt.shape
    once = dict(pipeline_mode=pl.Buffered(1))
    return pl.pallas_call(
        functools.partial(_foxt_kernel, tq=tq, sub=min(256, tq)),
        grid=(bsz, seq // tq),
        in_specs=[pl.BlockSpec((1, hd, tq), lambda b, i: (b, 0, i)),
                  pl.BlockSpec((1, seq, 2 * hd), lambda b, i: (b, 0, 0), **once),
                  pl.BlockSpec((1, hd, seq), lambda b, i: (b, 0, 0), **once),
                  pl.BlockSpec((1, H_B, 1, seq), lambda b, i: (b, 0, 0, 0)),
                  pl.BlockSpec(kmaug.shape, lambda b, i: (0, 0)), pl.BlockSpec(vmt.shape, lambda b, i: (0, 0))],
        out_specs=pl.BlockSpec((1, tq, hd), lambda b, i: (b, i, 0)),
        out_shape=jax.ShapeDtypeStruct((bsz, seq, hd), BF16),
        scratch_shapes=[pltpu.VMEM((H_B, 1, tq), F32), pltpu.VMEM((H_B, 1, tq), F32), pltpu.VMEM((H_B, DH_B, tq), F32)],
        compiler_params=_cparams(("arbitrary", "arbitrary")),
        name="fox_prompt",
    )(fqt, fkaug, fvt, c_t, kmaug, vmt)


def _fox_sample_kernel(pt_ref, *refs, pps):
    k_refs = refs[:pps]
    v_refs = refs[pps:2 * pps]
    lf_refs = refs[2 * pps:3 * pps]
    q_ref, kn_ref, vn_ref, lfn_ref, wsuf_ref, o_ref, m_sc, l_sc, run_sc, acc_sc = refs[3 * pps:]
    j = pl.program_id(1)
    nj = pl.num_programs(1)
    rows = k_refs[0].shape[1]
    q8 = q_ref[0]

    @pl.when(j == 0)
    def _():
        run_sc[...] = lfn_ref[0]
        m_sc[...] = jnp.broadcast_to(jnp.sum(q8 * kn_ref[0], -1, keepdims=True), (8, LANES))
        l_sc[...] = jnp.ones((8, LANES), F32)
        acc_sc[...] = vn_ref[0]

    qb = q8.astype(BF16)
    lf = jnp.concatenate([lf_refs[i][0] for i in range(pps)], 0)
    within = jnp.dot(lf, wsuf_ref[...], precision=HIGHEST, preferred_element_type=F32)
    tot = jnp.broadcast_to(jnp.sum(lf, -1, keepdims=True), (8 * pps, LANES))
    run = run_sc[...]
    valid = (lax.broadcasted_iota(jnp.int32, (8, rows), 1) % H_B) == lax.broadcasted_iota(jnp.int32, (8, rows), 0)
    scores = []
    for i in range(pps):
        sl = slice(8 * i, 8 * i + 8)
        bias = within[sl] + run[:, 0:1]
        scores.append(jnp.where(valid, _dot_nt(qb, k_refs[i][0].astype(BF16)) + bias, NEG_INF))
        run = run + tot[sl]
    run_sc[...] = run
    s = jnp.concatenate(scores, 1)
    m_prev = m_sc[...][:, 0:1]
    m_new = jnp.maximum(m_prev, jnp.max(s, -1, keepdims=True))
    alpha = jnp.exp(m_prev - m_new)
    p = jnp.exp(s - m_new)
    l_sc[...] = jnp.broadcast_to(alpha * l_sc[...][:, 0:1] + jnp.sum(p, -1, keepdims=True), (8, LANES))
    m_sc[...] = jnp.broadcast_to(m_new, (8, LANES))
    acc = alpha * acc_sc[...]
    for i in range(pps):
        acc = acc + _dot(p[:, i * rows:(i + 1) * rows].astype(BF16), v_refs[i][0].astype(BF16))
    acc_sc[...] = acc

    @pl.when(j == nj - 1)
    def _():
        o_ref[0] = acc / l_sc[...][:, 0:1]


def _fox_sample_call(page_table, cache_k, cache_v, cache_logf, fq, fk, fv, logf_new, *, pps):
    n, n_pages = page_table.shape
    n_pool, page = cache_k.shape[0], cache_k.shape[1]
    rows = page * H_B
    ck = cache_k.reshape(n_pool, rows, DH_B)
    cv = cache_v.reshape(n_pool, rows, DH_B)
    clf = jnp.pad(jnp.swapaxes(cache_logf, 1, 2), ((0, 0), (0, 8 - H_B), (0, 0)))
    tile8 = lambda a: jnp.pad(a.reshape(n, H_B, DH_B), ((0, 0), (0, 8 - H_B), (0, 0)))
    lfn = jnp.broadcast_to(jnp.pad(logf_new, ((0, 0), (0, 8 - H_B)))[:, :, None], (n, 8, LANES))
    wsuf = (lax.broadcasted_iota(jnp.int32, (page, rows), 0)
            > lax.broadcasted_iota(jnp.int32, (page, rows), 1) // H_B).astype(F32)

    def page_map(i):
        return lambda b, j, pt: (pt[b * n_pages + n_pages - 1 - (j * pps + i)], 0, 0)

    kv_specs = [pl.BlockSpec((1, rows, DH_B), page_map(i)) for i in range(pps)]
    lf_specs = [pl.BlockSpec((1, 8, page), page_map(i)) for i in range(pps)]
    tspec = pl.BlockSpec((1, 8, LANES), lambda b, j, pt: (b, 0, 0))
    full = lambda a: pl.BlockSpec(a.shape, lambda b, j, pt: (0, 0))
    grid_spec = pltpu.PrefetchScalarGridSpec(
        num_scalar_prefetch=1,
        grid=(n, n_pages // pps),
        in_specs=kv_specs + kv_specs + lf_specs + [tspec, tspec, tspec, tspec, full(wsuf)],
        out_specs=tspec,
        scratch_shapes=[pltpu.VMEM((8, LANES), F32)] * 4,
    )
    o = pl.pallas_call(
        functools.partial(_fox_sample_kernel, pps=pps),
        grid_spec=grid_spec,
        out_shape=jax.ShapeDtypeStruct((n, 8, DH_B), F32),
        compiler_params=_cparams(("arbitrary", "arbitrary")),
        name="fox_sample",
    )(page_table.reshape(-1), *([ck] * pps), *([cv] * pps), *([clf] * pps), tile8(fq), tile8(fk), tile8(fv), lfn,
      wsuf)
    return o[:, :H_B].reshape(n, H_B * DH_B)


def _back_kernel(x_ref, oa_ref, z_ref, ob_ref, gates_ref, lng_ref, lnb_ref, nw_ref, wbr_ref, wout_ref,
                 l1g_ref, l1b_ref, wr_ref, br_ref, h32_ref, hb_ref, comb_ref):
    tm = x_ref.shape[0]
    xn = _layer_norm(x_ref[...], lng_ref[...], lnb_ref[...])
    nw = nw_ref[...]
    parts = []
    for h in range(H_A):
        cols = slice(h * DV_A, (h + 1) * DV_A)
        o = oa_ref[:, cols].astype(F32)
        o = o * lax.rsqrt(jnp.mean(o * o, -1, keepdims=True) + RMS_EPS) * nw
        zz = z_ref[:, cols].astype(F32)
        parts.append((o * (zz * _sigmoid(zz))).astype(BF16))
    oa = jnp.concatenate(parts, 1)
    d = wout_ref.shape[0]
    merged = (_sigmoid(gates_ref[:, 0:d].astype(F32)) * _dot(oa, wbr_ref[0])
              + _sigmoid(gates_ref[:, d:2 * d].astype(F32)) * _dot(ob_ref[...].astype(BF16), wbr_ref[1]))
    mix = _dot(merged.astype(BF16), wout_ref[...])
    hh = _layer_norm(DEEP_ALPHA * xn + mix, l1g_ref[...], l1b_ref[...])
    h32_ref[...] = hh
    hb_ref[...] = hh.astype(BF16)

    logits = jnp.dot(hh, wr_ref[...], precision=HIGHEST, preferred_element_type=F32) + br_ref[...]
    lane = lax.broadcasted_iota(jnp.int32, (tm, LANES), 1)
    big = jnp.int32(LANES)
    is_grp = lane < L_EXP
    gl = jnp.where(is_grp, logits, NEG_INF)
    gmax = jnp.max(gl, -1, keepdims=True)
    gsel = jnp.min(jnp.where(jnp.logical_and(is_grp, gl == gmax), lane, big), -1, keepdims=True)
    gw = 1.0 / jnp.sum(jnp.exp(gl - gmax), -1, keepdims=True)
    lo = L_EXP + gsel * EXPERTS_PER_GROUP
    in_grp = jnp.logical_and(lane >= lo, lane < lo + EXPERTS_PER_GROUP)
    el = jnp.where(in_grp, logits, NEG_INF)
    m1 = jnp.max(el, -1, keepdims=True)
    i1 = jnp.min(jnp.where(el == m1, lane, big), -1, keepdims=True)
    el2 = jnp.where(lane == i1, NEG_INF, el)
    m2 = jnp.max(el2, -1, keepdims=True)
    i2 = jnp.min(jnp.where(el2 == m2, lane, big), -1, keepdims=True)
    e2 = jnp.exp(m2 - m1)
    w1 = gw / (1.0 + e2)
    w2 = gw * e2 / (1.0 + e2)
    comb_ref[...] = jnp.where(lane == i1, w1, jnp.where(lane == i2, w2, 0.0))


def _back_call(x, oa, z, ob, gates, lng, lnb, nw, wbr, wout, l1g, l1b, wr, br, *, tm):
    rows, d = x.shape
    tspec = lambda width: pl.BlockSpec((tm, width), lambda i: (i, 0))
    full = lambda a: pl.BlockSpec(a.shape, lambda i: (0,) * a.ndim)
    return pl.pallas_call(
        _back_kernel,
        grid=(rows // tm,),
        in_specs=[tspec(d), tspec(W_A), tspec(W_A), tspec(W_A), tspec(2 * d), full(lng), full(lnb), full(nw),
                  full(wbr), full(wout), full(l1g), full(l1b), full(wr), full(br)],
        out_specs=[tspec(d), tspec(d), tspec(LANES)],
        out_shape=[jax.ShapeDtypeStruct((rows, d), F32), jax.ShapeDtypeStruct((rows, d), BF16),
                   jax.ShapeDtypeStruct((rows, LANES), F32)],
        compiler_params=_cparams(("arbitrary",)),
        name="back",
    )(x, oa, z, ob, gates, lng, lnb, nw, wbr, wout, l1g, l1b, wr, br)


def _moe_kernel(hb_ref, h32_ref, comb_ref, wg_ref, wu_ref, wd_ref, l2g_ref, l2b_ref, y_ref, acc_ref):
    e = pl.program_id(1)
    ne = pl.num_programs(1)

    @pl.when(e == 0)
    def _():
        acc_ref[...] = jnp.zeros_like(acc_ref)

    x = hb_ref[...]
    g = _dot(x, wg_ref[0])
    hid = (g * _sigmoid(g)) * _dot(x, wu_ref[0])
    lane = lax.broadcasted_iota(jnp.int32, comb_ref.shape, 1)
    wgt = jnp.sum(jnp.where(lane == L_EXP + e, comb_ref[...], 0.0), -1, keepdims=True)
    acc_ref[...] += wgt * _dot(hid.astype(BF16), wd_ref[0])

    @pl.when(e == ne - 1)
    def _():
        y_ref[...] = _layer_norm(DEEP_ALPHA * h32_ref[...] + acc_ref[...], l2g_ref[...], l2b_ref[...])


def _moe_call(hb, h32, comb, wg, wu, wd, l2g, l2b, *, tm):
    rows, d = h32.shape
    ne, _, dff = wg.shape
    tspec = lambda width: pl.BlockSpec((tm, width), lambda i, e: (i, 0))
    full = lambda a: pl.BlockSpec(a.shape, lambda i, e: (0,) * a.ndim)
    return pl.pallas_call(
        _moe_kernel,
        grid=(rows // tm, ne),
        in_specs=[tspec(d), tspec(d), tspec(LANES),
                  pl.BlockSpec((1, d, dff), lambda i, e: (e, 0, 0)), pl.BlockSpec((1, d, dff), lambda i, e: (e, 0, 0)),
                  pl.BlockSpec((1, dff, d), lambda i, e: (e, 0, 0)), full(l2g), full(l2b)],
        out_specs=tspec(d),
        out_shape=jax.ShapeDtypeStruct((rows, d), F32),
        scratch_shapes=[pltpu.VMEM((tm, d), F32)],
        compiler_params=_cparams(("arbitrary", "arbitrary")),
        name="moe",
    )(hb, h32, comb, wg, wu, wd, l2g, l2b)


def _pick(n, prefs):
    for t in prefs:
        if n % t == 0:
            return t
    return n


def _rearranged_w_in(w_in):
    hk = H_A * DK_A
    widths = (hk, hk, W_A, W_A, H_A, H_A, W_A, W_A, W_A, H_B, w_in.shape[1] - (4 * hk + 2 * H_A + 3 * W_A + H_B))
    offs = [0]
    for wd in widths:
        offs.append(offs[-1] + wd)
    sl = lambda i: w_in[:, offs[i]:offs[i + 1]]
    gq, gk, gv, gz, ga, gb, fq, fk, fv, ff, gates = (sl(i) for i in range(11))
    pad = jnp.zeros((w_in.shape[0], LANES - 4 * H_A), w_in.dtype)
    return jnp.concatenate([gq, gk, gv, gz, fq, fk, fv, gates, ga, gb, ff, ff, pad], 1).astype(BF16)


def kernel(x_prompt, x_sample, cache_k, cache_v, cache_logf, state_gdn, state_conv, page_table, meta_tokens, ln_in_g, ln_in_b, w_in, conv_w, a_log, dt_bias, f_bias, gdn_norm_w, w_branch, w_out, ln1_g, ln1_b, w_router_group, b_router_group, w_router_expert, b_router_expert, w_gate, w_up, w_down, ln2_g, ln2_b):
    bsz, seq, d = x_prompt.shape
    n_dec = x_sample.shape[0]
    row2 = lambda a: a.reshape(1, -1).astype(F32)
    lng, lnb = row2(ln_in_g), row2(ln_in_b)
    w_big = _rearranged_w_in(w_in[0])
    convw = conv_w[0].astype(F32)
    zpad = lambda a, n: jnp.pad(a.astype(F32), (0, n - a.shape[0]))
    sp = jnp.zeros((8, LANES), F32)
    sp = sp.at[0].set(zpad(a_log[0], LANES)).at[1].set(zpad(dt_bias[0], LANES))
    sp = sp.at[2].set(zpad(jnp.concatenate([jnp.zeros((L_LOGF,), F32), f_bias[0], f_bias[0]]), LANES))

    meta = _front_call(meta_tokens.astype(F32)[None], lng, lnb, w_big, convw, jnp.zeros((1, 8, QKV), F32), sp,
                       jnp.zeros((1, LANES), F32), tm=N_META, emit_t=False)
    pad_rows = CHUNK_A - N_META
    front_pad = lambda a: jnp.pad(a, ((0, 0), (pad_rows, 0), (0, 0)))
    sm_meta = front_pad(meta["small"])
    _, s_meta = _gdn_call(front_pad(meta["qa"]), front_pad(meta["ka"]), front_pad(meta["va"]), sm_meta,
                          jnp.swapaxes(sm_meta, 1, 2)[:, :16], jnp.zeros((1, H_A, DK_A, DV_A), F32), lt=CHUNK_A)
    c_meta = meta["small"][0, :, L_C:L_C + H_B]
    c0 = jnp.zeros((1, LANES), F32).at[0, L_C:L_C + H_B].set(c_meta[-1])
    rnd = lambda a: a.astype(BF16).astype(F32)
    bias = jnp.pad(-c_meta * LOG2E, ((0, LANES - N_META), (0, 0)), constant_values=NEG_INF)
    hi = rnd(bias)
    mid = rnd(bias - hi)
    aug = jnp.zeros((LANES, H_B, DH_B), F32).at[:, :, 0].set(hi).at[:, :, 1].set(mid).at[:, :, 2].set(bias - hi - mid)
    kpad = jnp.pad(meta["fk32"][0], ((0, LANES - N_META), (0, 0))).reshape(LANES, H_B, DH_B)
    kmaug = jnp.concatenate([kpad, aug], -1).reshape(LANES, 2 * H_B * DH_B).astype(BF16)
    vmt = jnp.pad(meta["fv32"][0], ((0, LANES - N_META), (0, 0))).T.astype(BF16)

    tm = _pick(seq, (512, 256, 128, 64))
    pf = _front_call(x_prompt, lng, lnb, w_big, convw, meta["tail"], sp, c0, tm=tm, emit_t=True)
    o_a, s_prompt = _gdn_call(pf["qa"], pf["ka"], pf["va"], pf["small"], pf["small_t"], s_meta,
                              lt=_pick(seq, (256, 128, 64)))
    c_t = pf["small_t"][:, L_C:L_C + H_B].reshape(bsz, H_B, 1, seq)
    o_b = _foxt_call(pf["fqt"], pf["fkaug"], pf["fvt"], c_t, kmaug, vmt, tq=tm)

    nw = row2(gdn_norm_w[0])
    wbr = w_branch[0].astype(BF16)
    wout = w_out[0].astype(BF16)
    l1g, l1b, l2g, l2b = row2(ln1_g[0]), row2(ln1_b[0]), row2(ln2_g[0]), row2(ln2_b[0])
    wr = jnp.pad(jnp.concatenate([w_router_group[0], w_router_expert[0]], 1).astype(F32),
                 ((0, 0), (0, LANES - N_GROUPS - N_EXPERTS)))
    br = jnp.pad(jnp.concatenate([b_router_group[0], b_router_expert[0]]).astype(F32),
                 (0, LANES - N_GROUPS - N_EXPERTS)).reshape(1, LANES)
    wg, wu, wd = w_gate[0].astype(BF16), w_up[0].astype(BF16), w_down[0].astype(BF16)

    def layer_back(x2, oa2, z2, ob2, gates2, tile, moe_tile):
        h32, hb, comb = _back_call(x2, oa2, z2, ob2, gates2, lng, lnb, nw, wbr, wout, l1g, l1b, wr, br, tm=tile)
        return _moe_call(hb, h32, comb, wg, wu, wd, l2g, l2b, tm=moe_tile)

    flat = lambda a: a.reshape(bsz * seq, a.shape[-1])
    n_tok = bsz * seq
    y_prompt = layer_back(flat(x_prompt), flat(o_a), flat(pf["z"]), flat(o_b), flat(pf["gates"]),
                          _pick(n_tok, (512, 256, 128, 64)), _pick(n_tok, (1024, 512, 256, 128, 64))).reshape(bsz, seq, d)

    st = jnp.swapaxes(state_conv[0].astype(F32), 0, 1)
    sf = _sample_front_call(x_sample.reshape(n_dec, d), lng, lnb, w_big, convw, st, sp)
    o_a_s, s_sample = _gdn_step_call(sf["q"], sf["k"], sf["v"], sf["small"], state_gdn[0].astype(F32))
    n_pages = page_table.shape[1]
    o_b_s = _fox_sample_call(page_table, cache_k[0], cache_v[0], cache_logf[0], sf["fq"], sf["fk"], sf["fv"],
                             sf["small"][:, L_LOGF:L_LOGF + H_B], pps=_pick(n_pages, (8, 4, 2, 1)))
    y_sample = layer_back(x_sample.reshape(n_dec, d), o_a_s, sf["z"], o_b_s, sf["gates"], n_dec, n_dec)

    heads = lambda a: a.reshape(a.shape[:-1] + (H_B, DH_B))
    with_meta = lambda m, p: jnp.concatenate([jnp.broadcast_to(m, (bsz,) + m.shape[1:]), p], 1)
    k_prompt = heads(with_meta(meta["fk32"], pf["fk32"]))[None]
    v_prompt = heads(with_meta(meta["fv32"], pf["fv32"]))[None]
    logf_prompt = with_meta(meta["small"][:, :, L_LOGF:L_LOGF + H_B], pf["small"][:, :, L_LOGF:L_LOGF + H_B])[None]
    conv_prompt = pf["tail"][:, 8 - (CONV_W - 1):][None]
    conv_sample = jnp.concatenate([state_conv[0][:, 1:].astype(F32), sf["new"][:, None]], 1)[None]
    return (y_prompt, y_sample.reshape(n_dec, 1, d), k_prompt, v_prompt, logf_prompt, s_prompt[None], conv_prompt,
            heads(sf["fk"])[None, :, None], heads(sf["fv"])[None, :, None],
            sf["small"][:, L_LOGF:L_LOGF + H_B][None, :, None], s_sample[None], conv_sample)
```

```python
import functools

import jax
import jax.numpy as jnp
from jax import lax
from jax.experimental import pallas as pl
from jax.experimental.pallas import tpu as pltpu

F32 = jnp.float32
BF16 = jnp.bfloat16
HIGHEST = lax.Precision.HIGHEST

N_META = 16
H_A = 4
DK_A = 128
DV_A = 128
CONV_W = 4
CHUNK_A = 64
H_B = 4
DH_B = 128
N_GROUPS = 4
EXPERTS_PER_GROUP = 4
N_EXPERTS = N_GROUPS * EXPERTS_PER_GROUP
D_FF_E = 512
DEPTH = 1
DEEP_ALPHA = (2 * DEPTH) ** 0.25
LN_EPS = 1e-5
RMS_EPS = 1e-6
NEG_INF = -1e30
LOG2E = 1.4426950408889634

LANES = 128
W_A = H_A * DV_A
QKV = 3 * W_A
C_QKV = 0
C_Z = C_QKV + QKV
C_FQ = C_Z + W_A
C_FK = C_FQ + W_A
C_FV = C_FK + W_A
C_GATES = C_FV + W_A
C_SMALL = C_GATES + 2 * 1024
P_COLS = C_SMALL + LANES
L_G, L_BETA, L_LOGF, L_C = 0, 4, 8, 12
L_GRP, L_EXP = 0, 4

VMEM_LIMIT = 56 * 1024 * 1024


def _cparams(sem, vmem=VMEM_LIMIT):
    return pltpu.CompilerParams(dimension_semantics=sem, vmem_limit_bytes=vmem)


def _layer_norm(x, g, b):
    mu = jnp.mean(x, -1, keepdims=True)
    xc = x - mu
    var = jnp.mean(xc * xc, -1, keepdims=True)
    return xc * lax.rsqrt(var + LN_EPS) * g + b


def _softplus(x):
    return jnp.maximum(x, 0.0) + jnp.log1p(jnp.exp(-jnp.abs(x)))


def _sigmoid(x):
    return 1.0 / (1.0 + jnp.exp(-x))


def _dot(a, b):
    return jnp.dot(a, b, preferred_element_type=F32)


def _dot_nt(a, b):
    return lax.dot_general(a, b, (((1,), (1,)), ((), ())), preferred_element_type=F32)


def _split3(x):
    hi = x.astype(BF16)
    r1 = x - hi.astype(F32)
    mid = r1.astype(BF16)
    return hi, mid, (r1 - mid.astype(F32)).astype(BF16)


def _dot_split3(a, x):
    n = x.shape[1]
    y = _dot(a, jnp.concatenate(_split3(x), 1))
    return y[:, :n] + y[:, n:2 * n] + y[:, 2 * n:]


def _split3_dot(x, b):
    m = x.shape[0]
    y = _dot(jnp.concatenate(_split3(x), 0), b)
    return y[:m] + y[m:2 * m] + y[2 * m:]


def _dot_tn(a, b):
    return lax.dot_general(a, b, (((0,), (0,)), ((), ())), preferred_element_type=F32)


def _gate_values(sm, sp, rows):
    lane = lax.broadcasted_iota(jnp.int32, (rows, LANES), 1)
    g = -jnp.exp(sp[0:1]) * _softplus(sm + sp[1:2])
    beta = _sigmoid(sm)
    logf = -_softplus(-(sm + sp[2:3]))
    return jnp.where(lane < L_BETA, g, jnp.where(lane < L_LOGF, beta, jnp.where(lane < L_C + 4, logf, 0.0)))


def _front_kernel(x_ref, lng_ref, lnb_ref, w_ref, convw_ref, halo_ref, sp_ref, c0_ref, *rest, names, tm, blk):
    o = dict(zip(names, rest))
    buf_ref, carry_ref = rest[len(names):]
    qa_ref, ka_ref, va_ref, tail_ref, small_ref = o["qa"], o["ka"], o["va"], o["tail"], o["small"]
    j = pl.program_id(1)
    nj = pl.num_programs(1)

    @pl.when(j == 0)
    def _():
        buf_ref[0:8, :] = halo_ref[0]
        carry_ref[...] = c0_ref[...]

    xn = _layer_norm(x_ref[0], lng_ref[...], lnb_ref[...]).astype(BF16)

    def proj(c0, width):
        return _dot(xn, w_ref[:, c0:c0 + width])

    buf_ref[8:8 + tm, :] = proj(C_QKV, QKV)
    outs = (qa_ref, ka_ref, va_ref)
    for grp in range(QKV // LANES):
        c = grp * LANES
        w = convw_ref[:, c:c + LANES]
        y = (buf_ref[5:5 + tm, c:c + LANES] * w[0:1] + buf_ref[6:6 + tm, c:c + LANES] * w[1:2]
             + buf_ref[7:7 + tm, c:c + LANES] * w[2:3] + buf_ref[8:8 + tm, c:c + LANES] * w[3:4])
        y = y * _sigmoid(y)
        which, head = divmod(grp, H_A)
        if which < 2:
            y = y * lax.rsqrt(jnp.sum(y * y, -1, keepdims=True) + RMS_EPS)
            if which == 0:
                y = y * DK_A ** -0.5
        outs[which][0, :, head * LANES:(head + 1) * LANES] = y.astype(BF16)

    @pl.when(j == nj - 1)
    def _():
        tail_ref[0] = buf_ref[tm:tm + 8, :]

    buf_ref[0:8, :] = buf_ref[tm:tm + 8, :]

    fk = proj(C_FK, W_A)
    fv = proj(C_FV, W_A)
    if "fk32" in o:
        o["fk32"][0] = fk
        o["fv32"][0] = fv
    if "kout" in o:
        for h in range(H_B):
            o["kout"][0, pl.ds(h, tm, stride=H_B), :] = fk[:, h * DH_B:(h + 1) * DH_B]
            o["vout"][0, pl.ds(h, tm, stride=H_B), :] = fv[:, h * DH_B:(h + 1) * DH_B]
    if "z" in o:
        o["z"][0] = proj(C_Z, W_A).astype(BF16)
        o["gates"][0] = proj(C_GATES, 2 * 1024).astype(BF16)
        o["fqt"][0] = (proj(C_FQ, W_A) * (DH_B ** -0.5 * LOG2E)).T.astype(BF16)
        o["fvt"][0] = fv.T.astype(BF16)

    vals = _gate_values(proj(C_SMALL, LANES), sp_ref[...], tm)
    span = min(LANES, tm)
    r = lax.broadcasted_iota(jnp.int32, (span, span), 0)
    c = lax.broadcasted_iota(jnp.int32, (span, span), 1)
    same_blk = c >= (r // blk) * blk
    tri = jnp.where(c <= r, jnp.where(same_blk, 1.0, 0.0), 0.0).astype(BF16)
    parts = [_dot_split3(tri, vals[i * span:(i + 1) * span]) for i in range(tm // span)]
    cs = parts[0] if len(parts) == 1 else jnp.concatenate(parts, 0)
    run = carry_ref[...]
    pieces = []
    for b in range(tm // blk):
        piece = cs[b * blk:(b + 1) * blk, :] + run
        pieces.append(piece)
        run = piece[blk - 1:blk, :]
    carry_ref[...] = run
    cfull = pieces[0] if len(pieces) == 1 else jnp.concatenate(pieces, 0)
    lane = lax.broadcasted_iota(jnp.int32, (tm, LANES), 1)
    small = jnp.where(lane < L_BETA, cs, jnp.where(lane < L_C, vals, cfull))
    small_ref[0] = small
    if "small_t" in o:
        o["small_t"][0] = small.T[0:16, :]
        rnd = lambda a: a.astype(BF16).astype(F32)
        bias = (cfull[0:1, :] - cfull) * LOG2E
        hi = rnd(bias)
        mid = rnd(bias - hi)
        lo = bias - hi - mid
        for h in range(H_B):
            col = slice(L_C + h, L_C + h + 1)
            aug = jnp.where(lane == 0, hi[:, col], jnp.where(lane == 1, mid[:, col], jnp.where(lane == 2, lo[:, col], 0.0)))
            o["fkaug"][0, :, 2 * h * DH_B:(2 * h + 1) * DH_B] = fk[:, h * DH_B:(h + 1) * DH_B].astype(BF16)
            o["fkaug"][0, :, (2 * h + 1) * DH_B:(2 * h + 2) * DH_B] = aug.astype(BF16)


def _front_call(x, lng, lnb, w_big, conv_w, halo, sp, c0, *, tm, emit_t):
    bsz, seq, d = x.shape
    blk = min(CHUNK_A, tm)
    nj = seq // tm
    full2 = lambda a: pl.BlockSpec(a.shape, lambda b, j: (0, 0))
    tok = lambda width, dt: (jax.ShapeDtypeStruct((bsz, seq, width), dt),
                             pl.BlockSpec((1, tm, width), lambda b, j: (b, j, 0)))
    tok_t = lambda rows, dt: (jax.ShapeDtypeStruct((bsz, rows, seq), dt),
                              pl.BlockSpec((1, rows, tm), lambda b, j: (b, 0, j)))
    outs = dict(qa=tok(W_A, BF16), ka=tok(W_A, BF16), va=tok(W_A, BF16), small=tok(LANES, F32),
                tail=(jax.ShapeDtypeStruct((bsz, 8, QKV), F32), pl.BlockSpec((1, 8, QKV), lambda b, j: (b, 0, 0))))
    if emit_t:
        skip = N_META * H_B
        kv = (jax.ShapeDtypeStruct((bsz, skip + seq * H_B, DH_B), F32),
              pl.BlockSpec((pl.Element(1), pl.Element(tm * H_B), pl.Element(DH_B)),
                           lambda b, j: (b, pl.multiple_of(skip + j * (tm * H_B), skip), 0)))
        outs.update(kout=kv, vout=kv, z=tok(W_A, BF16), gates=tok(2 * 1024, BF16), small_t=tok_t(16, F32),
                    fqt=tok_t(W_A, BF16), fvt=tok_t(W_A, BF16), fkaug=tok(2 * W_A, BF16))
    else:
        outs.update(fk32=tok(W_A, F32), fv32=tok(W_A, F32))
    names = tuple(outs)

    res = pl.pallas_call(
        functools.partial(_front_kernel, names=names, tm=tm, blk=blk),
        grid=(bsz, nj),
        in_specs=[pl.BlockSpec((1, tm, d), lambda b, j: (b, j, 0)), full2(lng), full2(lnb),
                  pl.BlockSpec(w_big.shape, lambda b, j: (0, 0), pipeline_mode=pl.Buffered(1)),
                  full2(conv_w), pl.BlockSpec((1, 8, QKV), lambda b, j: (0, 0, 0)), full2(sp), full2(c0)],
        out_specs=[outs[n][1] for n in names],
        out_shape=[outs[n][0] for n in names],
        scratch_shapes=[pltpu.VMEM((tm + 8, QKV), F32), pltpu.VMEM((1, LANES), F32)],
        compiler_params=_cparams(("arbitrary", "arbitrary")),
        name="front",
    )(x, lng, lnb, w_big, conv_w, halo, sp, c0)
    return dict(zip(names, res))


def _put_meta_rows_kernel(k_any, v_any, km_ref, vm_ref, kout_ref, vout_ref):
    del k_any, v_any
    kout_ref[0] = km_ref[...]
    vout_ref[0] = vm_ref[...]


def _put_meta_rows(kout, vout, km, vm):
    bsz = kout.shape[0]
    head = pl.BlockSpec((1,) + km.shape, lambda b: (b, 0, 0))
    small = pl.BlockSpec(km.shape, lambda b: (0, 0))
    anyspec = pl.BlockSpec(memory_space=pl.ANY)
    return pl.pallas_call(
        _put_meta_rows_kernel,
        grid=(bsz,),
        in_specs=[anyspec, anyspec, small, small],
        out_specs=[head, head],
        out_shape=[jax.ShapeDtypeStruct(kout.shape, kout.dtype), jax.ShapeDtypeStruct(vout.shape, vout.dtype)],
        input_output_aliases={0: 0, 1: 1},
        compiler_params=_cparams(("arbitrary",)),
        name="put_meta_rows",
    )(kout, vout, km, vm)


def _sample_front_kernel(x_ref, lng_ref, lnb_ref, w_ref, convw_ref, st_ref, sp_ref,
                         q_ref, k_ref, v_ref, z_ref, fq_ref, fk_ref, fv_ref, gates_ref, small_ref, new_ref):
    rows = x_ref.shape[0]
    xn = _layer_norm(x_ref[...], lng_ref[...], lnb_ref[...]).astype(BF16)

    def proj(c0, width):
        return _dot(xn, w_ref[:, c0:c0 + width])

    new = proj(C_QKV, QKV)
    new_ref[...] = new
    w = convw_ref[...]
    y = st_ref[0] * w[0:1] + st_ref[1] * w[1:2] + st_ref[2] * w[2:3] + new * w[3:4]
    y = y * _sigmoid(y)
    outs = (q_ref, k_ref, v_ref)
    for grp in range(QKV // LANES):
        which, head = divmod(grp, H_A)
        yy = y[:, grp * LANES:(grp + 1) * LANES]
        if which < 2:
            yy = yy * lax.rsqrt(jnp.sum(yy * yy, -1, keepdims=True) + RMS_EPS)
            if which == 0:
                yy = yy * DK_A ** -0.5
        outs[which][:, head * LANES:(head + 1) * LANES] = yy
    z_ref[...] = proj(C_Z, W_A)
    fq_ref[...] = proj(C_FQ, W_A) * DH_B ** -0.5
    fk_ref[...] = proj(C_FK, W_A)
    fv_ref[...] = proj(C_FV, W_A)
    gates_ref[...] = proj(C_GATES, 2 * 1024)
    small_ref[...] = _gate_values(proj(C_SMALL, LANES), sp_ref[...], rows)


def _sample_front_call(x, lng, lnb, w_big, conv_w, st, sp):
    rows = x.shape[0]
    o = lambda width: jax.ShapeDtypeStruct((rows, width), F32)
    res = pl.pallas_call(
        _sample_front_kernel,
        out_shape=[o(W_A), o(W_A), o(W_A), o(W_A), o(W_A), o(W_A), o(W_A), o(2 * 1024), o(LANES), o(QKV)],
        compiler_params=pltpu.CompilerParams(vmem_limit_bytes=VMEM_LIMIT),
        name="sample_front",
    )(x, lng, lnb, w_big, conv_w, st, sp)
    return dict(zip(["q", "k", "v", "z", "fq", "fk", "fv", "gates", "small", "new"], res))


def _gdn_kernel(q_ref, k_ref, v_ref, sm_ref, smt_ref, s0_ref, o_ref, sout_ref, state_ref, *, lt):
    j = pl.program_id(0)
    nj = pl.num_programs(0)
    ch = CHUNK_A
    bsz = q_ref.shape[0]

    @pl.when(j == 0)
    def _():
        for b in range(bsz):
            state_ref[b] = s0_ref[0]

    ri = lax.broadcasted_iota(jnp.int32, (ch, ch), 0)
    ci = lax.broadcasted_iota(jnp.int32, (ch, ch), 1)
    incl = ri >= ci
    strict = ri > ci
    units = [(b, c, h) for b in range(bsz) for c in range(lt // ch) for h in range(H_A)]
    t = {}
    for u in units:
        b, c, h = u
        rows = slice(c * ch, (c + 1) * ch)
        cols = slice(h * DK_A, (h + 1) * DK_A)
        kc = k_ref[b, rows, cols]
        qc = q_ref[b, rows, cols]
        gc_col = sm_ref[b, rows, L_G + h:L_G + h + 1]
        beta = sm_ref[b, rows, L_BETA + h:L_BETA + h + 1]
        gc_row = smt_ref[b, L_G + h:L_G + h + 1, rows]
        gc_last = sm_ref[b, c * ch + ch - 1:c * ch + ch, L_G + h:L_G + h + 1]
        eg = jnp.exp(gc_col)
        kf = kc.astype(F32)
        kb = kf * beta
        t[u] = dict(kc=kc, decay=jnp.exp(jnp.where(incl, gc_col - gc_row, NEG_INF)),
                    lhs=jnp.concatenate([kb.astype(BF16), qc], 0),
                    rhs=jnp.concatenate([v_ref[b, rows, cols].astype(F32) * beta, kb * eg], 1),
                    q_dec=(qc.astype(F32) * eg).astype(BF16),
                    k_dec=(kf * jnp.exp(gc_last - gc_col)).astype(BF16), g_tot=jnp.exp(gc_last))
    for u in units:
        d = t[u]
        a = _dot_nt(d["lhs"], d["kc"])
        d["m"] = jnp.where(strict, a[:ch] * d["decay"], 0.0)
        d["attn"] = (a[ch:] * d["decay"]).astype(BF16)
    for u in units:
        d = t[u]
        mb = d["m"].astype(BF16)
        d["x"] = -d["m"]
        d["p"] = _dot(mb, mb)
    for _ in range(4):
        for u in units:
            d = t[u]
            pb = d["p"].astype(BF16)
            r = _dot(jnp.concatenate([pb, d["x"].astype(BF16)], 0), pb)
            d["x"] = d["x"] + d["p"] + r[ch:]
            d["p"] = r[:ch]
    for u in units:
        d = t[u]
        d["x"] = d["x"] + d["p"] + _dot(d["x"].astype(BF16), d["p"].astype(BF16))
    for u in units:
        d = t[u]
        sol = d["rhs"] + _dot(d["x"].astype(BF16), d["rhs"].astype(BF16))
        d["u"] = sol[:, :DV_A]
        d["wq"] = jnp.concatenate([sol[:, DV_A:].astype(BF16), d["q_dec"]], 0)
    heads = [(b, h) for b in range(bsz) for h in range(H_A)]
    s = {bh: state_ref[bh[0], bh[1]] for bh in heads}
    for c in range(lt // ch):
        ws = {bh: _dot(t[bh[0], c, bh[1]]["wq"], s[bh].astype(BF16)) for bh in heads}
        v_new = {bh: (t[bh[0], c, bh[1]]["u"] - ws[bh][:ch]).astype(BF16) for bh in heads}
        for bh in heads:
            b, h = bh
            d = t[b, c, h]
            o = ws[bh][ch:] + _dot(d["attn"], v_new[bh])
            o_ref[b, c * ch:(c + 1) * ch, h * DV_A:(h + 1) * DV_A] = o.astype(o_ref.dtype)
        for bh in heads:
            d = t[bh[0], c, bh[1]]
            s[bh] = s[bh] * d["g_tot"] + _dot_tn(d["k_dec"], v_new[bh])
    for bh in heads:
        state_ref[bh[0], bh[1]] = s[bh]

    @pl.when(j == nj - 1)
    def _():
        sout_ref[...] = state_ref[...]


def _gdn_call(q, k, v, small, small_t, s0, *, lt):
    bsz, seq, _ = q.shape
    nj = seq // lt
    tspec = pl.BlockSpec((bsz, lt, W_A), lambda j: (0, j, 0))
    sshape = (bsz, H_A, DK_A, DV_A)
    return pl.pallas_call(
        functools.partial(_gdn_kernel, lt=lt),
        grid=(nj,),
        in_specs=[tspec, tspec, tspec, pl.BlockSpec((bsz, lt, LANES), lambda j: (0, j, 0)),
                  pl.BlockSpec((bsz, 16, lt), lambda j: (0, 0, j)),
                  pl.BlockSpec((1, H_A, DK_A, DV_A), lambda j: (0, 0, 0, 0))],
        out_specs=[tspec, pl.BlockSpec(sshape, lambda j: (0, 0, 0, 0))],
        out_shape=[jax.ShapeDtypeStruct((bsz, seq, W_A), BF16), jax.ShapeDtypeStruct(sshape, F32)],
        scratch_shapes=[pltpu.VMEM(sshape, F32)],
        compiler_params=_cparams(("arbitrary",)),
        name="gdn_chunk",
    )(q, k, v, small, small_t, s0)


def _gdn_step_kernel(q_ref, k_ref, v_ref, sm_ref, s_ref, o_ref, sout_ref):
    ri = lax.broadcasted_iota(jnp.int32, (DK_A, DK_A), 0)
    ci = lax.broadcasted_iota(jnp.int32, (DK_A, DK_A), 1)
    eye = ri == ci

    def column(row):
        return jnp.sum(jnp.where(eye, jnp.broadcast_to(row, (DK_A, DK_A)), 0.0), -1, keepdims=True)

    sm = sm_ref[0]
    for h in range(H_A):
        cols = slice(h * DK_A, (h + 1) * DK_A)
        kcol = column(k_ref[0, :, cols])
        qcol = column(q_ref[0, :, cols])
        s = s_ref[0, h] * jnp.exp(sm[:, L_G + h:L_G + h + 1])
        v_t = (v_ref[0, :, cols] - jnp.sum(kcol * s, 0, keepdims=True)) * sm[:, L_BETA + h:L_BETA + h + 1]
        s = s + kcol * v_t
        sout_ref[0, h] = s
        o_ref[0, :, cols] = jnp.sum(qcol * s, 0, keepdims=True)


def _gdn_step_call(q, k, v, small, state):
    n = q.shape[0]
    row = lambda a: a.reshape(n, 1, a.shape[-1])
    rspec = lambda width: pl.BlockSpec((1, 1, width), lambda b: (b, 0, 0))
    sspec = pl.BlockSpec((1, H_A, DK_A, DV_A), lambda b: (b, 0, 0, 0))
    o, s = pl.pallas_call(
        _gdn_step_kernel,
        grid=(n,),
        in_specs=[rspec(W_A), rspec(W_A), rspec(W_A), rspec(LANES), sspec],
        out_specs=[rspec(W_A), sspec],
        out_shape=[jax.ShapeDtypeStruct((n, 1, W_A), F32), jax.ShapeDtypeStruct(state.shape, F32)],
        compiler_params=_cparams(("arbitrary",)),
        name="gdn_step",
    )(row(q), row(k), row(v), row(small), state)
    return o.reshape(n, W_A), s


def _foxt_kernel(qt_ref, ka_ref, vt_ref, ct_ref, kma_ref, vmt_ref, o_ref, m_sc, l_sc, acc_sc, *, tq, sub):
    qi = pl.program_id(1)
    qstart = pl.multiple_of(qi * tq, tq)
    heads = range(H_B)
    hs = lambda h: slice(h * DH_B, (h + 1) * DH_B)
    ha = lambda h: slice(2 * h * DH_B, (2 * h + 2) * DH_B)
    ones = jnp.where(lax.broadcasted_iota(jnp.int32, (DH_B, tq), 0) < 3, 1.0, 0.0).astype(BF16)
    qa = [jnp.concatenate([qt_ref[0, hs(h), :], ones], 0) for h in heads]
    c_first = [ct_ref[0, h, :, pl.ds(qstart, LANES)][:, 0:1] for h in heads]

    def step(keys, vt, offs, mask, first):
        s = [_dot(keys[h], qa[h]) for h in heads]
        p, alpha = [], []
        for h in heads:
            sh = s[h] if mask is None else jnp.where(mask, s[h], NEG_INF)
            blk_max = jnp.max(sh, 0, keepdims=True) + offs[h]
            m_new = blk_max if first else jnp.maximum(m_sc[h], blk_max)
            ph = jnp.exp2(sh - (m_new - offs[h]))
            row_sum = jnp.sum(ph, 0, keepdims=True)
            if first:
                l_sc[h] = row_sum
            else:
                a = jnp.exp2(m_sc[h] - m_new)
                l_sc[h] = a * l_sc[h] + row_sum
                alpha.append(a)
            m_sc[h] = m_new
            p.append(ph.astype(BF16))
        for h in heads:
            pv = _dot(vt[h], p[h])
            acc_sc[h] = pv if first else alpha[h] * acc_sc[h] + pv

    step([kma_ref[:, ha(h)] for h in heads], [vmt_ref[hs(h), :] for h in heads],
         [c_first[h] * LOG2E for h in heads], None, True)

    def block(start, masked):
        offs = [(c_first[h] - ct_ref[0, h, :, pl.ds(start, LANES)][:, 0:1]) * LOG2E for h in heads]
        for part in range(tq // sub):
            lo = start + part * sub
            mask = None
            if masked:
                key = lax.broadcasted_iota(jnp.int32, (sub, tq), 0) + part * sub
                mask = key <= lax.broadcasted_iota(jnp.int32, (sub, tq), 1)
            step([ka_ref[0, pl.ds(lo, sub), ha(h)] for h in heads],
                 [vt_ref[0, hs(h), pl.ds(lo, sub)] for h in heads], offs, mask, False)

    def body(jb, carry):
        block(pl.multiple_of(jb * tq, tq), False)
        return carry

    lax.fori_loop(0, qi, body, 0)
    block(qstart, True)
    for h in heads:
        o_ref[0, :, hs(h)] = (acc_sc[h] / l_sc[h]).T.astype(o_ref.dtype)


def _foxt_call(fqt, fkaug, fvt, c_t, kmaug, vmt, *, tq):
    bsz, hd, seq = fqt.shape
    once = dict(pipeline_mode=pl.Buffered(1))
    return pl.pallas_call(
        functools.partial(_foxt_kernel, tq=tq, sub=min(256, tq)),
        grid=(bsz, seq // tq),
        in_specs=[pl.BlockSpec((1, hd, tq), lambda b, i: (b, 0, i)),
                  pl.BlockSpec((1, seq, 2 * hd), lambda b, i: (b, 0, 0), **once),
                  pl.BlockSpec((1, hd, seq), lambda b, i: (b, 0, 0), **once),
                  pl.BlockSpec((1, H_B, 1, seq), lambda b, i: (b, 0, 0, 0)),
                  pl.BlockSpec(kmaug.shape, lambda b, i: (0, 0)), pl.BlockSpec(vmt.shape, lambda b, i: (0, 0))],
        out_specs=pl.BlockSpec((1, tq, hd), lambda b, i: (b, i, 0)),
        out_shape=jax.ShapeDtypeStruct((bsz, seq, hd), BF16),
        scratch_shapes=[pltpu.VMEM((H_B, 1, tq), F32), pltpu.VMEM((H_B, 1, tq), F32), pltpu.VMEM((H_B, DH_B, tq), F32)],
        compiler_params=_cparams(("arbitrary", "arbitrary")),
        name="fox_prompt",
    )(fqt, fkaug, fvt, c_t, kmaug, vmt)


def _fox_sample_kernel(pt_ref, *refs, pps):
    k_refs = refs[:pps]
    v_refs = refs[pps:2 * pps]
    lf_refs = refs[2 * pps:3 * pps]
    q_ref, kn_ref, vn_ref, lfn_ref, wsuf_ref, o_ref, m_sc, l_sc, run_sc, acc_sc = refs[3 * pps:]
    j = pl.program_id(1)
    nj = pl.num_programs(1)
    rows = k_refs[0].shape[1]
    q8 = q_ref[0]

    @pl.when(j == 0)
    def _():
        run_sc[...] = lfn_ref[0]
        m_sc[...] = jnp.broadcast_to(jnp.sum(q8 * kn_ref[0], -1, keepdims=True), (8, LANES))
        l_sc[...] = jnp.ones((8, LANES), F32)
        acc_sc[...] = vn_ref[0]

    qb = q8.astype(BF16)
    lf = jnp.concatenate([lf_refs[i][0] for i in range(pps)], 0)
    within = _split3_dot(lf, wsuf_ref[...])
    tot = jnp.broadcast_to(jnp.sum(lf, -1, keepdims=True), (8 * pps, LANES))
    run = run_sc[...]
    valid = (lax.broadcasted_iota(jnp.int32, (8, rows), 1) % H_B) == lax.broadcasted_iota(jnp.int32, (8, rows), 0)
    scores = []
    for i in range(pps):
        sl = slice(8 * i, 8 * i + 8)
        bias = within[sl] + run[:, 0:1]
        scores.append(jnp.where(valid, _dot_nt(qb, k_refs[i][0].astype(BF16)) + bias, NEG_INF))
        run = run + tot[sl]
    run_sc[...] = run
    s = jnp.concatenate(scores, 1)
    m_prev = m_sc[...][:, 0:1]
    m_new = jnp.maximum(m_prev, jnp.max(s, -1, keepdims=True))
    alpha = jnp.exp(m_prev - m_new)
    p = jnp.exp(s - m_new)
    l_sc[...] = jnp.broadcast_to(alpha * l_sc[...][:, 0:1] + jnp.sum(p, -1, keepdims=True), (8, LANES))
    m_sc[...] = jnp.broadcast_to(m_new, (8, LANES))
    acc = alpha * acc_sc[...]
    for i in range(pps):
        acc = acc + _dot(p[:, i * rows:(i + 1) * rows].astype(BF16), v_refs[i][0].astype(BF16))
    acc_sc[...] = acc

    @pl.when(j == nj - 1)
    def _():
        o_ref[0] = acc / l_sc[...][:, 0:1]


def _fox_sample_call(page_table, cache_k, cache_v, cache_logf, fq, fk, fv, logf_new, *, pps):
    n, n_pages = page_table.shape
    n_pool, page = cache_k.shape[0], cache_k.shape[1]
    rows = page * H_B
    ck = cache_k.reshape(n_pool, rows, DH_B)
    cv = cache_v.reshape(n_pool, rows, DH_B)
    clf = jnp.pad(jnp.swapaxes(cache_logf, 1, 2), ((0, 0), (0, 8 - H_B), (0, 0)))
    tile8 = lambda a: jnp.pad(a.reshape(n, H_B, DH_B), ((0, 0), (0, 8 - H_B), (0, 0)))
    lfn = jnp.broadcast_to(jnp.pad(logf_new, ((0, 0), (0, 8 - H_B)))[:, :, None], (n, 8, LANES))
    wsuf = (lax.broadcasted_iota(jnp.int32, (page, rows), 0)
            > lax.broadcasted_iota(jnp.int32, (page, rows), 1) // H_B).astype(BF16)

    def page_map(i):
        return lambda b, j, pt: (pt[b * n_pages + n_pages - 1 - (j * pps + i)], 0, 0)

    kv_specs = [pl.BlockSpec((1, rows, DH_B), page_map(i)) for i in range(pps)]
    lf_specs = [pl.BlockSpec((1, 8, page), page_map(i)) for i in range(pps)]
    tspec = pl.BlockSpec((1, 8, LANES), lambda b, j, pt: (b, 0, 0))
    full = lambda a: pl.BlockSpec(a.shape, lambda b, j, pt: (0, 0))
    grid_spec = pltpu.PrefetchScalarGridSpec(
        num_scalar_prefetch=1,
        grid=(n, n_pages // pps),
        in_specs=kv_specs + kv_specs + lf_specs + [tspec, tspec, tspec, tspec, full(wsuf)],
        out_specs=tspec,
        scratch_shapes=[pltpu.VMEM((8, LANES), F32)] * 4,
    )
    o = pl.pallas_call(
        functools.partial(_fox_sample_kernel, pps=pps),
        grid_spec=grid_spec,
        out_shape=jax.ShapeDtypeStruct((n, 8, DH_B), F32),
        compiler_params=_cparams(("arbitrary", "arbitrary")),
        name="fox_sample",
    )(page_table.reshape(-1), *([ck] * pps), *([cv] * pps), *([clf] * pps), tile8(fq), tile8(fk), tile8(fv), lfn,
      wsuf)
    return o[:, :H_B].reshape(n, H_B * DH_B)


def _back_kernel(x_ref, oa_ref, z_ref, ob_ref, gates_ref, lng_ref, lnb_ref, nw_ref, wbr_ref, wout_ref,
                 l1g_ref, l1b_ref, wr_ref, br_ref, h32_ref, hb_ref, comb_ref):
    tm = x_ref.shape[0]
    xn = _layer_norm(x_ref[...], lng_ref[...], lnb_ref[...])
    nw = nw_ref[...]
    parts = []
    for h in range(H_A):
        cols = slice(h * DV_A, (h + 1) * DV_A)
        o = oa_ref[:, cols].astype(F32)
        o = o * lax.rsqrt(jnp.mean(o * o, -1, keepdims=True) + RMS_EPS) * nw
        zz = z_ref[:, cols].astype(F32)
        parts.append((o * (zz * _sigmoid(zz))).astype(BF16))
    oa = jnp.concatenate(parts, 1)
    d = wout_ref.shape[0]
    merged = (_sigmoid(gates_ref[:, 0:d].astype(F32)) * _dot(oa, wbr_ref[0])
              + _sigmoid(gates_ref[:, d:2 * d].astype(F32)) * _dot(ob_ref[...].astype(BF16), wbr_ref[1]))
    mix = _dot(merged.astype(BF16), wout_ref[...])
    hh = _layer_norm(DEEP_ALPHA * xn + mix, l1g_ref[...], l1b_ref[...])
    h32_ref[...] = hh
    hb_ref[...] = hh.astype(BF16)

    h_hi = hh.astype(BF16)
    h_mid = (hh - h_hi.astype(F32)).astype(BF16)
    logits = _dot(jnp.concatenate([h_hi, h_mid, h_hi], 1), wr_ref[...]) + br_ref[...]
    lane = lax.broadcasted_iota(jnp.int32, (tm, LANES), 1)
    big = jnp.int32(LANES)
    is_grp = lane < L_EXP
    gl = jnp.where(is_grp, logits, NEG_INF)
    gmax = jnp.max(gl, -1, keepdims=True)
    gsel = jnp.min(jnp.where(jnp.logical_and(is_grp, gl == gmax), lane, big), -1, keepdims=True)
    gw = 1.0 / jnp.sum(jnp.exp(gl - gmax), -1, keepdims=True)
    lo = L_EXP + gsel * EXPERTS_PER_GROUP
    in_grp = jnp.logical_and(lane >= lo, lane < lo + EXPERTS_PER_GROUP)
    el = jnp.where(in_grp, logits, NEG_INF)
    m1 = jnp.max(el, -1, keepdims=True)
    i1 = jnp.min(jnp.where(el == m1, lane, big), -1, keepdims=True)
    el2 = jnp.where(lane == i1, NEG_INF, el)
    m2 = jnp.max(el2, -1, keepdims=True)
    i2 = jnp.min(jnp.where(el2 == m2, lane, big), -1, keepdims=True)
    e2 = jnp.exp(m2 - m1)
    w1 = gw / (1.0 + e2)
    w2 = gw * e2 / (1.0 + e2)
    comb_ref[...] = jnp.where(lane == i1, w1, jnp.where(lane == i2, w2, 0.0))


def _back_call(x, oa, z, ob, gates, lng, lnb, nw, wbr, wout, l1g, l1b, wr, br, *, tm):
    rows, d = x.shape
    tspec = lambda width: pl.BlockSpec((tm, width), lambda i: (i, 0))
    full = lambda a: pl.BlockSpec(a.shape, lambda i: (0,) * a.ndim)
    return pl.pallas_call(
        _back_kernel,
        grid=(rows // tm,),
        in_specs=[tspec(d), tspec(W_A), tspec(W_A), tspec(W_A), tspec(2 * d), full(lng), full(lnb), full(nw),
                  full(wbr), full(wout), full(l1g), full(l1b), full(wr), full(br)],
        out_specs=[tspec(d), tspec(d), tspec(LANES)],
        out_shape=[jax.ShapeDtypeStruct((rows, d), F32), jax.ShapeDtypeStruct((rows, d), BF16),
                   jax.ShapeDtypeStruct((rows, LANES), F32)],
        compiler_params=_cparams(("arbitrary",)),
        name="back",
    )(x, oa, z, ob, gates, lng, lnb, nw, wbr, wout, l1g, l1b, wr, br)


def _moe_kernel(hb_ref, h32_ref, comb_ref, wg_ref, wu_ref, wd_ref, l2g_ref, l2b_ref, y_ref, acc_ref):
    e = pl.program_id(1)
    ne = pl.num_programs(1)

    @pl.when(e == 0)
    def _():
        acc_ref[...] = jnp.zeros_like(acc_ref)

    x = hb_ref[...]
    g = _dot(x, wg_ref[0])
    hid = (g * _sigmoid(g)) * _dot(x, wu_ref[0])
    lane = lax.broadcasted_iota(jnp.int32, comb_ref.shape, 1)
    wgt = jnp.sum(jnp.where(lane == L_EXP + e, comb_ref[...], 0.0), -1, keepdims=True)
    acc_ref[...] += wgt * _dot(hid.astype(BF16), wd_ref[0])

    @pl.when(e == ne - 1)
    def _():
        y_ref[...] = _layer_norm(DEEP_ALPHA * h32_ref[...] + acc_ref[...], l2g_ref[...], l2b_ref[...])


def _moe_call(hb, h32, comb, wg, wu, wd, l2g, l2b, *, tm):
    rows, d = h32.shape
    ne, _, dff = wg.shape
    tspec = lambda width: pl.BlockSpec((tm, width), lambda i, e: (i, 0))
    full = lambda a: pl.BlockSpec(a.shape, lambda i, e: (0,) * a.ndim)
    return pl.pallas_call(
        _moe_kernel,
        grid=(rows // tm, ne),
        in_specs=[tspec(d), tspec(d), tspec(LANES),
                  pl.BlockSpec((1, d, dff), lambda i, e: (e, 0, 0)), pl.BlockSpec((1, d, dff), lambda i, e: (e, 0, 0)),
                  pl.BlockSpec((1, dff, d), lambda i, e: (e, 0, 0)), full(l2g), full(l2b)],
        out_specs=tspec(d),
        out_shape=jax.ShapeDtypeStruct((rows, d), F32),
        scratch_shapes=[pltpu.VMEM((tm, d), F32)],
        compiler_params=_cparams(("arbitrary", "arbitrary")),
        name="moe",
    )(hb, h32, comb, wg, wu, wd, l2g, l2b)


def _pick(n, prefs):
    for t in prefs:
        if n % t == 0:
            return t
    return n


def _rearranged_w_in(w_in):
    hk = H_A * DK_A
    widths = (hk, hk, W_A, W_A, H_A, H_A, W_A, W_A, W_A, H_B, w_in.shape[1] - (4 * hk + 2 * H_A + 3 * W_A + H_B))
    offs = [0]
    for wd in widths:
        offs.append(offs[-1] + wd)
    sl = lambda i: w_in[:, offs[i]:offs[i + 1]]
    gq, gk, gv, gz, ga, gb, fq, fk, fv, ff, gates = (sl(i) for i in range(11))
    pad = jnp.zeros((w_in.shape[0], LANES - 4 * H_A), w_in.dtype)
    return jnp.concatenate([gq, gk, gv, gz, fq, fk, fv, gates, ga, gb, ff, ff, pad], 1).astype(BF16)


def kernel(x_prompt, x_sample, cache_k, cache_v, cache_logf, state_gdn, state_conv, page_table, meta_tokens, ln_in_g, ln_in_b, w_in, conv_w, a_log, dt_bias, f_bias, gdn_norm_w, w_branch, w_out, ln1_g, ln1_b, w_router_group, b_router_group, w_router_expert, b_router_expert, w_gate, w_up, w_down, ln2_g, ln2_b):
    bsz, seq, d = x_prompt.shape
    n_dec = x_sample.shape[0]
    row2 = lambda a: a.reshape(1, -1).astype(F32)
    lng, lnb = row2(ln_in_g), row2(ln_in_b)
    w_big = _rearranged_w_in(w_in[0])
    convw = conv_w[0].astype(F32)
    zpad = lambda a, n: jnp.pad(a.astype(F32), (0, n - a.shape[0]))
    sp = jnp.zeros((8, LANES), F32)
    sp = sp.at[0].set(zpad(a_log[0], LANES)).at[1].set(zpad(dt_bias[0], LANES))
    sp = sp.at[2].set(zpad(jnp.concatenate([jnp.zeros((L_LOGF,), F32), f_bias[0], f_bias[0]]), LANES))

    meta = _front_call(meta_tokens.astype(F32)[None], lng, lnb, w_big, convw, jnp.zeros((1, 8, QKV), F32), sp,
                       jnp.zeros((1, LANES), F32), tm=N_META, emit_t=False)
    pad_rows = CHUNK_A - N_META
    front_pad = lambda a: jnp.pad(a, ((0, 0), (pad_rows, 0), (0, 0)))
    sm_meta = front_pad(meta["small"])
    _, s_meta = _gdn_call(front_pad(meta["qa"]), front_pad(meta["ka"]), front_pad(meta["va"]), sm_meta,
                          jnp.swapaxes(sm_meta, 1, 2)[:, :16], jnp.zeros((1, H_A, DK_A, DV_A), F32), lt=CHUNK_A)
    c_meta = meta["small"][0, :, L_C:L_C + H_B]
    c0 = jnp.zeros((1, LANES), F32).at[0, L_C:L_C + H_B].set(c_meta[-1])
    rnd = lambda a: a.astype(BF16).astype(F32)
    bias = jnp.pad(-c_meta * LOG2E, ((0, LANES - N_META), (0, 0)), constant_values=NEG_INF)
    hi = rnd(bias)
    mid = rnd(bias - hi)
    aug = jnp.zeros((LANES, H_B, DH_B), F32).at[:, :, 0].set(hi).at[:, :, 1].set(mid).at[:, :, 2].set(bias - hi - mid)
    kpad = jnp.pad(meta["fk32"][0], ((0, LANES - N_META), (0, 0))).reshape(LANES, H_B, DH_B)
    kmaug = jnp.concatenate([kpad, aug], -1).reshape(LANES, 2 * H_B * DH_B).astype(BF16)
    vmt = jnp.pad(meta["fv32"][0], ((0, LANES - N_META), (0, 0))).T.astype(BF16)

    tm = _pick(seq, (512, 256, 128, 64))
    pf = _front_call(x_prompt, lng, lnb, w_big, convw, meta["tail"], sp, c0, tm=tm, emit_t=True)
    o_a, s_prompt = _gdn_call(pf["qa"], pf["ka"], pf["va"], pf["small"], pf["small_t"], s_meta,
                              lt=_pick(seq, (256, 128, 64)))
    c_t = pf["small_t"][:, L_C:L_C + H_B].reshape(bsz, H_B, 1, seq)
    o_b = _foxt_call(pf["fqt"], pf["fkaug"], pf["fvt"], c_t, kmaug, vmt, tq=tm)

    nw = row2(gdn_norm_w[0])
    wbr = w_branch[0].astype(BF16)
    wout = w_out[0].astype(BF16)
    l1g, l1b, l2g, l2b = row2(ln1_g[0]), row2(ln1_b[0]), row2(ln2_g[0]), row2(ln2_b[0])
    wr = jnp.pad(jnp.concatenate([w_router_group[0], w_router_expert[0]], 1).astype(F32),
                 ((0, 0), (0, LANES - N_GROUPS - N_EXPERTS)))
    wr_hi = wr.astype(BF16)
    wr = jnp.concatenate([wr_hi, wr_hi, (wr - wr_hi.astype(F32)).astype(BF16)], 0)
    br = jnp.pad(jnp.concatenate([b_router_group[0], b_router_expert[0]]).astype(F32),
                 (0, LANES - N_GROUPS - N_EXPERTS)).reshape(1, LANES)
    wg, wu, wd = w_gate[0].astype(BF16), w_up[0].astype(BF16), w_down[0].astype(BF16)

    def layer_back(x2, oa2, z2, ob2, gates2, tile, moe_tile):
        h32, hb, comb = _back_call(x2, oa2, z2, ob2, gates2, lng, lnb, nw, wbr, wout, l1g, l1b, wr, br, tm=tile)
        return _moe_call(hb, h32, comb, wg, wu, wd, l2g, l2b, tm=moe_tile)

    flat = lambda a: a.reshape(bsz * seq, a.shape[-1])
    n_tok = bsz * seq
    y_prompt = layer_back(flat(x_prompt), flat(o_a), flat(pf["z"]), flat(o_b), flat(pf["gates"]),
                          _pick(n_tok, (512, 256, 128, 64)), _pick(n_tok, (1024, 512, 256, 128, 64))).reshape(bsz, seq, d)

    st = jnp.swapaxes(state_conv[0].astype(F32), 0, 1)
    sf = _sample_front_call(x_sample.reshape(n_dec, d), lng, lnb, w_big, convw, st, sp)
    o_a_s, s_sample = _gdn_step_call(sf["q"], sf["k"], sf["v"], sf["small"], state_gdn[0].astype(F32))
    n_pages = page_table.shape[1]
    o_b_s = _fox_sample_call(page_table, cache_k[0], cache_v[0], cache_logf[0], sf["fq"], sf["fk"], sf["fv"],
                             sf["small"][:, L_LOGF:L_LOGF + H_B], pps=_pick(n_pages, (8, 4, 2, 1)))
    y_sample = layer_back(x_sample.reshape(n_dec, d), o_a_s, sf["z"], o_b_s, sf["gates"], n_dec, n_dec)

    heads = lambda a: a.reshape(a.shape[:-1] + (H_B, DH_B))
    with_meta = lambda m, p: jnp.concatenate([jnp.broadcast_to(m, (bsz,) + m.shape[1:]), p], 1)
    rows = lambda a: a[0].reshape(N_META * H_B, DH_B)
    k_rows, v_rows = _put_meta_rows(pf["kout"], pf["vout"], rows(meta["fk32"]), rows(meta["fv32"]))
    k_prompt = k_rows.reshape(1, bsz, seq + N_META, H_B, DH_B)
    v_prompt = v_rows.reshape(1, bsz, seq + N_META, H_B, DH_B)
    logf_prompt = with_meta(meta["small"][:, :, L_LOGF:L_LOGF + H_B], pf["small"][:, :, L_LOGF:L_LOGF + H_B])[None]
    conv_prompt = pf["tail"][:, 8 - (CONV_W - 1):][None]
    conv_sample = jnp.concatenate([state_conv[0][:, 1:].astype(F32), sf["new"][:, None]], 1)[None]
    return (y_prompt, y_sample.reshape(n_dec, 1, d), k_prompt, v_prompt, logf_prompt, s_prompt[None], conv_prompt,
            heads(sf["fk"])[None, :, None], heads(sf["fv"])[None, :, None],
            sf["small"][:, L_LOGF:L_LOGF + H_B][None, :, None], s_sample[None], conv_sample)
```

```python
import functools

import jax
import jax.numpy as jnp
from jax import lax
from jax.experimental import pallas as pl
from jax.experimental.pallas import tpu as pltpu

F32 = jnp.float32
BF16 = jnp.bfloat16
HIGHEST = lax.Precision.HIGHEST

N_META = 16
H_A = 4
DK_A = 128
DV_A = 128
CONV_W = 4
CHUNK_A = 64
H_B = 4
DH_B = 128
N_GROUPS = 4
EXPERTS_PER_GROUP = 4
N_EXPERTS = N_GROUPS * EXPERTS_PER_GROUP
D_FF_E = 512
DEPTH = 1
DEEP_ALPHA = (2 * DEPTH) ** 0.25
LN_EPS = 1e-5
RMS_EPS = 1e-6
NEG_INF = -1e30
LOG2E = 1.4426950408889634

LANES = 128
W_A = H_A * DV_A
QKV = 3 * W_A
C_QKV = 0
C_Z = C_QKV + QKV
C_FQ = C_Z + W_A
C_FK = C_FQ + W_A
C_FV = C_FK + W_A
C_GATES = C_FV + W_A
C_SMALL = C_GATES + 2 * 1024
P_COLS = C_SMALL + LANES
L_G, L_BETA, L_LOGF, L_C = 0, 4, 8, 12
L_GRP, L_EXP = 0, 4
L_GID = 32

VMEM_LIMIT = 56 * 1024 * 1024


def _cparams(sem, vmem=VMEM_LIMIT):
    return pltpu.CompilerParams(dimension_semantics=sem, vmem_limit_bytes=vmem)


def _layer_norm(x, g, b):
    mu = jnp.mean(x, -1, keepdims=True)
    xc = x - mu
    var = jnp.mean(xc * xc, -1, keepdims=True)
    return xc * lax.rsqrt(var + LN_EPS) * g + b


def _softplus(x):
    return jnp.maximum(x, 0.0) + jnp.log1p(jnp.exp(-jnp.abs(x)))


def _sigmoid(x):
    return 1.0 / (1.0 + jnp.exp(-x))


def _dot(a, b):
    return jnp.dot(a, b, preferred_element_type=F32)


def _dot_nt(a, b):
    return lax.dot_general(a, b, (((1,), (1,)), ((), ())), preferred_element_type=F32)


def _split3(x):
    hi = x.astype(BF16)
    r1 = x - hi.astype(F32)
    mid = r1.astype(BF16)
    return hi, mid, (r1 - mid.astype(F32)).astype(BF16)


def _dot_split3(a, x):
    n = x.shape[1]
    y = _dot(a, jnp.concatenate(_split3(x), 1))
    return y[:, :n] + y[:, n:2 * n] + y[:, 2 * n:]


def _split3_dot(x, b):
    m = x.shape[0]
    y = _dot(jnp.concatenate(_split3(x), 0), b)
    return y[:m] + y[m:2 * m] + y[2 * m:]


def _dot_tn(a, b):
    return lax.dot_general(a, b, (((0,), (0,)), ((), ())), preferred_element_type=F32)


def _gate_values(sm, sp, rows):
    lane = lax.broadcasted_iota(jnp.int32, (rows, LANES), 1)
    g = -jnp.exp(sp[0:1]) * _softplus(sm + sp[1:2])
    beta = _sigmoid(sm)
    logf = -_softplus(-(sm + sp[2:3]))
    return jnp.where(lane < L_BETA, g, jnp.where(lane < L_LOGF, beta, jnp.where(lane < L_C + 4, logf, 0.0)))


def _front_kernel(x_ref, lng_ref, lnb_ref, w_ref, convw_ref, halo_ref, sp_ref, c0_ref, *rest, names, tm, blk):
    o = dict(zip(names, rest))
    buf_ref, carry_ref = rest[len(names):]
    qa_ref, ka_ref, va_ref, tail_ref, small_ref = o["qa"], o["ka"], o["va"], o["tail"], o["small"]
    j = pl.program_id(1)
    nj = pl.num_programs(1)

    @pl.when(j == 0)
    def _():
        buf_ref[0:8, :] = halo_ref[0]
        carry_ref[...] = c0_ref[...]

    xn = _layer_norm(x_ref[0], lng_ref[...], lnb_ref[...]).astype(BF16)

    def proj(c0, width):
        return _dot(xn, w_ref[:, c0:c0 + width])

    buf_ref[8:8 + tm, :] = proj(C_QKV, QKV)
    outs = (qa_ref, ka_ref, va_ref)
    for grp in range(QKV // LANES):
        c = grp * LANES
        w = convw_ref[:, c:c + LANES]
        y = (buf_ref[5:5 + tm, c:c + LANES] * w[0:1] + buf_ref[6:6 + tm, c:c + LANES] * w[1:2]
             + buf_ref[7:7 + tm, c:c + LANES] * w[2:3] + buf_ref[8:8 + tm, c:c + LANES] * w[3:4])
        y = y * _sigmoid(y)
        which, head = divmod(grp, H_A)
        if which < 2:
            y = y * lax.rsqrt(jnp.sum(y * y, -1, keepdims=True) + RMS_EPS)
            if which == 0:
                y = y * DK_A ** -0.5
        outs[which][0, :, head * LANES:(head + 1) * LANES] = y.astype(BF16)

    @pl.when(j == nj - 1)
    def _():
        tail_ref[0] = buf_ref[tm:tm + 8, :]

    buf_ref[0:8, :] = buf_ref[tm:tm + 8, :]

    fk = proj(C_FK, W_A)
    fv = proj(C_FV, W_A)
    if "fk32" in o:
        o["fk32"][0] = fk
        o["fv32"][0] = fv
    if "kout" in o:
        for h in range(H_B):
            o["kout"][0, pl.ds(h, tm, stride=H_B), :] = fk[:, h * DH_B:(h + 1) * DH_B]
            o["vout"][0, pl.ds(h, tm, stride=H_B), :] = fv[:, h * DH_B:(h + 1) * DH_B]
    if "z" in o:
        o["z"][0] = proj(C_Z, W_A).astype(BF16)
        o["gates"][0] = proj(C_GATES, 2 * 1024).astype(BF16)
        o["fqt"][0] = (proj(C_FQ, W_A) * (DH_B ** -0.5 * LOG2E)).T.astype(BF16)
        o["fvt"][0] = fv.T.astype(BF16)

    vals = _gate_values(proj(C_SMALL, LANES), sp_ref[...], tm)
    span = min(LANES, tm)
    r = lax.broadcasted_iota(jnp.int32, (span, span), 0)
    c = lax.broadcasted_iota(jnp.int32, (span, span), 1)
    same_blk = c >= (r // blk) * blk
    tri = jnp.where(c <= r, jnp.where(same_blk, 1.0, 0.0), 0.0).astype(BF16)
    parts = [_dot_split3(tri, vals[i * span:(i + 1) * span]) for i in range(tm // span)]
    cs = parts[0] if len(parts) == 1 else jnp.concatenate(parts, 0)
    run = carry_ref[...]
    pieces = []
    for b in range(tm // blk):
        piece = cs[b * blk:(b + 1) * blk, :] + run
        pieces.append(piece)
        run = piece[blk - 1:blk, :]
    carry_ref[...] = run
    cfull = pieces[0] if len(pieces) == 1 else jnp.concatenate(pieces, 0)
    lane = lax.broadcasted_iota(jnp.int32, (tm, LANES), 1)
    small = jnp.where(lane < L_BETA, cs, jnp.where(lane < L_C, vals, cfull))
    small_ref[0] = small
    if "small_t" in o:
        o["small_t"][0] = small.T[0:16, :]
        rnd = lambda a: a.astype(BF16).astype(F32)
        bias = (cfull[0:1, :] - cfull) * LOG2E
        hi = rnd(bias)
        mid = rnd(bias - hi)
        lo = bias - hi - mid
        for h in range(H_B):
            col = slice(L_C + h, L_C + h + 1)
            aug = jnp.where(lane == 0, hi[:, col], jnp.where(lane == 1, mid[:, col], jnp.where(lane == 2, lo[:, col], 0.0)))
            o["fkaug"][0, :, 2 * h * DH_B:(2 * h + 1) * DH_B] = fk[:, h * DH_B:(h + 1) * DH_B].astype(BF16)
            o["fkaug"][0, :, (2 * h + 1) * DH_B:(2 * h + 2) * DH_B] = aug.astype(BF16)


def _front_call(x, lng, lnb, w_big, conv_w, halo, sp, c0, *, tm, emit_t):
    bsz, seq, d = x.shape
    blk = min(CHUNK_A, tm)
    nj = seq // tm
    full2 = lambda a: pl.BlockSpec(a.shape, lambda b, j: (0, 0))
    tok = lambda width, dt: (jax.ShapeDtypeStruct((bsz, seq, width), dt),
                             pl.BlockSpec((1, tm, width), lambda b, j: (b, j, 0)))
    tok_t = lambda rows, dt: (jax.ShapeDtypeStruct((bsz, rows, seq), dt),
                              pl.BlockSpec((1, rows, tm), lambda b, j: (b, 0, j)))
    outs = dict(qa=tok(W_A, BF16), ka=tok(W_A, BF16), va=tok(W_A, BF16), small=tok(LANES, F32),
                tail=(jax.ShapeDtypeStruct((bsz, 8, QKV), F32), pl.BlockSpec((1, 8, QKV), lambda b, j: (b, 0, 0))))
    if emit_t:
        skip = N_META * H_B
        kv = (jax.ShapeDtypeStruct((bsz, skip + seq * H_B, DH_B), F32),
              pl.BlockSpec((pl.Element(1), pl.Element(tm * H_B), pl.Element(DH_B)),
                           lambda b, j: (b, pl.multiple_of(skip + j * (tm * H_B), skip), 0)))
        outs.update(kout=kv, vout=kv, z=tok(W_A, BF16), gates=tok(2 * 1024, BF16), small_t=tok_t(16, F32),
                    fqt=tok_t(W_A, BF16), fvt=tok_t(W_A, BF16), fkaug=tok(2 * W_A, BF16))
    else:
        outs.update(fk32=tok(W_A, F32), fv32=tok(W_A, F32))
    names = tuple(outs)

    res = pl.pallas_call(
        functools.partial(_front_kernel, names=names, tm=tm, blk=blk),
        grid=(bsz, nj),
        in_specs=[pl.BlockSpec((1, tm, d), lambda b, j: (b, j, 0)), full2(lng), full2(lnb),
                  pl.BlockSpec(w_big.shape, lambda b, j: (0, 0), pipeline_mode=pl.Buffered(1)),
                  full2(conv_w), pl.BlockSpec((1, 8, QKV), lambda b, j: (0, 0, 0)), full2(sp), full2(c0)],
        out_specs=[outs[n][1] for n in names],
        out_shape=[outs[n][0] for n in names],
        scratch_shapes=[pltpu.VMEM((tm + 8, QKV), F32), pltpu.VMEM((1, LANES), F32)],
        compiler_params=_cparams(("arbitrary", "arbitrary")),
        name="front",
    )(x, lng, lnb, w_big, conv_w, halo, sp, c0)
    return dict(zip(names, res))


def _put_meta_rows_kernel(k_any, v_any, km_ref, vm_ref, kout_ref, vout_ref):
    del k_any, v_any
    kout_ref[0] = km_ref[...]
    vout_ref[0] = vm_ref[...]


def _put_meta_rows(kout, vout, km, vm):
    bsz = kout.shape[0]
    head = pl.BlockSpec((1,) + km.shape, lambda b: (b, 0, 0))
    small = pl.BlockSpec(km.shape, lambda b: (0, 0))
    anyspec = pl.BlockSpec(memory_space=pl.ANY)
    return pl.pallas_call(
        _put_meta_rows_kernel,
        grid=(bsz,),
        in_specs=[anyspec, anyspec, small, small],
        out_specs=[head, head],
        out_shape=[jax.ShapeDtypeStruct(kout.shape, kout.dtype), jax.ShapeDtypeStruct(vout.shape, vout.dtype)],
        input_output_aliases={0: 0, 1: 1},
        compiler_params=_cparams(("arbitrary",)),
        name="put_meta_rows",
    )(kout, vout, km, vm)


def _sample_front_kernel(x_ref, lng_ref, lnb_ref, w_ref, convw_ref, st_ref, sp_ref,
                         q_ref, k_ref, v_ref, z_ref, fq_ref, fk_ref, fv_ref, gates_ref, small_ref, new_ref):
    rows = x_ref.shape[0]
    xn = _layer_norm(x_ref[...], lng_ref[...], lnb_ref[...]).astype(BF16)

    def proj(c0, width):
        return _dot(xn, w_ref[:, c0:c0 + width])

    new = proj(C_QKV, QKV)
    new_ref[...] = new
    w = convw_ref[...]
    y = st_ref[0] * w[0:1] + st_ref[1] * w[1:2] + st_ref[2] * w[2:3] + new * w[3:4]
    y = y * _sigmoid(y)
    outs = (q_ref, k_ref, v_ref)
    for grp in range(QKV // LANES):
        which, head = divmod(grp, H_A)
        yy = y[:, grp * LANES:(grp + 1) * LANES]
        if which < 2:
            yy = yy * lax.rsqrt(jnp.sum(yy * yy, -1, keepdims=True) + RMS_EPS)
            if which == 0:
                yy = yy * DK_A ** -0.5
        outs[which][:, head * LANES:(head + 1) * LANES] = yy
    z_ref[...] = proj(C_Z, W_A)
    fq_ref[...] = proj(C_FQ, W_A) * DH_B ** -0.5
    fk_ref[...] = proj(C_FK, W_A)
    fv_ref[...] = proj(C_FV, W_A)
    gates_ref[...] = proj(C_GATES, 2 * 1024)
    small_ref[...] = _gate_values(proj(C_SMALL, LANES), sp_ref[...], rows)


def _sample_front_call(x, lng, lnb, w_big, conv_w, st, sp):
    rows = x.shape[0]
    o = lambda width: jax.ShapeDtypeStruct((rows, width), F32)
    res = pl.pallas_call(
        _sample_front_kernel,
        out_shape=[o(W_A), o(W_A), o(W_A), o(W_A), o(W_A), o(W_A), o(W_A), o(2 * 1024), o(LANES), o(QKV)],
        compiler_params=pltpu.CompilerParams(vmem_limit_bytes=VMEM_LIMIT),
        name="sample_front",
    )(x, lng, lnb, w_big, conv_w, st, sp)
    return dict(zip(["q", "k", "v", "z", "fq", "fk", "fv", "gates", "small", "new"], res))


def _gdn_kernel(q_ref, k_ref, v_ref, sm_ref, smt_ref, s0_ref, o_ref, sout_ref, state_ref, *, lt):
    j = pl.program_id(0)
    nj = pl.num_programs(0)
    ch = CHUNK_A
    bsz = q_ref.shape[0]

    @pl.when(j == 0)
    def _():
        for b in range(bsz):
            state_ref[b] = s0_ref[0]

    ri = lax.broadcasted_iota(jnp.int32, (ch, ch), 0)
    ci = lax.broadcasted_iota(jnp.int32, (ch, ch), 1)
    incl = ri >= ci
    strict = ri > ci
    units = [(b, c, h) for b in range(bsz) for c in range(lt // ch) for h in range(H_A)]
    t = {}
    for u in units:
        b, c, h = u
        rows = slice(c * ch, (c + 1) * ch)
        cols = slice(h * DK_A, (h + 1) * DK_A)
        kc = k_ref[b, rows, cols]
        qc = q_ref[b, rows, cols]
        gc_col = sm_ref[b, rows, L_G + h:L_G + h + 1]
        beta = sm_ref[b, rows, L_BETA + h:L_BETA + h + 1]
        gc_row = smt_ref[b, L_G + h:L_G + h + 1, rows]
        gc_last = sm_ref[b, c * ch + ch - 1:c * ch + ch, L_G + h:L_G + h + 1]
        eg = jnp.exp(gc_col)
        kf = kc.astype(F32)
        kb = kf * beta
        t[u] = dict(kc=kc, decay=jnp.exp(jnp.where(incl, gc_col - gc_row, NEG_INF)),
                    lhs=jnp.concatenate([kb.astype(BF16), qc], 0),
                    rhs=jnp.concatenate([v_ref[b, rows, cols].astype(F32) * beta, kb * eg], 1),
                    q_dec=(qc.astype(F32) * eg).astype(BF16),
                    k_dec=(kf * jnp.exp(gc_last - gc_col)).astype(BF16), g_tot=jnp.exp(gc_last))
    for u in units:
        d = t[u]
        a = _dot_nt(d["lhs"], d["kc"])
        d["m"] = jnp.where(strict, a[:ch] * d["decay"], 0.0)
        d["attn"] = (a[ch:] * d["decay"]).astype(BF16)
    for u in units:
        d = t[u]
        mb = d["m"].astype(BF16)
        d["x"] = -d["m"]
        d["p"] = _dot(mb, mb)
    for _ in range(4):
        for u in units:
            d = t[u]
            pb = d["p"].astype(BF16)
            r = _dot(jnp.concatenate([pb, d["x"].astype(BF16)], 0), pb)
            d["x"] = d["x"] + d["p"] + r[ch:]
            d["p"] = r[:ch]
    for u in units:
        d = t[u]
        d["x"] = d["x"] + d["p"] + _dot(d["x"].astype(BF16), d["p"].astype(BF16))
    for u in units:
        d = t[u]
        sol = d["rhs"] + _dot(d["x"].astype(BF16), d["rhs"].astype(BF16))
        d["u"] = sol[:, :DV_A]
        d["wq"] = jnp.concatenate([sol[:, DV_A:].astype(BF16), d["q_dec"]], 0)
    heads = [(b, h) for b in range(bsz) for h in range(H_A)]
    s = {bh: state_ref[bh[0], bh[1]] for bh in heads}
    for c in range(lt // ch):
        ws = {bh: _dot(t[bh[0], c, bh[1]]["wq"], s[bh].astype(BF16)) for bh in heads}
        v_new = {bh: (t[bh[0], c, bh[1]]["u"] - ws[bh][:ch]).astype(BF16) for bh in heads}
        for bh in heads:
            b, h = bh
            d = t[b, c, h]
            o = ws[bh][ch:] + _dot(d["attn"], v_new[bh])
            o_ref[b, c * ch:(c + 1) * ch, h * DV_A:(h + 1) * DV_A] = o.astype(o_ref.dtype)
        for bh in heads:
            d = t[bh[0], c, bh[1]]
            s[bh] = s[bh] * d["g_tot"] + _dot_tn(d["k_dec"], v_new[bh])
    for bh in heads:
        state_ref[bh[0], bh[1]] = s[bh]

    @pl.when(j == nj - 1)
    def _():
        sout_ref[...] = state_ref[...]


def _gdn_call(q, k, v, small, small_t, s0, *, lt):
    bsz, seq, _ = q.shape
    nj = seq // lt
    tspec = pl.BlockSpec((bsz, lt, W_A), lambda j: (0, j, 0))
    sshape = (bsz, H_A, DK_A, DV_A)
    return pl.pallas_call(
        functools.partial(_gdn_kernel, lt=lt),
        grid=(nj,),
        in_specs=[tspec, tspec, tspec, pl.BlockSpec((bsz, lt, LANES), lambda j: (0, j, 0)),
                  pl.BlockSpec((bsz, 16, lt), lambda j: (0, 0, j)),
                  pl.BlockSpec((1, H_A, DK_A, DV_A), lambda j: (0, 0, 0, 0))],
        out_specs=[tspec, pl.BlockSpec(sshape, lambda j: (0, 0, 0, 0))],
        out_shape=[jax.ShapeDtypeStruct((bsz, seq, W_A), BF16), jax.ShapeDtypeStruct(sshape, F32)],
        scratch_shapes=[pltpu.VMEM(sshape, F32)],
        compiler_params=_cparams(("arbitrary",)),
        name="gdn_chunk",
    )(q, k, v, small, small_t, s0)


def _gdn_step_kernel(q_ref, k_ref, v_ref, sm_ref, s_ref, o_ref, sout_ref):
    ri = lax.broadcasted_iota(jnp.int32, (DK_A, DK_A), 0)
    ci = lax.broadcasted_iota(jnp.int32, (DK_A, DK_A), 1)
    eye = ri == ci

    def column(row):
        return jnp.sum(jnp.where(eye, jnp.broadcast_to(row, (DK_A, DK_A)), 0.0), -1, keepdims=True)

    sm = sm_ref[0]
    for h in range(H_A):
        cols = slice(h * DK_A, (h + 1) * DK_A)
        kcol = column(k_ref[0, :, cols])
        qcol = column(q_ref[0, :, cols])
        s = s_ref[0, h] * jnp.exp(sm[:, L_G + h:L_G + h + 1])
        v_t = (v_ref[0, :, cols] - jnp.sum(kcol * s, 0, keepdims=True)) * sm[:, L_BETA + h:L_BETA + h + 1]
        s = s + kcol * v_t
        sout_ref[0, h] = s
        o_ref[0, :, cols] = jnp.sum(qcol * s, 0, keepdims=True)


def _gdn_step_call(q, k, v, small, state):
    n = q.shape[0]
    row = lambda a: a.reshape(n, 1, a.shape[-1])
    rspec = lambda width: pl.BlockSpec((1, 1, width), lambda b: (b, 0, 0))
    sspec = pl.BlockSpec((1, H_A, DK_A, DV_A), lambda b: (b, 0, 0, 0))
    o, s = pl.pallas_call(
        _gdn_step_kernel,
        grid=(n,),
        in_specs=[rspec(W_A), rspec(W_A), rspec(W_A), rspec(LANES), sspec],
        out_specs=[rspec(W_A), sspec],
        out_shape=[jax.ShapeDtypeStruct((n, 1, W_A), F32), jax.ShapeDtypeStruct(state.shape, F32)],
        compiler_params=_cparams(("arbitrary",)),
        name="gdn_step",
    )(row(q), row(k), row(v), row(small), state)
    return o.reshape(n, W_A), s


def _foxt_kernel(qt_ref, ka_ref, vt_ref, ct_ref, kma_ref, vmt_ref, o_ref, m_sc, l_sc, acc_sc, *, tq, sub):
    qi = pl.program_id(1)
    qstart = pl.multiple_of(qi * tq, tq)
    heads = range(H_B)
    hs = lambda h: slice(h * DH_B, (h + 1) * DH_B)
    ha = lambda h: slice(2 * h * DH_B, (2 * h + 2) * DH_B)
    ones = jnp.where(lax.broadcasted_iota(jnp.int32, (DH_B, tq), 0) < 3, 1.0, 0.0).astype(BF16)
    qa = [jnp.concatenate([qt_ref[0, hs(h), :], ones], 0) for h in heads]
    c_first = [ct_ref[0, h, :, pl.ds(qstart, LANES)][:, 0:1] for h in heads]

    def step(keys, vt, offs, mask, first):
        s = [_dot(keys[h], qa[h]) for h in heads]
        p, alpha = [], []
        for h in heads:
            sh = s[h] if mask is None else jnp.where(mask, s[h], NEG_INF)
            blk_max = jnp.max(sh, 0, keepdims=True) + offs[h]
            m_new = blk_max if first else jnp.maximum(m_sc[h], blk_max)
            ph = jnp.exp2(sh - (m_new - offs[h]))
            row_sum = jnp.sum(ph, 0, keepdims=True)
            if first:
                l_sc[h] = row_sum
            else:
                a = jnp.exp2(m_sc[h] - m_new)
                l_sc[h] = a * l_sc[h] + row_sum
                alpha.append(a)
            m_sc[h] = m_new
            p.append(ph.astype(BF16))
        for h in heads:
            pv = _dot(vt[h], p[h])
            acc_sc[h] = pv if first else alpha[h] * acc_sc[h] + pv

    step([kma_ref[:, ha(h)] for h in heads], [vmt_ref[hs(h), :] for h in heads],
         [c_first[h] * LOG2E for h in heads], None, True)

    def block(start, masked):
        offs = [(c_first[h] - ct_ref[0, h, :, pl.ds(start, LANES)][:, 0:1]) * LOG2E for h in heads]
        for part in range(tq // sub):
            lo = start + part * sub
            mask = None
            if masked:
                key = lax.broadcasted_iota(jnp.int32, (sub, tq), 0) + part * sub
                mask = key <= lax.broadcasted_iota(jnp.int32, (sub, tq), 1)
            step([ka_ref[0, pl.ds(lo, sub), ha(h)] for h in heads],
                 [vt_ref[0, hs(h), pl.ds(lo, sub)] for h in heads], offs, mask, False)

    def body(jb, carry):
        block(pl.multiple_of(jb * tq, tq), False)
        return carry

    lax.fori_loop(0, qi, body, 0)
    block(qstart, True)
    for h in heads:
        o_ref[0, :, hs(h)] = (acc_sc[h] / l_sc[h]).T.astype(o_ref.dtype)


def _foxt_call(fqt, fkaug, fvt, c_t, kmaug, vmt, *, tq):
    bsz, hd, seq = fqt.shape
    once = dict(pipeline_mode=pl.Buffered(1))
    return pl.pallas_call(
        functools.partial(_foxt_kernel, tq=tq, sub=min(256, tq)),
        grid=(bsz, seq // tq),
        in_specs=[pl.BlockSpec((1, hd, tq), lambda b, i: (b, 0, i)),
                  pl.BlockSpec((1, seq, 2 * hd), lambda b, i: (b, 0, 0), **once),
                  pl.BlockSpec((1, hd, seq), lambda b, i: (b, 0, 0), **once),
                  pl.BlockSpec((1, H_B, 1, seq), lambda b, i: (b, 0, 0, 0)),
                  pl.BlockSpec(kmaug.shape, lambda b, i: (0, 0)), pl.BlockSpec(vmt.shape, lambda b, i: (0, 0))],
        out_specs=pl.BlockSpec((1, tq, hd), lambda b, i: (b, i, 0)),
        out_shape=jax.ShapeDtypeStruct((bsz, seq, hd), BF16),
        scratch_shapes=[pltpu.VMEM((H_B, 1, tq), F32), pltpu.VMEM((H_B, 1, tq), F32), pltpu.VMEM((H_B, DH_B, tq), F32)],
        compiler_params=_cparams(("arbitrary", "arbitrary")),
        name="fox_prompt",
    )(fqt, fkaug, fvt, c_t, kmaug, vmt)


def _fox_sample_kernel(pt_ref, *refs, pps):
    k_refs = refs[:pps]
    v_refs = refs[pps:2 * pps]
    lf_refs = refs[2 * pps:3 * pps]
    q_ref, kn_ref, vn_ref, lfn_ref, wsuf_ref, o_ref, m_sc, l_sc, run_sc, acc_sc = refs[3 * pps:]
    j = pl.program_id(1)
    nj = pl.num_programs(1)
    rows = k_refs[0].shape[1]
    q8 = q_ref[0]

    @pl.when(j == 0)
    def _():
        run_sc[...] = lfn_ref[0]
        m_sc[...] = jnp.broadcast_to(jnp.sum(q8 * kn_ref[0], -1, keepdims=True), (8, LANES))
        l_sc[...] = jnp.ones((8, LANES), F32)
        acc_sc[...] = vn_ref[0]

    qb = q8.astype(BF16)
    lf = jnp.concatenate([lf_refs[i][0] for i in range(pps)], 0)
    within = _split3_dot(lf, wsuf_ref[...])
    tot = jnp.broadcast_to(jnp.sum(lf, -1, keepdims=True), (8 * pps, LANES))
    run = run_sc[...]
    valid = (lax.broadcasted_iota(jnp.int32, (8, rows), 1) % H_B) == lax.broadcasted_iota(jnp.int32, (8, rows), 0)
    scores = []
    for i in range(pps):
        sl = slice(8 * i, 8 * i + 8)
        bias = within[sl] + run[:, 0:1]
        scores.append(jnp.where(valid, _dot_nt(qb, k_refs[i][0].astype(BF16)) + bias, NEG_INF))
        run = run + tot[sl]
    run_sc[...] = run
    s = jnp.concatenate(scores, 1)
    m_prev = m_sc[...][:, 0:1]
    m_new = jnp.maximum(m_prev, jnp.max(s, -1, keepdims=True))
    alpha = jnp.exp(m_prev - m_new)
    p = jnp.exp(s - m_new)
    l_sc[...] = jnp.broadcast_to(alpha * l_sc[...][:, 0:1] + jnp.sum(p, -1, keepdims=True), (8, LANES))
    m_sc[...] = jnp.broadcast_to(m_new, (8, LANES))
    acc = alpha * acc_sc[...]
    for i in range(pps):
        acc = acc + _dot(p[:, i * rows:(i + 1) * rows].astype(BF16), v_refs[i][0].astype(BF16))
    acc_sc[...] = acc

    @pl.when(j == nj - 1)
    def _():
        o_ref[0] = acc / l_sc[...][:, 0:1]


def _fox_sample_call(page_table, cache_k, cache_v, cache_logf, fq, fk, fv, logf_new, *, pps):
    n, n_pages = page_table.shape
    n_pool, page = cache_k.shape[0], cache_k.shape[1]
    rows = page * H_B
    ck = cache_k.reshape(n_pool, rows, DH_B)
    cv = cache_v.reshape(n_pool, rows, DH_B)
    clf = jnp.pad(jnp.swapaxes(cache_logf, 1, 2), ((0, 0), (0, 8 - H_B), (0, 0)))
    tile8 = lambda a: jnp.pad(a.reshape(n, H_B, DH_B), ((0, 0), (0, 8 - H_B), (0, 0)))
    lfn = jnp.broadcast_to(jnp.pad(logf_new, ((0, 0), (0, 8 - H_B)))[:, :, None], (n, 8, LANES))
    wsuf = (lax.broadcasted_iota(jnp.int32, (page, rows), 0)
            > lax.broadcasted_iota(jnp.int32, (page, rows), 1) // H_B).astype(BF16)

    def page_map(i):
        return lambda b, j, pt: (pt[b * n_pages + n_pages - 1 - (j * pps + i)], 0, 0)

    kv_specs = [pl.BlockSpec((1, rows, DH_B), page_map(i)) for i in range(pps)]
    lf_specs = [pl.BlockSpec((1, 8, page), page_map(i)) for i in range(pps)]
    tspec = pl.BlockSpec((1, 8, LANES), lambda b, j, pt: (b, 0, 0))
    full = lambda a: pl.BlockSpec(a.shape, lambda b, j, pt: (0, 0))
    grid_spec = pltpu.PrefetchScalarGridSpec(
        num_scalar_prefetch=1,
        grid=(n, n_pages // pps),
        in_specs=kv_specs + kv_specs + lf_specs + [tspec, tspec, tspec, tspec, full(wsuf)],
        out_specs=tspec,
        scratch_shapes=[pltpu.VMEM((8, LANES), F32)] * 4,
    )
    o = pl.pallas_call(
        functools.partial(_fox_sample_kernel, pps=pps),
        grid_spec=grid_spec,
        out_shape=jax.ShapeDtypeStruct((n, 8, DH_B), F32),
        compiler_params=_cparams(("arbitrary", "arbitrary")),
        name="fox_sample",
    )(page_table.reshape(-1), *([ck] * pps), *([cv] * pps), *([clf] * pps), tile8(fq), tile8(fk), tile8(fv), lfn,
      wsuf)
    return o[:, :H_B].reshape(n, H_B * DH_B)


def _back_kernel(x_ref, oa_ref, z_ref, ob_ref, gates_ref, lng_ref, lnb_ref, nw_ref, wbr_ref, wout_ref,
                 l1g_ref, l1b_ref, wr_ref, br_ref, h32_ref, hb_ref, comb_ref):
    tm = x_ref.shape[0]
    xn = _layer_norm(x_ref[...], lng_ref[...], lnb_ref[...])
    nw = nw_ref[...]
    parts = []
    for h in range(H_A):
        cols = slice(h * DV_A, (h + 1) * DV_A)
        o = oa_ref[:, cols].astype(F32)
        o = o * lax.rsqrt(jnp.mean(o * o, -1, keepdims=True) + RMS_EPS) * nw
        zz = z_ref[:, cols].astype(F32)
        parts.append((o * (zz * _sigmoid(zz))).astype(BF16))
    oa = jnp.concatenate(parts, 1)
    d = wout_ref.shape[0]
    merged = (_sigmoid(gates_ref[:, 0:d].astype(F32)) * _dot(oa, wbr_ref[0])
              + _sigmoid(gates_ref[:, d:2 * d].astype(F32)) * _dot(ob_ref[...].astype(BF16), wbr_ref[1]))
    mix = _dot(merged.astype(BF16), wout_ref[...])
    hh = _layer_norm(DEEP_ALPHA * xn + mix, l1g_ref[...], l1b_ref[...])
    h32_ref[...] = hh
    hb_ref[...] = hh.astype(BF16)

    h_hi = hh.astype(BF16)
    h_mid = (hh - h_hi.astype(F32)).astype(BF16)
    logits = _dot(jnp.concatenate([h_hi, h_mid, h_hi], 1), wr_ref[...]) + br_ref[...]
    lane = lax.broadcasted_iota(jnp.int32, (tm, LANES), 1)
    big = jnp.int32(LANES)
    is_grp = lane < L_EXP
    gl = jnp.where(is_grp, logits, NEG_INF)
    gmax = jnp.max(gl, -1, keepdims=True)
    gsel = jnp.min(jnp.where(jnp.logical_and(is_grp, gl == gmax), lane, big), -1, keepdims=True)
    gw = 1.0 / jnp.sum(jnp.exp(gl - gmax), -1, keepdims=True)
    lo = L_EXP + gsel * EXPERTS_PER_GROUP
    in_grp = jnp.logical_and(lane >= lo, lane < lo + EXPERTS_PER_GROUP)
    el = jnp.where(in_grp, logits, NEG_INF)
    m1 = jnp.max(el, -1, keepdims=True)
    i1 = jnp.min(jnp.where(el == m1, lane, big), -1, keepdims=True)
    el2 = jnp.where(lane == i1, NEG_INF, el)
    m2 = jnp.max(el2, -1, keepdims=True)
    i2 = jnp.min(jnp.where(el2 == m2, lane, big), -1, keepdims=True)
    e2 = jnp.exp(m2 - m1)
    w1 = gw / (1.0 + e2)
    w2 = gw * e2 / (1.0 + e2)
    comb_ref[...] = jnp.where(lane == i1, w1, jnp.where(lane == i2, w2,
                                                        jnp.where(lane == L_GID, gsel.astype(F32), 0.0)))


def _back_call(x, oa, z, ob, gates, lng, lnb, nw, wbr, wout, l1g, l1b, wr, br, *, tm):
    rows, d = x.shape
    tspec = lambda width: pl.BlockSpec((tm, width), lambda i: (i, 0))
    full = lambda a: pl.BlockSpec(a.shape, lambda i: (0,) * a.ndim)
    return pl.pallas_call(
        _back_kernel,
        grid=(rows // tm,),
        in_specs=[tspec(d), tspec(W_A), tspec(W_A), tspec(W_A), tspec(2 * d), full(lng), full(lnb), full(nw),
                  full(wbr), full(wout), full(l1g), full(l1b), full(wr), full(br)],
        out_specs=[tspec(d), tspec(d), tspec(LANES)],
        out_shape=[jax.ShapeDtypeStruct((rows, d), F32), jax.ShapeDtypeStruct((rows, d), BF16),
                   jax.ShapeDtypeStruct((rows, LANES), F32)],
        compiler_params=_cparams(("arbitrary",)),
        name="back",
    )(x, oa, z, ob, gates, lng, lnb, nw, wbr, wout, l1g, l1b, wr, br)


def _moe_kernel(hb_ref, h32_ref, comb_ref, wg_ref, wu_ref, wd_ref, l2g_ref, l2b_ref, y_ref, acc_ref):
    e = pl.program_id(1)
    ne = pl.num_programs(1)

    @pl.when(e == 0)
    def _():
        acc_ref[...] = jnp.zeros_like(acc_ref)

    x = hb_ref[...]
    g = _dot(x, wg_ref[0])
    hid = (g * _sigmoid(g)) * _dot(x, wu_ref[0])
    lane = lax.broadcasted_iota(jnp.int32, comb_ref.shape, 1)
    wgt = jnp.sum(jnp.where(lane == L_EXP + e, comb_ref[...], 0.0), -1, keepdims=True)
    acc_ref[...] += wgt * _dot(hid.astype(BF16), wd_ref[0])

    @pl.when(e == ne - 1)
    def _():
        y_ref[...] = _layer_norm(DEEP_ALPHA * h32_ref[...] + acc_ref[...], l2g_ref[...], l2b_ref[...])


def _moe_call(hb, h32, comb, wg, wu, wd, l2g, l2b, *, tm):
    rows, d = h32.shape
    ne, _, dff = wg.shape
    tspec = lambda width: pl.BlockSpec((tm, width), lambda i, e: (i, 0))
    full = lambda a: pl.BlockSpec(a.shape, lambda i, e: (0,) * a.ndim)
    return pl.pallas_call(
        _moe_kernel,
        grid=(rows // tm, ne),
        in_specs=[tspec(d), tspec(d), tspec(LANES),
                  pl.BlockSpec((1, d, dff), lambda i, e: (e, 0, 0)), pl.BlockSpec((1, d, dff), lambda i, e: (e, 0, 0)),
                  pl.BlockSpec((1, dff, d), lambda i, e: (e, 0, 0)), full(l2g), full(l2b)],
        out_specs=tspec(d),
        out_shape=jax.ShapeDtypeStruct((rows, d), F32),
        scratch_shapes=[pltpu.VMEM((tm, d), F32)],
        compiler_params=_cparams(("arbitrary", "arbitrary")),
        name="moe",
    )(hb, h32, comb, wg, wu, wd, l2g, l2b)


def _moe_grouped_kernel(hb_ref, comb_ref, wg_ref, wu_ref, wd_ref, o_ref, xs_ref, cs_ref, acc_ref, pt_ref, rng_ref,
                        *, ts):
    g = pl.program_id(1)
    t = hb_ref.shape[0]

    @pl.when(g == 0)
    def _():
        comb = comb_ref[...]
        lane = lax.broadcasted_iota(jnp.int32, (t, LANES), 1)
        gid = comb[:, L_GID:L_GID + 1].astype(jnp.int32)
        onehot = jnp.where(lane == gid, 1.0, 0.0)
        r = lax.broadcasted_iota(jnp.int32, (t, t), 0)
        c = lax.broadcasted_iota(jnp.int32, (t, t), 1)
        csum = _dot(jnp.where(c <= r, 1.0, 0.0).astype(BF16), onehot.astype(BF16))
        counts = csum[t - 1:t, :]
        before = (lax.broadcasted_iota(jnp.int32, (LANES, LANES), 0)
                  < lax.broadcasted_iota(jnp.int32, (LANES, LANES), 1)).astype(BF16)
        starts = _split3_dot(jnp.broadcast_to(counts, (8, LANES)), before)[0:1]
        pos = (jnp.sum(onehot * (starts + csum), -1, keepdims=True) - 1.0).astype(jnp.int32)
        pos_row = jnp.broadcast_to(pos.astype(F32), (t, LANES)).T[0:1, :].astype(jnp.int32)
        perm = jnp.where(r == pos_row, 1.0, 0.0).astype(BF16)
        xs_ref[...] = _dot(perm, hb_ref[...]).astype(BF16)
        cs_ref[...] = _dot_split3(perm, comb)
        pt_ref[...] = jnp.where(c == pos, 1.0, 0.0).astype(BF16)
        acc_ref[...] = jnp.zeros_like(acc_ref)
        for gg in range(N_GROUPS):
            rng_ref[gg] = starts[0, gg].astype(jnp.int32)
            rng_ref[N_GROUPS + gg] = counts[0, gg].astype(jnp.int32)

    start = rng_ref[g]
    stop = start + rng_ref[N_GROUPS + g]
    lane = lax.broadcasted_iota(jnp.int32, (ts, LANES), 1)

    def sub_tile(j, carry):
        r0 = pl.multiple_of(j * ts, ts)
        x = xs_ref[pl.ds(r0, ts), :]
        cw = cs_ref[pl.ds(r0, ts), :]
        out = None
        for e in range(EXPERTS_PER_GROUP):
            gate = _dot(x, wg_ref[0, e])
            hid = (gate * _sigmoid(gate)) * _dot(x, wu_ref[0, e])
            wgt = jnp.sum(jnp.where(lane == L_EXP + g * EXPERTS_PER_GROUP + e, cw, 0.0), -1, keepdims=True)
            term = wgt * _dot(hid.astype(BF16), wd_ref[0, e])
            out = term if out is None else out + term
        acc_ref[pl.ds(r0, ts), :] += out
        return carry

    lax.fori_loop(start // ts, (stop + ts - 1) // ts, sub_tile, 0)

    @pl.when(g == N_GROUPS - 1)
    def _():
        o_ref[...] = _dot(pt_ref[...], acc_ref[...].astype(BF16)).astype(o_ref.dtype)


def _moe_grouped_call(hb, comb, wg, wu, wd, *, tm, ts):
    rows, d = hb.shape
    dff = wg.shape[-1]
    grp = lambda a: a.reshape((N_GROUPS, EXPERTS_PER_GROUP) + a.shape[1:])
    tspec = lambda width: pl.BlockSpec((tm, width), lambda i, g: (i, 0))
    wspec = lambda a, b: pl.BlockSpec((1, EXPERTS_PER_GROUP, a, b), lambda i, g: (g, 0, 0, 0))
    return pl.pallas_call(
        functools.partial(_moe_grouped_kernel, ts=ts),
        grid=(rows // tm, N_GROUPS),
        in_specs=[tspec(d), tspec(LANES), wspec(d, dff), wspec(d, dff), wspec(dff, d)],
        out_specs=tspec(d),
        out_shape=jax.ShapeDtypeStruct((rows, d), BF16),
        scratch_shapes=[pltpu.VMEM((tm, d), BF16), pltpu.VMEM((tm, LANES), F32), pltpu.VMEM((tm, d), F32),
                        pltpu.VMEM((tm, tm), BF16), pltpu.SMEM((2 * N_GROUPS,), jnp.int32)],
        compiler_params=_cparams(("arbitrary", "arbitrary")),
        name="moe_grouped",
    )(hb, comb, grp(wg), grp(wu), grp(wd))


def _ln2_kernel(h_ref, f_ref, g_ref, b_ref, y_ref):
    y_ref[...] = _layer_norm(DEEP_ALPHA * h_ref[...] + f_ref[...].astype(F32), g_ref[...], b_ref[...])


def _ln2_call(h32, ffn, l2g, l2b, *, tm):
    rows, d = h32.shape
    tspec = pl.BlockSpec((tm, d), lambda i: (i, 0))
    full = lambda a: pl.BlockSpec(a.shape, lambda i: (0, 0))
    return pl.pallas_call(
        _ln2_kernel,
        grid=(rows // tm,),
        in_specs=[tspec, tspec, full(l2g), full(l2b)],
        out_specs=tspec,
        out_shape=jax.ShapeDtypeStruct((rows, d), F32),
        compiler_params=_cparams(("arbitrary",)),
        name="ln2",
    )(h32, ffn, l2g, l2b)


def _pick(n, prefs):
    for t in prefs:
        if n % t == 0:
            return t
    return n


def _rearranged_w_in(w_in):
    hk = H_A * DK_A
    widths = (hk, hk, W_A, W_A, H_A, H_A, W_A, W_A, W_A, H_B, w_in.shape[1] - (4 * hk + 2 * H_A + 3 * W_A + H_B))
    offs = [0]
    for wd in widths:
        offs.append(offs[-1] + wd)
    sl = lambda i: w_in[:, offs[i]:offs[i + 1]]
    gq, gk, gv, gz, ga, gb, fq, fk, fv, ff, gates = (sl(i) for i in range(11))
    pad = jnp.zeros((w_in.shape[0], LANES - 4 * H_A), w_in.dtype)
    return jnp.concatenate([gq, gk, gv, gz, fq, fk, fv, gates, ga, gb, ff, ff, pad], 1).astype(BF16)


def kernel(x_prompt, x_sample, cache_k, cache_v, cache_logf, state_gdn, state_conv, page_table, meta_tokens, ln_in_g, ln_in_b, w_in, conv_w, a_log, dt_bias, f_bias, gdn_norm_w, w_branch, w_out, ln1_g, ln1_b, w_router_group, b_router_group, w_router_expert, b_router_expert, w_gate, w_up, w_down, ln2_g, ln2_b):
    bsz, seq, d = x_prompt.shape
    n_dec = x_sample.shape[0]
    row2 = lambda a: a.reshape(1, -1).astype(F32)
    lng, lnb = row2(ln_in_g), row2(ln_in_b)
    w_big = _rearranged_w_in(w_in[0])
    convw = conv_w[0].astype(F32)
    zpad = lambda a, n: jnp.pad(a.astype(F32), (0, n - a.shape[0]))
    sp = jnp.zeros((8, LANES), F32)
    sp = sp.at[0].set(zpad(a_log[0], LANES)).at[1].set(zpad(dt_bias[0], LANES))
    sp = sp.at[2].set(zpad(jnp.concatenate([jnp.zeros((L_LOGF,), F32), f_bias[0], f_bias[0]]), LANES))

    meta = _front_call(meta_tokens.astype(F32)[None], lng, lnb, w_big, convw, jnp.zeros((1, 8, QKV), F32), sp,
                       jnp.zeros((1, LANES), F32), tm=N_META, emit_t=False)
    pad_rows = CHUNK_A - N_META
    front_pad = lambda a: jnp.pad(a, ((0, 0), (pad_rows, 0), (0, 0)))
    sm_meta = front_pad(meta["small"])
    _, s_meta = _gdn_call(front_pad(meta["qa"]), front_pad(meta["ka"]), front_pad(meta["va"]), sm_meta,
                          jnp.swapaxes(sm_meta, 1, 2)[:, :16], jnp.zeros((1, H_A, DK_A, DV_A), F32), lt=CHUNK_A)
    c_meta = meta["small"][0, :, L_C:L_C + H_B]
    c0 = jnp.zeros((1, LANES), F32).at[0, L_C:L_C + H_B].set(c_meta[-1])
    rnd = lambda a: a.astype(BF16).astype(F32)
    bias = jnp.pad(-c_meta * LOG2E, ((0, LANES - N_META), (0, 0)), constant_values=NEG_INF)
    hi = rnd(bias)
    mid = rnd(bias - hi)
    aug = jnp.zeros((LANES, H_B, DH_B), F32).at[:, :, 0].set(hi).at[:, :, 1].set(mid).at[:, :, 2].set(bias - hi - mid)
    kpad = jnp.pad(meta["fk32"][0], ((0, LANES - N_META), (0, 0))).reshape(LANES, H_B, DH_B)
    kmaug = jnp.concatenate([kpad, aug], -1).reshape(LANES, 2 * H_B * DH_B).astype(BF16)
    vmt = jnp.pad(meta["fv32"][0], ((0, LANES - N_META), (0, 0))).T.astype(BF16)

    tm = _pick(seq, (512, 256, 128, 64))
    pf = _front_call(x_prompt, lng, lnb, w_big, convw, meta["tail"], sp, c0, tm=tm, emit_t=True)
    o_a, s_prompt = _gdn_call(pf["qa"], pf["ka"], pf["va"], pf["small"], pf["small_t"], s_meta,
                              lt=_pick(seq, (256, 128, 64)))
    c_t = pf["small_t"][:, L_C:L_C + H_B].reshape(bsz, H_B, 1, seq)
    o_b = _foxt_call(pf["fqt"], pf["fkaug"], pf["fvt"], c_t, kmaug, vmt, tq=tm)

    nw = row2(gdn_norm_w[0])
    wbr = w_branch[0].astype(BF16)
    wout = w_out[0].astype(BF16)
    l1g, l1b, l2g, l2b = row2(ln1_g[0]), row2(ln1_b[0]), row2(ln2_g[0]), row2(ln2_b[0])
    wr = jnp.pad(jnp.concatenate([w_router_group[0], w_router_expert[0]], 1).astype(F32),
                 ((0, 0), (0, LANES - N_GROUPS - N_EXPERTS)))
    wr_hi = wr.astype(BF16)
    wr = jnp.concatenate([wr_hi, wr_hi, (wr - wr_hi.astype(F32)).astype(BF16)], 0)
    br = jnp.pad(jnp.concatenate([b_router_group[0], b_router_expert[0]]).astype(F32),
                 (0, LANES - N_GROUPS - N_EXPERTS)).reshape(1, LANES)
    wg, wu, wd = w_gate[0].astype(BF16), w_up[0].astype(BF16), w_down[0].astype(BF16)

    def layer_back(x2, oa2, z2, ob2, gates2, tile, moe_tile):
        h32, hb, comb = _back_call(x2, oa2, z2, ob2, gates2, lng, lnb, nw, wbr, wout, l1g, l1b, wr, br, tm=tile)
        if moe_tile % 256:
            return _moe_call(hb, h32, comb, wg, wu, wd, l2g, l2b, tm=moe_tile)
        ffn = _moe_grouped_call(hb, comb, wg, wu, wd, tm=moe_tile, ts=128)
        return _ln2_call(h32, ffn, l2g, l2b, tm=tile)

    flat = lambda a: a.reshape(bsz * seq, a.shape[-1])
    n_tok = bsz * seq
    y_prompt = layer_back(flat(x_prompt), flat(o_a), flat(pf["z"]), flat(o_b), flat(pf["gates"]),
                          _pick(n_tok, (512, 256, 128, 64)), _pick(n_tok, (1024, 512, 256, 128, 64))).reshape(bsz, seq, d)

    st = jnp.swapaxes(state_conv[0].astype(F32), 0, 1)
    sf = _sample_front_call(x_sample.reshape(n_dec, d), lng, lnb, w_big, convw, st, sp)
    o_a_s, s_sample = _gdn_step_call(sf["q"], sf["k"], sf["v"], sf["small"], state_gdn[0].astype(F32))
    n_pages = page_table.shape[1]
    o_b_s = _fox_sample_call(page_table, cache_k[0], cache_v[0], cache_logf[0], sf["fq"], sf["fk"], sf["fv"],
                             sf["small"][:, L_LOGF:L_LOGF + H_B], pps=_pick(n_pages, (8, 4, 2, 1)))
    y_sample = layer_back(x_sample.reshape(n_dec, d), o_a_s, sf["z"], o_b_s, sf["gates"], n_dec, n_dec)

    heads = lambda a: a.reshape(a.shape[:-1] + (H_B, DH_B))
    with_meta = lambda m, p: jnp.concatenate([jnp.broadcast_to(m, (bsz,) + m.shape[1:]), p], 1)
    rows = lambda a: a[0].reshape(N_META * H_B, DH_B)
    k_rows, v_rows = _put_meta_rows(pf["kout"], pf["vout"], rows(meta["fk32"]), rows(meta["fv32"]))
    k_prompt = k_rows.reshape(1, bsz, seq + N_META, H_B, DH_B)
    v_prompt = v_rows.reshape(1, bsz, seq + N_META, H_B, DH_B)
    logf_prompt = with_meta(meta["small"][:, :, L_LOGF:L_LOGF + H_B], pf["small"][:, :, L_LOGF:L_LOGF + H_B])[None]
    conv_prompt = pf["tail"][:, 8 - (CONV_W - 1):][None]
    conv_sample = jnp.concatenate([state_conv[0][:, 1:].astype(F32), sf["new"][:, None]], 1)[None]
    return (y_prompt, y_sample.reshape(n_dec, 1, d), k_prompt, v_prompt, logf_prompt, s_prompt[None], conv_prompt,
            heads(sf["fk"])[None, :, None], heads(sf["fv"])[None, :, None],
            sf["small"][:, L_LOGF:L_LOGF + H_B][None, :, None], s_sample[None], conv_sample)
```

```python
import functools

import jax
import jax.numpy as jnp
from jax import lax
from jax.experimental import pallas as pl
from jax.experimental.pallas import tpu as pltpu

F32 = jnp.float32
BF16 = jnp.bfloat16
HIGHEST = lax.Precision.HIGHEST

N_META = 16
H_A = 4
DK_A = 128
DV_A = 128
CONV_W = 4
CHUNK_A = 64
H_B = 4
DH_B = 128
N_GROUPS = 4
EXPERTS_PER_GROUP = 4
N_EXPERTS = N_GROUPS * EXPERTS_PER_GROUP
D_FF_E = 512
DEPTH = 1
DEEP_ALPHA = (2 * DEPTH) ** 0.25
LN_EPS = 1e-5
RMS_EPS = 1e-6
NEG_INF = -1e30
LOG2E = 1.4426950408889634

LANES = 128
W_A = H_A * DV_A
QKV = 3 * W_A
C_QKV = 0
C_Z = C_QKV + QKV
C_FQ = C_Z + W_A
C_FK = C_FQ + W_A
C_FV = C_FK + W_A
C_GATES = C_FV + W_A
C_SMALL = C_GATES + 2 * 1024
P_COLS = C_SMALL + LANES
L_G, L_BETA, L_LOGF, L_C = 0, 4, 8, 12
L_GRP, L_EXP = 0, 4
DV_AUG = DH_B + 16
L_GID = 32

VMEM_LIMIT = 56 * 1024 * 1024


def _cparams(sem, vmem=VMEM_LIMIT):
    return pltpu.CompilerParams(dimension_semantics=sem, vmem_limit_bytes=vmem)


def _layer_norm(x, g, b):
    mu = jnp.mean(x, -1, keepdims=True)
    xc = x - mu
    var = jnp.mean(xc * xc, -1, keepdims=True)
    return xc * lax.rsqrt(var + LN_EPS) * g + b


def _softplus(x):
    return jnp.maximum(x, 0.0) + jnp.log1p(jnp.exp(-jnp.abs(x)))


def _sigmoid(x):
    return 1.0 / (1.0 + jnp.exp(-x))


def _dot(a, b):
    return jnp.dot(a, b, preferred_element_type=F32)


def _dot_nt(a, b):
    return lax.dot_general(a, b, (((1,), (1,)), ((), ())), preferred_element_type=F32)


def _split3(x):
    hi = x.astype(BF16)
    r1 = x - hi.astype(F32)
    mid = r1.astype(BF16)
    return hi, mid, (r1 - mid.astype(F32)).astype(BF16)


def _dot_split3(a, x):
    n = x.shape[1]
    y = _dot(a, jnp.concatenate(_split3(x), 1))
    return y[:, :n] + y[:, n:2 * n] + y[:, 2 * n:]


def _split3_dot(x, b):
    m = x.shape[0]
    y = _dot(jnp.concatenate(_split3(x), 0), b)
    return y[:m] + y[m:2 * m] + y[2 * m:]


def _dot_tn(a, b):
    return lax.dot_general(a, b, (((0,), (0,)), ((), ())), preferred_element_type=F32)


def _gate_values(sm, sp, rows):
    lane = lax.broadcasted_iota(jnp.int32, (rows, LANES), 1)
    g = -jnp.exp(sp[0:1]) * _softplus(sm + sp[1:2])
    beta = _sigmoid(sm)
    logf = -_softplus(-(sm + sp[2:3]))
    return jnp.where(lane < L_BETA, g, jnp.where(lane < L_LOGF, beta, jnp.where(lane < L_C + 4, logf, 0.0)))


def _front_kernel(x_ref, lng_ref, lnb_ref, w_ref, convw_ref, halo_ref, sp_ref, c0_ref, *rest, names, tm, blk):
    o = dict(zip(names, rest))
    buf_ref, carry_ref = rest[len(names):]
    qa_ref, ka_ref, va_ref, tail_ref, small_ref = o["qa"], o["ka"], o["va"], o["tail"], o["small"]
    j = pl.program_id(1)
    nj = pl.num_programs(1)

    @pl.when(j == 0)
    def _():
        buf_ref[0:8, :] = halo_ref[0]
        carry_ref[...] = c0_ref[...]

    xn = _layer_norm(x_ref[0], lng_ref[...], lnb_ref[...]).astype(BF16)

    def proj(c0, width):
        return _dot(xn, w_ref[:, c0:c0 + width])

    buf_ref[8:8 + tm, :] = proj(C_QKV, QKV)
    outs = (qa_ref, ka_ref, va_ref)
    for grp in range(QKV // LANES):
        c = grp * LANES
        w = convw_ref[:, c:c + LANES]
        y = (buf_ref[5:5 + tm, c:c + LANES] * w[0:1] + buf_ref[6:6 + tm, c:c + LANES] * w[1:2]
             + buf_ref[7:7 + tm, c:c + LANES] * w[2:3] + buf_ref[8:8 + tm, c:c + LANES] * w[3:4])
        y = y * _sigmoid(y)
        which, head = divmod(grp, H_A)
        if which < 2:
            y = y * lax.rsqrt(jnp.sum(y * y, -1, keepdims=True) + RMS_EPS)
            if which == 0:
                y = y * DK_A ** -0.5
        outs[which][0, :, head * LANES:(head + 1) * LANES] = y.astype(BF16)

    @pl.when(j == nj - 1)
    def _():
        tail_ref[0] = buf_ref[tm:tm + 8, :]

    buf_ref[0:8, :] = buf_ref[tm:tm + 8, :]

    fk = proj(C_FK, W_A)
    fv = proj(C_FV, W_A)
    if "fk32" in o:
        o["fk32"][0] = fk
        o["fv32"][0] = fv
    if "kout" in o:
        for h in range(H_B):
            o["kout"][0, pl.ds(h, tm, stride=H_B), :] = fk[:, h * DH_B:(h + 1) * DH_B]
            o["vout"][0, pl.ds(h, tm, stride=H_B), :] = fv[:, h * DH_B:(h + 1) * DH_B]
    if "z" in o:
        o["z"][0] = proj(C_Z, W_A).astype(BF16)
        o["gates"][0] = proj(C_GATES, 2 * 1024).astype(BF16)
        o["fqt"][0] = (proj(C_FQ, W_A) * (DH_B ** -0.5 * LOG2E)).T.astype(BF16)
        fvt = fv.T
        one_row = jnp.where(lax.broadcasted_iota(jnp.int32, (DV_AUG - DH_B, tm), 0) == 0, 1.0, 0.0)
        for h in range(H_B):
            o["fvt"][0, h * DV_AUG:(h + 1) * DV_AUG, :] = jnp.concatenate(
                [fvt[h * DH_B:(h + 1) * DH_B], one_row], 0).astype(BF16)

    vals = _gate_values(proj(C_SMALL, LANES), sp_ref[...], tm)
    span = min(LANES, tm)
    r = lax.broadcasted_iota(jnp.int32, (span, span), 0)
    c = lax.broadcasted_iota(jnp.int32, (span, span), 1)
    same_blk = c >= (r // blk) * blk
    tri = jnp.where(c <= r, jnp.where(same_blk, 1.0, 0.0), 0.0).astype(BF16)
    parts = [_dot_split3(tri, vals[i * span:(i + 1) * span]) for i in range(tm // span)]
    cs = parts[0] if len(parts) == 1 else jnp.concatenate(parts, 0)
    run = carry_ref[...]
    pieces = []
    for b in range(tm // blk):
        piece = cs[b * blk:(b + 1) * blk, :] + run
        pieces.append(piece)
        run = piece[blk - 1:blk, :]
    carry_ref[...] = run
    cfull = pieces[0] if len(pieces) == 1 else jnp.concatenate(pieces, 0)
    lane = lax.broadcasted_iota(jnp.int32, (tm, LANES), 1)
    small = jnp.where(lane < L_BETA, cs, jnp.where(lane < L_C, vals, cfull))
    small_ref[0] = small
    if "small_t" in o:
        o["small_t"][0] = small.T[0:16, :]
        rnd = lambda a: a.astype(BF16).astype(F32)
        bias = (cfull[0:1, :] - cfull) * LOG2E
        hi = rnd(bias)
        mid = rnd(bias - hi)
        lo = bias - hi - mid
        for h in range(H_B):
            col = slice(L_C + h, L_C + h + 1)
            aug = jnp.where(lane == 0, hi[:, col], jnp.where(lane == 1, mid[:, col], jnp.where(lane == 2, lo[:, col], 0.0)))
            o["fkaug"][0, :, 2 * h * DH_B:(2 * h + 1) * DH_B] = fk[:, h * DH_B:(h + 1) * DH_B].astype(BF16)
            o["fkaug"][0, :, (2 * h + 1) * DH_B:(2 * h + 2) * DH_B] = aug.astype(BF16)


def _front_call(x, lng, lnb, w_big, conv_w, halo, sp, c0, *, tm, emit_t):
    bsz, seq, d = x.shape
    blk = min(CHUNK_A, tm)
    nj = seq // tm
    full2 = lambda a: pl.BlockSpec(a.shape, lambda b, j: (0, 0))
    tok = lambda width, dt: (jax.ShapeDtypeStruct((bsz, seq, width), dt),
                             pl.BlockSpec((1, tm, width), lambda b, j: (b, j, 0)))
    tok_t = lambda rows, dt: (jax.ShapeDtypeStruct((bsz, rows, seq), dt),
                              pl.BlockSpec((1, rows, tm), lambda b, j: (b, 0, j)))
    outs = dict(qa=tok(W_A, BF16), ka=tok(W_A, BF16), va=tok(W_A, BF16), small=tok(LANES, F32),
                tail=(jax.ShapeDtypeStruct((bsz, 8, QKV), F32), pl.BlockSpec((1, 8, QKV), lambda b, j: (b, 0, 0))))
    if emit_t:
        skip = N_META * H_B
        kv = (jax.ShapeDtypeStruct((bsz, skip + seq * H_B, DH_B), F32),
              pl.BlockSpec((pl.Element(1), pl.Element(tm * H_B), pl.Element(DH_B)),
                           lambda b, j: (b, pl.multiple_of(skip + j * (tm * H_B), skip), 0)))
        outs.update(kout=kv, vout=kv, z=tok(W_A, BF16), gates=tok(2 * 1024, BF16), small_t=tok_t(16, F32),
                    fqt=tok_t(W_A, BF16), fvt=tok_t(H_B * DV_AUG, BF16), fkaug=tok(2 * W_A, BF16))
    else:
        outs.update(fk32=tok(W_A, F32), fv32=tok(W_A, F32))
    names = tuple(outs)

    res = pl.pallas_call(
        functools.partial(_front_kernel, names=names, tm=tm, blk=blk),
        grid=(bsz, nj),
        in_specs=[pl.BlockSpec((1, tm, d), lambda b, j: (b, j, 0)), full2(lng), full2(lnb),
                  pl.BlockSpec(w_big.shape, lambda b, j: (0, 0), pipeline_mode=pl.Buffered(1)),
                  full2(conv_w), pl.BlockSpec((1, 8, QKV), lambda b, j: (0, 0, 0)), full2(sp), full2(c0)],
        out_specs=[outs[n][1] for n in names],
        out_shape=[outs[n][0] for n in names],
        scratch_shapes=[pltpu.VMEM((tm + 8, QKV), F32), pltpu.VMEM((1, LANES), F32)],
        compiler_params=_cparams(("arbitrary", "arbitrary")),
        name="front",
    )(x, lng, lnb, w_big, conv_w, halo, sp, c0)
    return dict(zip(names, res))


def _put_meta_rows_kernel(k_any, v_any, km_ref, vm_ref, kout_ref, vout_ref):
    del k_any, v_any
    kout_ref[0] = km_ref[...]
    vout_ref[0] = vm_ref[...]


def _put_meta_rows(kout, vout, km, vm):
    bsz = kout.shape[0]
    head = pl.BlockSpec((1,) + km.shape, lambda b: (b, 0, 0))
    small = pl.BlockSpec(km.shape, lambda b: (0, 0))
    anyspec = pl.BlockSpec(memory_space=pl.ANY)
    return pl.pallas_call(
        _put_meta_rows_kernel,
        grid=(bsz,),
        in_specs=[anyspec, anyspec, small, small],
        out_specs=[head, head],
        out_shape=[jax.ShapeDtypeStruct(kout.shape, kout.dtype), jax.ShapeDtypeStruct(vout.shape, vout.dtype)],
        input_output_aliases={0: 0, 1: 1},
        compiler_params=_cparams(("arbitrary",)),
        name="put_meta_rows",
    )(kout, vout, km, vm)


def _sample_front_kernel(x_ref, lng_ref, lnb_ref, w_ref, convw_ref, st_ref, sp_ref,
                         q_ref, k_ref, v_ref, z_ref, fq_ref, fk_ref, fv_ref, gates_ref, small_ref, new_ref):
    rows = x_ref.shape[0]
    xn = _layer_norm(x_ref[...], lng_ref[...], lnb_ref[...]).astype(BF16)

    def proj(c0, width):
        return _dot(xn, w_ref[:, c0:c0 + width])

    new = proj(C_QKV, QKV)
    new_ref[...] = new
    w = convw_ref[...]
    y = st_ref[0] * w[0:1] + st_ref[1] * w[1:2] + st_ref[2] * w[2:3] + new * w[3:4]
    y = y * _sigmoid(y)
    outs = (q_ref, k_ref, v_ref)
    for grp in range(QKV // LANES):
        which, head = divmod(grp, H_A)
        yy = y[:, grp * LANES:(grp + 1) * LANES]
        if which < 2:
            yy = yy * lax.rsqrt(jnp.sum(yy * yy, -1, keepdims=True) + RMS_EPS)
            if which == 0:
                yy = yy * DK_A ** -0.5
        outs[which][:, head * LANES:(head + 1) * LANES] = yy
    z_ref[...] = proj(C_Z, W_A)
    fq_ref[...] = proj(C_FQ, W_A) * DH_B ** -0.5
    fk_ref[...] = proj(C_FK, W_A)
    fv_ref[...] = proj(C_FV, W_A)
    gates_ref[...] = proj(C_GATES, 2 * 1024)
    small_ref[...] = _gate_values(proj(C_SMALL, LANES), sp_ref[...], rows)


def _sample_front_call(x, lng, lnb, w_big, conv_w, st, sp):
    rows = x.shape[0]
    o = lambda width: jax.ShapeDtypeStruct((rows, width), F32)
    res = pl.pallas_call(
        _sample_front_kernel,
        out_shape=[o(W_A), o(W_A), o(W_A), o(W_A), o(W_A), o(W_A), o(W_A), o(2 * 1024), o(LANES), o(QKV)],
        compiler_params=pltpu.CompilerParams(vmem_limit_bytes=VMEM_LIMIT),
        name="sample_front",
    )(x, lng, lnb, w_big, conv_w, st, sp)
    return dict(zip(["q", "k", "v", "z", "fq", "fk", "fv", "gates", "small", "new"], res))


def _gdn_kernel(q_ref, k_ref, v_ref, sm_ref, smt_ref, s0_ref, o_ref, sout_ref, state_ref, *, lt):
    j = pl.program_id(0)
    nj = pl.num_programs(0)
    ch = CHUNK_A
    bsz = q_ref.shape[0]

    @pl.when(j == 0)
    def _():
        for b in range(bsz):
            state_ref[b] = s0_ref[0]

    ri = lax.broadcasted_iota(jnp.int32, (ch, ch), 0)
    ci = lax.broadcasted_iota(jnp.int32, (ch, ch), 1)
    incl = ri >= ci
    strict = ri > ci
    units = [(b, c, h) for b in range(bsz) for c in range(lt // ch) for h in range(H_A)]
    t = {}
    for u in units:
        b, c, h = u
        rows = slice(c * ch, (c + 1) * ch)
        cols = slice(h * DK_A, (h + 1) * DK_A)
        kc = k_ref[b, rows, cols]
        qc = q_ref[b, rows, cols]
        gc_col = sm_ref[b, rows, L_G + h:L_G + h + 1]
        beta = sm_ref[b, rows, L_BETA + h:L_BETA + h + 1]
        gc_row = smt_ref[b, L_G + h:L_G + h + 1, rows]
        gc_last = sm_ref[b, c * ch + ch - 1:c * ch + ch, L_G + h:L_G + h + 1]
        eg = jnp.exp(gc_col)
        kf = kc.astype(F32)
        kb = kf * beta
        t[u] = dict(kc=kc, decay=jnp.exp(jnp.where(incl, gc_col - gc_row, NEG_INF)),
                    lhs=jnp.concatenate([kb.astype(BF16), qc], 0),
                    rhs=jnp.concatenate([v_ref[b, rows, cols].astype(F32) * beta, kb * eg], 1),
                    q_dec=(qc.astype(F32) * eg).astype(BF16),
                    k_dec=(kf * jnp.exp(gc_last - gc_col)).astype(BF16), g_tot=jnp.exp(gc_last))
    for u in units:
        d = t[u]
        a = _dot_nt(d["lhs"], d["kc"])
        d["m"] = jnp.where(strict, a[:ch] * d["decay"], 0.0)
        d["attn"] = (a[ch:] * d["decay"]).astype(BF16)
    for u in units:
        d = t[u]
        mb = d["m"].astype(BF16)
        d["x"] = -d["m"]
        d["p"] = _dot(mb, mb)
    for _ in range(4):
        for u in units:
            d = t[u]
            pb = d["p"].astype(BF16)
            r = _dot(jnp.concatenate([pb, d["x"].astype(BF16)], 0), pb)
            d["x"] = d["x"] + d["p"] + r[ch:]
            d["p"] = r[:ch]
    for u in units:
        d = t[u]
        d["x"] = d["x"] + d["p"] + _dot(d["x"].astype(BF16), d["p"].astype(BF16))
    for u in units:
        d = t[u]
        sol = d["rhs"] + _dot(d["x"].astype(BF16), d["rhs"].astype(BF16))
        d["u"] = sol[:, :DV_A]
        d["wq"] = jnp.concatenate([sol[:, DV_A:].astype(BF16), d["q_dec"]], 0)
    heads = [(b, h) for b in range(bsz) for h in range(H_A)]
    s = {bh: state_ref[bh[0], bh[1]] for bh in heads}
    for c in range(lt // ch):
        ws = {bh: _dot(t[bh[0], c, bh[1]]["wq"], s[bh].astype(BF16)) for bh in heads}
        v_new = {bh: (t[bh[0], c, bh[1]]["u"] - ws[bh][:ch]).astype(BF16) for bh in heads}
        for bh in heads:
            b, h = bh
            d = t[b, c, h]
            o = ws[bh][ch:] + _dot(d["attn"], v_new[bh])
            o_ref[b, c * ch:(c + 1) * ch, h * DV_A:(h + 1) * DV_A] = o.astype(o_ref.dtype)
        for bh in heads:
            d = t[bh[0], c, bh[1]]
            s[bh] = s[bh] * d["g_tot"] + _dot_tn(d["k_dec"], v_new[bh])
    for bh in heads:
        state_ref[bh[0], bh[1]] = s[bh]

    @pl.when(j == nj - 1)
    def _():
        sout_ref[...] = state_ref[...]


def _gdn_call(q, k, v, small, small_t, s0, *, lt):
    bsz, seq, _ = q.shape
    nj = seq // lt
    tspec = pl.BlockSpec((bsz, lt, W_A), lambda j: (0, j, 0))
    sshape = (bsz, H_A, DK_A, DV_A)
    return pl.pallas_call(
        functools.partial(_gdn_kernel, lt=lt),
        grid=(nj,),
        in_specs=[tspec, tspec, tspec, pl.BlockSpec((bsz, lt, LANES), lambda j: (0, j, 0)),
                  pl.BlockSpec((bsz, 16, lt), lambda j: (0, 0, j)),
                  pl.BlockSpec((1, H_A, DK_A, DV_A), lambda j: (0, 0, 0, 0))],
        out_specs=[tspec, pl.BlockSpec(sshape, lambda j: (0, 0, 0, 0))],
        out_shape=[jax.ShapeDtypeStruct((bsz, seq, W_A), BF16), jax.ShapeDtypeStruct(sshape, F32)],
        scratch_shapes=[pltpu.VMEM(sshape, F32)],
        compiler_params=_cparams(("arbitrary",)),
        name="gdn_chunk",
    )(q, k, v, small, small_t, s0)


def _gdn_step_kernel(q_ref, k_ref, v_ref, sm_ref, s_ref, o_ref, sout_ref):
    ri = lax.broadcasted_iota(jnp.int32, (DK_A, DK_A), 0)
    ci = lax.broadcasted_iota(jnp.int32, (DK_A, DK_A), 1)
    eye = ri == ci

    def column(row):
        return jnp.sum(jnp.where(eye, jnp.broadcast_to(row, (DK_A, DK_A)), 0.0), -1, keepdims=True)

    sm = sm_ref[0]
    for h in range(H_A):
        cols = slice(h * DK_A, (h + 1) * DK_A)
        kcol = column(k_ref[0, :, cols])
        qcol = column(q_ref[0, :, cols])
        s = s_ref[0, h] * jnp.exp(sm[:, L_G + h:L_G + h + 1])
        v_t = (v_ref[0, :, cols] - jnp.sum(kcol * s, 0, keepdims=True)) * sm[:, L_BETA + h:L_BETA + h + 1]
        s = s + kcol * v_t
        sout_ref[0, h] = s
        o_ref[0, :, cols] = jnp.sum(qcol * s, 0, keepdims=True)


def _gdn_step_call(q, k, v, small, state):
    n = q.shape[0]
    row = lambda a: a.reshape(n, 1, a.shape[-1])
    rspec = lambda width: pl.BlockSpec((1, 1, width), lambda b: (b, 0, 0))
    sspec = pl.BlockSpec((1, H_A, DK_A, DV_A), lambda b: (b, 0, 0, 0))
    o, s = pl.pallas_call(
        _gdn_step_kernel,
        grid=(n,),
        in_specs=[rspec(W_A), rspec(W_A), rspec(W_A), rspec(LANES), sspec],
        out_specs=[rspec(W_A), sspec],
        out_shape=[jax.ShapeDtypeStruct((n, 1, W_A), F32), jax.ShapeDtypeStruct(state.shape, F32)],
        compiler_params=_cparams(("arbitrary",)),
        name="gdn_step",
    )(row(q), row(k), row(v), row(small), state)
    return o.reshape(n, W_A), s


def _foxt_kernel(qt_ref, ka_ref, vt_ref, ct_ref, kma_ref, vmt_ref, o_ref, m_sc, acc_sc, *, tq, sub):
    qi = pl.program_id(1)
    qstart = pl.multiple_of(qi * tq, tq)
    heads = range(H_B)
    hs = lambda h: slice(h * DH_B, (h + 1) * DH_B)
    ha = lambda h: slice(2 * h * DH_B, (2 * h + 2) * DH_B)
    hv = lambda h: slice(h * DV_AUG, (h + 1) * DV_AUG)
    ones = jnp.where(lax.broadcasted_iota(jnp.int32, (DH_B, tq), 0) < 3, 1.0, 0.0).astype(BF16)
    qa = [jnp.concatenate([qt_ref[0, hs(h), :], ones], 0) for h in heads]
    c_first = [ct_ref[0, h, :, pl.ds(qstart, LANES)][:, 0:1] for h in heads]

    def step(keys, vt, offs, masks, first):
        parts = range(len(keys[0]))
        s = [[_dot(keys[h][i], qa[h]) for i in parts] for h in heads]
        p, alpha = [], []
        for h in heads:
            sh = [s[h][i] if masks is None else jnp.where(masks[i], s[h][i], NEG_INF) for i in parts]
            blk_max = jnp.max(sh[0], 0, keepdims=True)
            for i in parts[1:]:
                blk_max = jnp.maximum(blk_max, jnp.max(sh[i], 0, keepdims=True))
            blk_max = blk_max + offs[h]
            m_new = blk_max if first else jnp.maximum(m_sc[h], blk_max)
            shift = m_new - offs[h]
            p.append([jnp.exp2(sh[i] - shift).astype(BF16) for i in parts])
            if not first:
                alpha.append(jnp.exp2(m_sc[h] - m_new))
            m_sc[h] = m_new
        for h in heads:
            pv = _dot(vt[h][0], p[h][0])
            for i in parts[1:]:
                pv = pv + _dot(vt[h][i], p[h][i])
            acc_sc[h] = pv if first else alpha[h] * acc_sc[h] + pv

    step([[kma_ref[:, ha(h)]] for h in heads], [[vmt_ref[hv(h), :]] for h in heads],
         [c_first[h] * LOG2E for h in heads], None, True)

    def block(start, masked):
        offs = [(c_first[h] - ct_ref[0, h, :, pl.ds(start, LANES)][:, 0:1]) * LOG2E for h in heads]
        parts = range(tq // sub)
        masks = None
        if masked:
            masks = [lax.broadcasted_iota(jnp.int32, (sub, tq), 0) + i * sub
                     <= lax.broadcasted_iota(jnp.int32, (sub, tq), 1) for i in parts]
        step([[ka_ref[0, pl.ds(start + i * sub, sub), ha(h)] for i in parts] for h in heads],
             [[vt_ref[0, hv(h), pl.ds(start + i * sub, sub)] for i in parts] for h in heads], offs, masks, False)

    def body(jb, carry):
        block(pl.multiple_of(jb * tq, tq), False)
        return carry

    lax.fori_loop(0, qi, body, 0)
    block(qstart, True)
    for h in heads:
        acc = acc_sc[h]
        o_ref[0, :, hs(h)] = (acc[:DH_B] / acc[DH_B:DH_B + 1]).T.astype(o_ref.dtype)


def _foxt_call(fqt, fkaug, fvt, c_t, kmaug, vmt, *, tq):
    bsz, hd, seq = fqt.shape
    hdv = fvt.shape[1]
    once = dict(pipeline_mode=pl.Buffered(1))
    return pl.pallas_call(
        functools.partial(_foxt_kernel, tq=tq, sub=min(256, tq)),
        grid=(bsz, seq // tq),
        in_specs=[pl.BlockSpec((1, hd, tq), lambda b, i: (b, 0, i)),
                  pl.BlockSpec((1, seq, 2 * hd), lambda b, i: (b, 0, 0), **once),
                  pl.BlockSpec((1, hdv, seq), lambda b, i: (b, 0, 0), **once),
                  pl.BlockSpec((1, H_B, 1, seq), lambda b, i: (b, 0, 0, 0)),
                  pl.BlockSpec(kmaug.shape, lambda b, i: (0, 0)), pl.BlockSpec(vmt.shape, lambda b, i: (0, 0))],
        out_specs=pl.BlockSpec((1, tq, hd), lambda b, i: (b, i, 0)),
        out_shape=jax.ShapeDtypeStruct((bsz, seq, hd), BF16),
        scratch_shapes=[pltpu.VMEM((H_B, 1, tq), F32), pltpu.VMEM((H_B, DV_AUG, tq), F32)],
        compiler_params=_cparams(("arbitrary", "arbitrary")),
        name="fox_prompt",
    )(fqt, fkaug, fvt, c_t, kmaug, vmt)


def _fox_sample_kernel(pt_ref, *refs, pps):
    k_refs = refs[:pps]
    v_refs = refs[pps:2 * pps]
    lf_refs = refs[2 * pps:3 * pps]
    q_ref, kn_ref, vn_ref, lfn_ref, wsuf_ref, o_ref, m_sc, l_sc, run_sc, acc_sc = refs[3 * pps:]
    j = pl.program_id(1)
    nj = pl.num_programs(1)
    rows = k_refs[0].shape[1]
    q8 = q_ref[0]

    @pl.when(j == 0)
    def _():
        run_sc[...] = lfn_ref[0]
        m_sc[...] = jnp.broadcast_to(jnp.sum(q8 * kn_ref[0], -1, keepdims=True), (8, LANES))
        l_sc[...] = jnp.ones((8, LANES), F32)
        acc_sc[...] = vn_ref[0]

    qb = q8.astype(BF16)
    lf = jnp.concatenate([lf_refs[i][0] for i in range(pps)], 0)
    within = _split3_dot(lf, wsuf_ref[...])
    tot = jnp.broadcast_to(jnp.sum(lf, -1, keepdims=True), (8 * pps, LANES))
    run = run_sc[...]
    valid = (lax.broadcasted_iota(jnp.int32, (8, rows), 1) % H_B) == lax.broadcasted_iota(jnp.int32, (8, rows), 0)
    scores = []
    for i in range(pps):
        sl = slice(8 * i, 8 * i + 8)
        bias = within[sl] + run[:, 0:1]
        scores.append(jnp.where(valid, _dot_nt(qb, k_refs[i][0].astype(BF16)) + bias, NEG_INF))
        run = run + tot[sl]
    run_sc[...] = run
    s = jnp.concatenate(scores, 1)
    m_prev = m_sc[...][:, 0:1]
    m_new = jnp.maximum(m_prev, jnp.max(s, -1, keepdims=True))
    alpha = jnp.exp(m_prev - m_new)
    p = jnp.exp(s - m_new)
    l_sc[...] = jnp.broadcast_to(alpha * l_sc[...][:, 0:1] + jnp.sum(p, -1, keepdims=True), (8, LANES))
    m_sc[...] = jnp.broadcast_to(m_new, (8, LANES))
    acc = alpha * acc_sc[...]
    for i in range(pps):
        acc = acc + _dot(p[:, i * rows:(i + 1) * rows].astype(BF16), v_refs[i][0].astype(BF16))
    acc_sc[...] = acc

    @pl.when(j == nj - 1)
    def _():
        o_ref[0] = acc / l_sc[...][:, 0:1]


def _fox_sample_call(page_table, cache_k, cache_v, cache_logf, fq, fk, fv, logf_new, *, pps):
    n, n_pages = page_table.shape
    n_pool, page = cache_k.shape[0], cache_k.shape[1]
    rows = page * H_B
    ck = cache_k.reshape(n_pool, rows, DH_B)
    cv = cache_v.reshape(n_pool, rows, DH_B)
    clf = jnp.pad(jnp.swapaxes(cache_logf, 1, 2), ((0, 0), (0, 8 - H_B), (0, 0)))
    tile8 = lambda a: jnp.pad(a.reshape(n, H_B, DH_B), ((0, 0), (0, 8 - H_B), (0, 0)))
    lfn = jnp.broadcast_to(jnp.pad(logf_new, ((0, 0), (0, 8 - H_B)))[:, :, None], (n, 8, LANES))
    wsuf = (lax.broadcasted_iota(jnp.int32, (page, rows), 0)
            > lax.broadcasted_iota(jnp.int32, (page, rows), 1) // H_B).astype(BF16)

    def page_map(i):
        return lambda b, j, pt: (pt[b * n_pages + n_pages - 1 - (j * pps + i)], 0, 0)

    kv_specs = [pl.BlockSpec((1, rows, DH_B), page_map(i)) for i in range(pps)]
    lf_specs = [pl.BlockSpec((1, 8, page), page_map(i)) for i in range(pps)]
    tspec = pl.BlockSpec((1, 8, LANES), lambda b, j, pt: (b, 0, 0))
    full = lambda a: pl.BlockSpec(a.shape, lambda b, j, pt: (0, 0))
    grid_spec = pltpu.PrefetchScalarGridSpec(
        num_scalar_prefetch=1,
        grid=(n, n_pages // pps),
        in_specs=kv_specs + kv_specs + lf_specs + [tspec, tspec, tspec, tspec, full(wsuf)],
        out_specs=tspec,
        scratch_shapes=[pltpu.VMEM((8, LANES), F32)] * 4,
    )
    o = pl.pallas_call(
        functools.partial(_fox_sample_kernel, pps=pps),
        grid_spec=grid_spec,
        out_shape=jax.ShapeDtypeStruct((n, 8, DH_B), F32),
        compiler_params=_cparams(("arbitrary", "arbitrary")),
        name="fox_sample",
    )(page_table.reshape(-1), *([ck] * pps), *([cv] * pps), *([clf] * pps), tile8(fq), tile8(fk), tile8(fv), lfn,
      wsuf)
    return o[:, :H_B].reshape(n, H_B * DH_B)


def _back_kernel(x_ref, oa_ref, z_ref, ob_ref, gates_ref, lng_ref, lnb_ref, nw_ref, wbr_ref, wout_ref,
                 l1g_ref, l1b_ref, wr_ref, br_ref, h32_ref, hb_ref, comb_ref):
    tm = x_ref.shape[0]
    xn = _layer_norm(x_ref[...], lng_ref[...], lnb_ref[...])
    nw = nw_ref[...]
    parts = []
    for h in range(H_A):
        cols = slice(h * DV_A, (h + 1) * DV_A)
        o = oa_ref[:, cols].astype(F32)
        o = o * lax.rsqrt(jnp.mean(o * o, -1, keepdims=True) + RMS_EPS) * nw
        zz = z_ref[:, cols].astype(F32)
        parts.append((o * (zz * _sigmoid(zz))).astype(BF16))
    oa = jnp.concatenate(parts, 1)
    d = wout_ref.shape[0]
    merged = (_sigmoid(gates_ref[:, 0:d].astype(F32)) * _dot(oa, wbr_ref[0])
              + _sigmoid(gates_ref[:, d:2 * d].astype(F32)) * _dot(ob_ref[...].astype(BF16), wbr_ref[1]))
    mix = _dot(merged.astype(BF16), wout_ref[...])
    hh = _layer_norm(DEEP_ALPHA * xn + mix, l1g_ref[...], l1b_ref[...])
    h32_ref[...] = hh
    hb_ref[...] = hh.astype(BF16)

    h_hi = hh.astype(BF16)
    h_mid = (hh - h_hi.astype(F32)).astype(BF16)
    logits = _dot(jnp.concatenate([h_hi, h_mid, h_hi], 1), wr_ref[...]) + br_ref[...]
    lane = lax.broadcasted_iota(jnp.int32, (tm, LANES), 1)
    big = jnp.int32(LANES)
    is_grp = lane < L_EXP
    gl = jnp.where(is_grp, logits, NEG_INF)
    gmax = jnp.max(gl, -1, keepdims=True)
    gsel = jnp.min(jnp.where(jnp.logical_and(is_grp, gl == gmax), lane, big), -1, keepdims=True)
    gw = 1.0 / jnp.sum(jnp.exp(gl - gmax), -1, keepdims=True)
    lo = L_EXP + gsel * EXPERTS_PER_GROUP
    in_grp = jnp.logical_and(lane >= lo, lane < lo + EXPERTS_PER_GROUP)
    el = jnp.where(in_grp, logits, NEG_INF)
    m1 = jnp.max(el, -1, keepdims=True)
    i1 = jnp.min(jnp.where(el == m1, lane, big), -1, keepdims=True)
    el2 = jnp.where(lane == i1, NEG_INF, el)
    m2 = jnp.max(el2, -1, keepdims=True)
    i2 = jnp.min(jnp.where(el2 == m2, lane, big), -1, keepdims=True)
    e2 = jnp.exp(m2 - m1)
    w1 = gw / (1.0 + e2)
    w2 = gw * e2 / (1.0 + e2)
    comb_ref[...] = jnp.where(lane == i1, w1, jnp.where(lane == i2, w2,
                                                        jnp.where(lane == L_GID, gsel.astype(F32), 0.0)))


def _back_call(x, oa, z, ob, gates, lng, lnb, nw, wbr, wout, l1g, l1b, wr, br, *, tm):
    rows, d = x.shape
    tspec = lambda width: pl.BlockSpec((tm, width), lambda i: (i, 0))
    full = lambda a: pl.BlockSpec(a.shape, lambda i: (0,) * a.ndim)
    return pl.pallas_call(
        _back_kernel,
        grid=(rows // tm,),
        in_specs=[tspec(d), tspec(W_A), tspec(W_A), tspec(W_A), tspec(2 * d), full(lng), full(lnb), full(nw),
                  full(wbr), full(wout), full(l1g), full(l1b), full(wr), full(br)],
        out_specs=[tspec(d), tspec(d), tspec(LANES)],
        out_shape=[jax.ShapeDtypeStruct((rows, d), F32), jax.ShapeDtypeStruct((rows, d), BF16),
                   jax.ShapeDtypeStruct((rows, LANES), F32)],
        compiler_params=_cparams(("arbitrary",)),
        name="back",
    )(x, oa, z, ob, gates, lng, lnb, nw, wbr, wout, l1g, l1b, wr, br)


def _moe_kernel(hb_ref, h32_ref, comb_ref, wg_ref, wu_ref, wd_ref, l2g_ref, l2b_ref, y_ref, acc_ref):
    e = pl.program_id(1)
    ne = pl.num_programs(1)

    @pl.when(e == 0)
    def _():
        acc_ref[...] = jnp.zeros_like(acc_ref)

    x = hb_ref[...]
    g = _dot(x, wg_ref[0])
    hid = (g * _sigmoid(g)) * _dot(x, wu_ref[0])
    lane = lax.broadcasted_iota(jnp.int32, comb_ref.shape, 1)
    wgt = jnp.sum(jnp.where(lane == L_EXP + e, comb_ref[...], 0.0), -1, keepdims=True)
    acc_ref[...] += wgt * _dot(hid.astype(BF16), wd_ref[0])

    @pl.when(e == ne - 1)
    def _():
        y_ref[...] = _layer_norm(DEEP_ALPHA * h32_ref[...] + acc_ref[...], l2g_ref[...], l2b_ref[...])


def _moe_call(hb, h32, comb, wg, wu, wd, l2g, l2b, *, tm):
    rows, d = h32.shape
    ne, _, dff = wg.shape
    tspec = lambda width: pl.BlockSpec((tm, width), lambda i, e: (i, 0))
    full = lambda a: pl.BlockSpec(a.shape, lambda i, e: (0,) * a.ndim)
    return pl.pallas_call(
        _moe_kernel,
        grid=(rows // tm, ne),
        in_specs=[tspec(d), tspec(d), tspec(LANES),
                  pl.BlockSpec((1, d, dff), lambda i, e: (e, 0, 0)), pl.BlockSpec((1, d, dff), lambda i, e: (e, 0, 0)),
                  pl.BlockSpec((1, dff, d), lambda i, e: (e, 0, 0)), full(l2g), full(l2b)],
        out_specs=tspec(d),
        out_shape=jax.ShapeDtypeStruct((rows, d), F32),
        scratch_shapes=[pltpu.VMEM((tm, d), F32)],
        compiler_params=_cparams(("arbitrary", "arbitrary")),
        name="moe",
    )(hb, h32, comb, wg, wu, wd, l2g, l2b)


def _moe_grouped_kernel(hb_ref, comb_ref, wg_ref, wu_ref, wd_ref, o_ref, xs_ref, cs_ref, acc_ref, pt_ref, rng_ref,
                        *, ts):
    g = pl.program_id(1)
    t = hb_ref.shape[0]

    @pl.when(g == 0)
    def _():
        comb = comb_ref[...]
        lane = lax.broadcasted_iota(jnp.int32, (t, LANES), 1)
        gid = comb[:, L_GID:L_GID + 1].astype(jnp.int32)
        onehot = jnp.where(lane == gid, 1.0, 0.0)
        r = lax.broadcasted_iota(jnp.int32, (t, t), 0)
        c = lax.broadcasted_iota(jnp.int32, (t, t), 1)
        csum = _dot(jnp.where(c <= r, 1.0, 0.0).astype(BF16), onehot.astype(BF16))
        counts = csum[t - 1:t, :]
        before = (lax.broadcasted_iota(jnp.int32, (LANES, LANES), 0)
                  < lax.broadcasted_iota(jnp.int32, (LANES, LANES), 1)).astype(BF16)
        starts = _split3_dot(jnp.broadcast_to(counts, (8, LANES)), before)[0:1]
        pos = (jnp.sum(onehot * (starts + csum), -1, keepdims=True) - 1.0).astype(jnp.int32)
        pos_row = jnp.broadcast_to(pos.astype(F32), (t, LANES)).T[0:1, :].astype(jnp.int32)
        perm = jnp.where(r == pos_row, 1.0, 0.0).astype(BF16)
        xs_ref[...] = _dot(perm, hb_ref[...]).astype(BF16)
        cs_ref[...] = _dot_split3(perm, comb)
        pt_ref[...] = jnp.where(c == pos, 1.0, 0.0).astype(BF16)
        acc_ref[...] = jnp.zeros_like(acc_ref)
        for gg in range(N_GROUPS):
            rng_ref[gg] = starts[0, gg].astype(jnp.int32)
            rng_ref[N_GROUPS + gg] = counts[0, gg].astype(jnp.int32)

    start = rng_ref[g]
    stop = start + rng_ref[N_GROUPS + g]
    lane = lax.broadcasted_iota(jnp.int32, (ts, LANES), 1)

    def sub_tile(j, carry):
        r0 = pl.multiple_of(j * ts, ts)
        x = xs_ref[pl.ds(r0, ts), :]
        cw = cs_ref[pl.ds(r0, ts), :]
        out = None
        for e in range(EXPERTS_PER_GROUP):
            gate = _dot(x, wg_ref[0, e])
            hid = (gate * _sigmoid(gate)) * _dot(x, wu_ref[0, e])
            wgt = jnp.sum(jnp.where(lane == L_EXP + g * EXPERTS_PER_GROUP + e, cw, 0.0), -1, keepdims=True)
            term = wgt * _dot(hid.astype(BF16), wd_ref[0, e])
            out = term if out is None else out + term
        acc_ref[pl.ds(r0, ts), :] += out
        return carry

    lax.fori_loop(start // ts, (stop + ts - 1) // ts, sub_tile, 0)

    @pl.when(g == N_GROUPS - 1)
    def _():
        o_ref[...] = _dot(pt_ref[...], acc_ref[...].astype(BF16)).astype(o_ref.dtype)


def _moe_grouped_call(hb, comb, wg, wu, wd, *, tm, ts):
    rows, d = hb.shape
    dff = wg.shape[-1]
    grp = lambda a: a.reshape((N_GROUPS, EXPERTS_PER_GROUP) + a.shape[1:])
    tspec = lambda width: pl.BlockSpec((tm, width), lambda i, g: (i, 0))
    wspec = lambda a, b: pl.BlockSpec((1, EXPERTS_PER_GROUP, a, b), lambda i, g: (g, 0, 0, 0))
    return pl.pallas_call(
        functools.partial(_moe_grouped_kernel, ts=ts),
        grid=(rows // tm, N_GROUPS),
        in_specs=[tspec(d), tspec(LANES), wspec(d, dff), wspec(d, dff), wspec(dff, d)],
        out_specs=tspec(d),
        out_shape=jax.ShapeDtypeStruct((rows, d), BF16),
        scratch_shapes=[pltpu.VMEM((tm, d), BF16), pltpu.VMEM((tm, LANES), F32), pltpu.VMEM((tm, d), F32),
                        pltpu.VMEM((tm, tm), BF16), pltpu.SMEM((2 * N_GROUPS,), jnp.int32)],
        compiler_params=_cparams(("arbitrary", "arbitrary")),
        name="moe_grouped",
    )(hb, comb, grp(wg), grp(wu), grp(wd))


def _ln2_kernel(h_ref, f_ref, g_ref, b_ref, y_ref):
    y_ref[...] = _layer_norm(DEEP_ALPHA * h_ref[...] + f_ref[...].astype(F32), g_ref[...], b_ref[...])


def _ln2_call(h32, ffn, l2g, l2b, *, tm):
    rows, d = h32.shape
    tspec = pl.BlockSpec((tm, d), lambda i: (i, 0))
    full = lambda a: pl.BlockSpec(a.shape, lambda i: (0, 0))
    return pl.pallas_call(
        _ln2_kernel,
        grid=(rows // tm,),
        in_specs=[tspec, tspec, full(l2g), full(l2b)],
        out_specs=tspec,
        out_shape=jax.ShapeDtypeStruct((rows, d), F32),
        compiler_params=_cparams(("arbitrary",)),
        name="ln2",
    )(h32, ffn, l2g, l2b)


def _pick(n, prefs):
    for t in prefs:
        if n % t == 0:
            return t
    return n


def _prep_w_kernel(w_ref, o_ref):
    big = 4 * W_A
    src_f = big + 2 * H_A
    src_gates = src_f + 3 * W_A + H_B
    o_ref[:, C_QKV:C_QKV + big] = w_ref[:, 0:big].astype(BF16)
    o_ref[:, C_FQ:C_FQ + 3 * W_A] = w_ref[:, src_f:src_f + 3 * W_A].astype(BF16)
    o_ref[:, C_GATES:C_GATES + 2 * 1024] = w_ref[:, src_gates:src_gates + 2 * 1024].astype(BF16)
    ab = w_ref[:, big:big + LANES]
    f_at = (src_f + 3 * W_A) % LANES
    ftile = w_ref[:, src_f + 3 * W_A - f_at:src_f + 3 * W_A - f_at + LANES]
    shifted = lambda to: ftile if (to - f_at) % LANES == 0 else pltpu.roll(ftile, (to - f_at) % LANES, 1)
    f1, f2 = shifted(L_LOGF), shifted(L_C)
    lane = lax.broadcasted_iota(jnp.int32, ab.shape, 1)
    small = jnp.where(lane < L_LOGF, ab, jnp.where(lane < L_C, f1, jnp.where(lane < L_C + H_B, f2, 0.0)))
    o_ref[:, C_SMALL:C_SMALL + LANES] = small.astype(BF16)


def _rearranged_w_in(w_in):
    d = w_in.shape[0]
    return pl.pallas_call(
        _prep_w_kernel,
        out_shape=jax.ShapeDtypeStruct((d, P_COLS), BF16),
        compiler_params=pltpu.CompilerParams(vmem_limit_bytes=VMEM_LIMIT),
        name="prep_w",
    )(w_in)


def kernel(x_prompt, x_sample, cache_k, cache_v, cache_logf, state_gdn, state_conv, page_table, meta_tokens, ln_in_g, ln_in_b, w_in, conv_w, a_log, dt_bias, f_bias, gdn_norm_w, w_branch, w_out, ln1_g, ln1_b, w_router_group, b_router_group, w_router_expert, b_router_expert, w_gate, w_up, w_down, ln2_g, ln2_b):
    bsz, seq, d = x_prompt.shape
    n_dec = x_sample.shape[0]
    row2 = lambda a: a.reshape(1, -1).astype(F32)
    lng, lnb = row2(ln_in_g), row2(ln_in_b)
    w_big = _rearranged_w_in(w_in[0])
    convw = conv_w[0].astype(F32)
    zpad = lambda a, n: jnp.pad(a.astype(F32), (0, n - a.shape[0]))
    sp = jnp.zeros((8, LANES), F32)
    sp = sp.at[0].set(zpad(a_log[0], LANES)).at[1].set(zpad(dt_bias[0], LANES))
    sp = sp.at[2].set(zpad(jnp.concatenate([jnp.zeros((L_LOGF,), F32), f_bias[0], f_bias[0]]), LANES))

    meta = _front_call(meta_tokens.astype(F32)[None], lng, lnb, w_big, convw, jnp.zeros((1, 8, QKV), F32), sp,
                       jnp.zeros((1, LANES), F32), tm=N_META, emit_t=False)
    pad_rows = CHUNK_A - N_META
    front_pad = lambda a: jnp.pad(a, ((0, 0), (pad_rows, 0), (0, 0)))
    sm_meta = front_pad(meta["small"])
    _, s_meta = _gdn_call(front_pad(meta["qa"]), front_pad(meta["ka"]), front_pad(meta["va"]), sm_meta,
                          jnp.swapaxes(sm_meta, 1, 2)[:, :16], jnp.zeros((1, H_A, DK_A, DV_A), F32), lt=CHUNK_A)
    c_meta = meta["small"][0, :, L_C:L_C + H_B]
    c0 = jnp.zeros((1, LANES), F32).at[0, L_C:L_C + H_B].set(c_meta[-1])
    rnd = lambda a: a.astype(BF16).astype(F32)
    bias = jnp.pad(-c_meta * LOG2E, ((0, LANES - N_META), (0, 0)), constant_values=NEG_INF)
    hi = rnd(bias)
    mid = rnd(bias - hi)
    aug = jnp.zeros((LANES, H_B, DH_B), F32).at[:, :, 0].set(hi).at[:, :, 1].set(mid).at[:, :, 2].set(bias - hi - mid)
    kpad = jnp.pad(meta["fk32"][0], ((0, LANES - N_META), (0, 0))).reshape(LANES, H_B, DH_B)
    kmaug = jnp.concatenate([kpad, aug], -1).reshape(LANES, 2 * H_B * DH_B).astype(BF16)
    vpad = jnp.pad(meta["fv32"][0], ((0, LANES - N_META), (0, 0))).reshape(LANES, H_B, DH_B)
    vones = jnp.zeros((LANES, H_B, DV_AUG - DH_B), F32).at[:, :, 0].set(1.0)
    vmt = jnp.concatenate([vpad, vones], -1).reshape(LANES, H_B * DV_AUG).T.astype(BF16)

    tm = _pick(seq, (512, 256, 128, 64))
    pf = _front_call(x_prompt, lng, lnb, w_big, convw, meta["tail"], sp, c0, tm=tm, emit_t=True)
    o_a, s_prompt = _gdn_call(pf["qa"], pf["ka"], pf["va"], pf["small"], pf["small_t"], s_meta,
                              lt=_pick(seq, (256, 128, 64)))
    c_t = pf["small_t"][:, L_C:L_C + H_B].reshape(bsz, H_B, 1, seq)
    o_b = _foxt_call(pf["fqt"], pf["fkaug"], pf["fvt"], c_t, kmaug, vmt, tq=tm)

    nw = row2(gdn_norm_w[0])
    wbr = w_branch[0].astype(BF16)
    wout = w_out[0].astype(BF16)
    l1g, l1b, l2g, l2b = row2(ln1_g[0]), row2(ln1_b[0]), row2(ln2_g[0]), row2(ln2_b[0])
    wr = jnp.pad(jnp.concatenate([w_router_group[0], w_router_expert[0]], 1).astype(F32),
                 ((0, 0), (0, LANES - N_GROUPS - N_EXPERTS)))
    wr_hi = wr.astype(BF16)
    wr = jnp.concatenate([wr_hi, wr_hi, (wr - wr_hi.astype(F32)).astype(BF16)], 0)
    br = jnp.pad(jnp.concatenate([b_router_group[0], b_router_expert[0]]).astype(F32),
                 (0, LANES - N_GROUPS - N_EXPERTS)).reshape(1, LANES)
    wg, wu, wd = w_gate[0].astype(BF16), w_up[0].astype(BF16), w_down[0].astype(BF16)

    def layer_back(x2, oa2, z2, ob2, gates2, tile, moe_tile):
        h32, hb, comb = _back_call(x2, oa2, z2, ob2, gates2, lng, lnb, nw, wbr, wout, l1g, l1b, wr, br, tm=tile)
        if moe_tile % 256:
            return _moe_call(hb, h32, comb, wg, wu, wd, l2g, l2b, tm=moe_tile)
        ffn = _moe_grouped_call(hb, comb, wg, wu, wd, tm=moe_tile, ts=128)
        return _ln2_call(h32, ffn, l2g, l2b, tm=tile)

    flat = lambda a: a.reshape(bsz * seq, a.shape[-1])
    n_tok = bsz * seq
    y_prompt = layer_back(flat(x_prompt), flat(o_a), flat(pf["z"]), flat(o_b), flat(pf["gates"]),
                          _pick(n_tok, (512, 256, 128, 64)), _pick(n_tok, (1024, 512, 256, 128, 64))).reshape(bsz, seq, d)

    st = jnp.swapaxes(state_conv[0].astype(F32), 0, 1)
    sf = _sample_front_call(x_sample.reshape(n_dec, d), lng, lnb, w_big, convw, st, sp)
    o_a_s, s_sample = _gdn_step_call(sf["q"], sf["k"], sf["v"], sf["small"], state_gdn[0].astype(F32))
    n_pages = page_table.shape[1]
    o_b_s = _fox_sample_call(page_table, cache_k[0], cache_v[0], cache_logf[0], sf["fq"], sf["fk"], sf["fv"],
                             sf["small"][:, L_LOGF:L_LOGF + H_B], pps=_pick(n_pages, (8, 4, 2, 1)))
    y_sample = layer_back(x_sample.reshape(n_dec, d), o_a_s, sf["z"], o_b_s, sf["gates"], n_dec, n_dec)

    heads = lambda a: a.reshape(a.shape[:-1] + (H_B, DH_B))
    with_meta = lambda m, p: jnp.concatenate([jnp.broadcast_to(m, (bsz,) + m.shape[1:]), p], 1)
    rows = lambda a: a[0].reshape(N_META * H_B, DH_B)
    k_rows, v_rows = _put_meta_rows(pf["kout"], pf["vout"], rows(meta["fk32"]), rows(meta["fv32"]))
    k_prompt = k_rows.reshape(1, bsz, seq + N_META, H_B, DH_B)
    v_prompt = v_rows.reshape(1, bsz, seq + N_META, H_B, DH_B)
    lf_meta = jnp.swapaxes(meta["small"][:, :, L_LOGF:L_LOGF + H_B], 1, 2)
    lf_t = jnp.concatenate([jnp.broadcast_to(lf_meta, (bsz, H_B, N_META)), pf["small_t"][:, L_LOGF:L_LOGF + H_B]], 2)
    logf_prompt = jnp.swapaxes(lf_t, 1, 2)[None]
    conv_prompt = pf["tail"][:, 8 - (CONV_W - 1):][None]
    conv_sample = jnp.concatenate([state_conv[0][:, 1:].astype(F32), sf["new"][:, None]], 1)[None]
    return (y_prompt, y_sample.reshape(n_dec, 1, d), k_prompt, v_prompt, logf_prompt, s_prompt[None], conv_prompt,
            heads(sf["fk"])[None, :, None], heads(sf["fv"])[None, :, None],
            sf["small"][:, L_LOGF:L_LOGF + H_B][None, :, None], s_sample[None], conv_sample)
```

```python
import functools

import jax
import jax.numpy as jnp
from jax import lax
from jax.experimental import pallas as pl
from jax.experimental.pallas import tpu as pltpu

F32 = jnp.float32
BF16 = jnp.bfloat16
HIGHEST = lax.Precision.HIGHEST

N_META = 16
H_A = 4
DK_A = 128
DV_A = 128
CONV_W = 4
CHUNK_A = 64
H_B = 4
DH_B = 128
N_GROUPS = 4
EXPERTS_PER_GROUP = 4
N_EXPERTS = N_GROUPS * EXPERTS_PER_GROUP
D_FF_E = 512
DEPTH = 1
DEEP_ALPHA = (2 * DEPTH) ** 0.25
LN_EPS = 1e-5
RMS_EPS = 1e-6
NEG_INF = -1e30
LOG2E = 1.4426950408889634

LANES = 128
W_A = H_A * DV_A
QKV = 3 * W_A
C_QKV = 0
C_Z = C_QKV + QKV
C_FQ = C_Z + W_A
C_FK = C_FQ + W_A
C_FV = C_FK + W_A
C_GATES = C_FV + W_A
C_SMALL = C_GATES + 2 * 1024
P_COLS = C_SMALL + LANES
L_G, L_BETA, L_LOGF, L_C = 0, 4, 8, 12
L_GRP, L_EXP = 0, 4
DV_AUG = DH_B + 16
L_GID = 32

VMEM_LIMIT = 56 * 1024 * 1024


def _cparams(sem, vmem=VMEM_LIMIT):
    return pltpu.CompilerParams(dimension_semantics=sem, vmem_limit_bytes=vmem)


def _layer_norm(x, g, b):
    mu = jnp.mean(x, -1, keepdims=True)
    xc = x - mu
    var = jnp.mean(xc * xc, -1, keepdims=True)
    return xc * lax.rsqrt(var + LN_EPS) * g + b


def _softplus(x):
    return jnp.maximum(x, 0.0) + jnp.log1p(jnp.exp(-jnp.abs(x)))


def _sigmoid(x):
    return 1.0 / (1.0 + jnp.exp(-x))


def _dot(a, b):
    return jnp.dot(a, b, preferred_element_type=F32)


def _dot_nt(a, b):
    return lax.dot_general(a, b, (((1,), (1,)), ((), ())), preferred_element_type=F32)


def _split3(x):
    hi = x.astype(BF16)
    r1 = x - hi.astype(F32)
    mid = r1.astype(BF16)
    return hi, mid, (r1 - mid.astype(F32)).astype(BF16)


def _dot_split3(a, x):
    n = x.shape[1]
    y = _dot(a, jnp.concatenate(_split3(x), 1))
    return y[:, :n] + y[:, n:2 * n] + y[:, 2 * n:]


def _split3_dot(x, b):
    m = x.shape[0]
    y = _dot(jnp.concatenate(_split3(x), 0), b)
    return y[:m] + y[m:2 * m] + y[2 * m:]


def _dot_tn(a, b):
    return lax.dot_general(a, b, (((0,), (0,)), ((), ())), preferred_element_type=F32)


def _gate_values(sm, sp, rows):
    lane = lax.broadcasted_iota(jnp.int32, (rows, LANES), 1)
    g = -jnp.exp(sp[0:1]) * _softplus(sm + sp[1:2])
    beta = _sigmoid(sm)
    logf = -_softplus(-(sm + sp[2:3]))
    return jnp.where(lane < L_BETA, g, jnp.where(lane < L_LOGF, beta, jnp.where(lane < L_C + 4, logf, 0.0)))


def _front_kernel(x_ref, lng_ref, lnb_ref, w_ref, convw_ref, halo_ref, sp_ref, c0_ref, *rest, names, tm, blk):
    o = dict(zip(names, rest))
    buf_ref, carry_ref = rest[len(names):]
    qa_ref, ka_ref, va_ref, tail_ref, small_ref = o["qa"], o["ka"], o["va"], o["tail"], o["small"]
    j = pl.program_id(1)
    nj = pl.num_programs(1)

    @pl.when(j == 0)
    def _():
        buf_ref[0:8, :] = halo_ref[0]
        carry_ref[...] = c0_ref[...]

    xn = _layer_norm(x_ref[0], lng_ref[...], lnb_ref[...]).astype(BF16)

    def proj(c0, width):
        return _dot(xn, w_ref[:, c0:c0 + width])

    buf_ref[8:8 + tm, :] = proj(C_QKV, QKV)
    outs = (qa_ref, ka_ref, va_ref)
    for grp in range(QKV // LANES):
        c = grp * LANES
        w = convw_ref[:, c:c + LANES]
        y = (buf_ref[5:5 + tm, c:c + LANES] * w[0:1] + buf_ref[6:6 + tm, c:c + LANES] * w[1:2]
             + buf_ref[7:7 + tm, c:c + LANES] * w[2:3] + buf_ref[8:8 + tm, c:c + LANES] * w[3:4])
        y = y * _sigmoid(y)
        which, head = divmod(grp, H_A)
        if which < 2:
            y = y * lax.rsqrt(jnp.sum(y * y, -1, keepdims=True) + RMS_EPS)
            if which == 0:
                y = y * DK_A ** -0.5
        outs[which][0, :, head * LANES:(head + 1) * LANES] = y.astype(BF16)

    @pl.when(j == nj - 1)
    def _():
        tail_ref[0] = buf_ref[tm:tm + 8, :]

    buf_ref[0:8, :] = buf_ref[tm:tm + 8, :]

    fk = proj(C_FK, W_A)
    fv = proj(C_FV, W_A)
    if "fk32" in o:
        o["fk32"][0] = fk
        o["fv32"][0] = fv
    if "kout" in o:
        for h in range(H_B):
            o["kout"][0, pl.ds(h, tm, stride=H_B), :] = fk[:, h * DH_B:(h + 1) * DH_B]
            o["vout"][0, pl.ds(h, tm, stride=H_B), :] = fv[:, h * DH_B:(h + 1) * DH_B]
    if "z" in o:
        o["z"][0] = proj(C_Z, W_A).astype(BF16)
        o["gates"][0] = proj(C_GATES, 2 * 1024).astype(BF16)
        o["fqt"][0] = (proj(C_FQ, W_A) * (DH_B ** -0.5 * LOG2E)).T.astype(BF16)
        fvt = fv.T
        one_row = jnp.where(lax.broadcasted_iota(jnp.int32, (DV_AUG - DH_B, tm), 0) == 0, 1.0, 0.0)
        for h in range(H_B):
            o["fvt"][0, h * DV_AUG:(h + 1) * DV_AUG, :] = jnp.concatenate(
                [fvt[h * DH_B:(h + 1) * DH_B], one_row], 0).astype(BF16)

    vals = _gate_values(proj(C_SMALL, LANES), sp_ref[...], tm)
    span = min(LANES, tm)
    r = lax.broadcasted_iota(jnp.int32, (span, span), 0)
    c = lax.broadcasted_iota(jnp.int32, (span, span), 1)
    same_blk = c >= (r // blk) * blk
    tri = jnp.where(c <= r, jnp.where(same_blk, 1.0, 0.0), 0.0).astype(BF16)
    parts = [_dot_split3(tri, vals[i * span:(i + 1) * span]) for i in range(tm // span)]
    cs = parts[0] if len(parts) == 1 else jnp.concatenate(parts, 0)
    run = carry_ref[...]
    pieces = []
    for b in range(tm // blk):
        piece = cs[b * blk:(b + 1) * blk, :] + run
        pieces.append(piece)
        run = piece[blk - 1:blk, :]
    carry_ref[...] = run
    cfull = pieces[0] if len(pieces) == 1 else jnp.concatenate(pieces, 0)
    lane = lax.broadcasted_iota(jnp.int32, (tm, LANES), 1)
    small = jnp.where(lane < L_BETA, cs, jnp.where(lane < L_C, vals, cfull))
    small_ref[0] = small
    if "small_t" in o:
        o["small_t"][0] = small.T[0:16, :]
        rnd = lambda a: a.astype(BF16).astype(F32)
        bias = (cfull[0:1, :] - cfull) * LOG2E
        hi = rnd(bias)
        mid = rnd(bias - hi)
        lo = bias - hi - mid
        for h in range(H_B):
            col = slice(L_C + h, L_C + h + 1)
            aug = jnp.where(lane == 0, hi[:, col], jnp.where(lane == 1, mid[:, col], jnp.where(lane == 2, lo[:, col], 0.0)))
            o["fkaug"][0, :, 2 * h * DH_B:(2 * h + 1) * DH_B] = fk[:, h * DH_B:(h + 1) * DH_B].astype(BF16)
            o["fkaug"][0, :, (2 * h + 1) * DH_B:(2 * h + 2) * DH_B] = aug.astype(BF16)


def _front_call(x, lng, lnb, w_big, conv_w, halo, sp, c0, *, tm, emit_t):
    bsz, seq, d = x.shape
    blk = min(CHUNK_A, tm)
    nj = seq // tm
    full2 = lambda a: pl.BlockSpec(a.shape, lambda b, j: (0, 0))
    tok = lambda width, dt: (jax.ShapeDtypeStruct((bsz, seq, width), dt),
                             pl.BlockSpec((1, tm, width), lambda b, j: (b, j, 0)))
    tok_t = lambda rows, dt: (jax.ShapeDtypeStruct((bsz, rows, seq), dt),
                              pl.BlockSpec((1, rows, tm), lambda b, j: (b, 0, j)))
    outs = dict(qa=tok(W_A, BF16), ka=tok(W_A, BF16), va=tok(W_A, BF16), small=tok(LANES, F32),
                tail=(jax.ShapeDtypeStruct((bsz, 8, QKV), F32), pl.BlockSpec((1, 8, QKV), lambda b, j: (b, 0, 0))))
    if emit_t:
        skip = N_META * H_B
        kv = (jax.ShapeDtypeStruct((bsz, skip + seq * H_B, DH_B), F32),
              pl.BlockSpec((pl.Element(1), pl.Element(tm * H_B), pl.Element(DH_B)),
                           lambda b, j: (b, pl.multiple_of(skip + j * (tm * H_B), skip), 0)))
        outs.update(kout=kv, vout=kv, z=tok(W_A, BF16), gates=tok(2 * 1024, BF16), small_t=tok_t(16, F32),
                    fqt=tok_t(W_A, BF16), fvt=tok_t(H_B * DV_AUG, BF16), fkaug=tok(2 * W_A, BF16))
    else:
        outs.update(fk32=tok(W_A, F32), fv32=tok(W_A, F32))
    names = tuple(outs)

    res = pl.pallas_call(
        functools.partial(_front_kernel, names=names, tm=tm, blk=blk),
        grid=(bsz, nj),
        in_specs=[pl.BlockSpec((1, tm, d), lambda b, j: (b, j, 0)), full2(lng), full2(lnb),
                  pl.BlockSpec(w_big.shape, lambda b, j: (0, 0), pipeline_mode=pl.Buffered(1)),
                  full2(conv_w), pl.BlockSpec((1, 8, QKV), lambda b, j: (0, 0, 0)), full2(sp), full2(c0)],
        out_specs=[outs[n][1] for n in names],
        out_shape=[outs[n][0] for n in names],
        scratch_shapes=[pltpu.VMEM((tm + 8, QKV), F32), pltpu.VMEM((1, LANES), F32)],
        compiler_params=_cparams(("arbitrary", "arbitrary")),
        name="front",
    )(x, lng, lnb, w_big, conv_w, halo, sp, c0)
    return dict(zip(names, res))


def _put_meta_rows_kernel(k_any, v_any, km_ref, vm_ref, kout_ref, vout_ref):
    del k_any, v_any
    kout_ref[0] = km_ref[...]
    vout_ref[0] = vm_ref[...]


def _put_meta_rows(kout, vout, km, vm):
    bsz = kout.shape[0]
    head = pl.BlockSpec((1,) + km.shape, lambda b: (b, 0, 0))
    small = pl.BlockSpec(km.shape, lambda b: (0, 0))
    anyspec = pl.BlockSpec(memory_space=pl.ANY)
    return pl.pallas_call(
        _put_meta_rows_kernel,
        grid=(bsz,),
        in_specs=[anyspec, anyspec, small, small],
        out_specs=[head, head],
        out_shape=[jax.ShapeDtypeStruct(kout.shape, kout.dtype), jax.ShapeDtypeStruct(vout.shape, vout.dtype)],
        input_output_aliases={0: 0, 1: 1},
        compiler_params=_cparams(("arbitrary",)),
        name="put_meta_rows",
    )(kout, vout, km, vm)


def _sample_front_kernel(x_ref, lng_ref, lnb_ref, w_ref, convw_ref, st_ref, sp_ref,
                         q_ref, k_ref, v_ref, z_ref, fq_ref, fk_ref, fv_ref, gates_ref, small_ref, new_ref):
    rows = x_ref.shape[0]
    xn = _layer_norm(x_ref[...], lng_ref[...], lnb_ref[...]).astype(BF16)

    def proj(c0, width):
        return _dot(xn, w_ref[:, c0:c0 + width])

    new = proj(C_QKV, QKV)
    new_ref[...] = new
    w = convw_ref[...]
    y = st_ref[0] * w[0:1] + st_ref[1] * w[1:2] + st_ref[2] * w[2:3] + new * w[3:4]
    y = y * _sigmoid(y)
    outs = (q_ref, k_ref, v_ref)
    for grp in range(QKV // LANES):
        which, head = divmod(grp, H_A)
        yy = y[:, grp * LANES:(grp + 1) * LANES]
        if which < 2:
            yy = yy * lax.rsqrt(jnp.sum(yy * yy, -1, keepdims=True) + RMS_EPS)
            if which == 0:
                yy = yy * DK_A ** -0.5
        outs[which][:, head * LANES:(head + 1) * LANES] = yy
    z_ref[...] = proj(C_Z, W_A)
    fq_ref[...] = proj(C_FQ, W_A) * DH_B ** -0.5
    fk_ref[...] = proj(C_FK, W_A)
    fv_ref[...] = proj(C_FV, W_A)
    gates_ref[...] = proj(C_GATES, 2 * 1024)
    small_ref[...] = _gate_values(proj(C_SMALL, LANES), sp_ref[...], rows)


def _sample_front_call(x, lng, lnb, w_big, conv_w, st, sp):
    rows = x.shape[0]
    o = lambda width: jax.ShapeDtypeStruct((rows, width), F32)
    res = pl.pallas_call(
        _sample_front_kernel,
        out_shape=[o(W_A), o(W_A), o(W_A), o(W_A), o(W_A), o(W_A), o(W_A), o(2 * 1024), o(LANES), o(QKV)],
        compiler_params=pltpu.CompilerParams(vmem_limit_bytes=VMEM_LIMIT),
        name="sample_front",
    )(x, lng, lnb, w_big, conv_w, st, sp)
    return dict(zip(["q", "k", "v", "z", "fq", "fk", "fv", "gates", "small", "new"], res))


def _gdn_kernel(q_ref, k_ref, v_ref, sm_ref, smt_ref, s0_ref, o_ref, sout_ref, state_ref, *, lt):
    j = pl.program_id(0)
    nj = pl.num_programs(0)
    ch = CHUNK_A
    bsz = q_ref.shape[0]

    @pl.when(j == 0)
    def _():
        for b in range(bsz):
            state_ref[b] = s0_ref[0]

    ri = lax.broadcasted_iota(jnp.int32, (ch, ch), 0)
    ci = lax.broadcasted_iota(jnp.int32, (ch, ch), 1)
    incl = ri >= ci
    strict = ri > ci
    units = [(b, c, h) for b in range(bsz) for c in range(lt // ch) for h in range(H_A)]
    t = {}
    for u in units:
        b, c, h = u
        rows = slice(c * ch, (c + 1) * ch)
        cols = slice(h * DK_A, (h + 1) * DK_A)
        kc = k_ref[b, rows, cols]
        qc = q_ref[b, rows, cols]
        gc_col = sm_ref[b, rows, L_G + h:L_G + h + 1]
        beta = sm_ref[b, rows, L_BETA + h:L_BETA + h + 1]
        gc_row = smt_ref[b, L_G + h:L_G + h + 1, rows]
        gc_last = sm_ref[b, c * ch + ch - 1:c * ch + ch, L_G + h:L_G + h + 1]
        eg = jnp.exp(gc_col)
        kf = kc.astype(F32)
        kb = kf * beta
        t[u] = dict(kc=kc, decay=jnp.exp(jnp.where(incl, gc_col - gc_row, NEG_INF)),
                    lhs=jnp.concatenate([kb.astype(BF16), qc], 0),
                    rhs=jnp.concatenate([v_ref[b, rows, cols].astype(F32) * beta, kb * eg], 1),
                    q_dec=(qc.astype(F32) * eg).astype(BF16),
                    k_dec=(kf * jnp.exp(gc_last - gc_col)).astype(BF16), g_tot=jnp.exp(gc_last))
    for u in units:
        d = t[u]
        a = _dot_nt(d["lhs"], d["kc"])
        d["m"] = jnp.where(strict, a[:ch] * d["decay"], 0.0)
        d["attn"] = (a[ch:] * d["decay"]).astype(BF16)
    for u in units:
        d = t[u]
        mb = d["m"].astype(BF16)
        d["x"] = -d["m"]
        d["p"] = _dot(mb, mb)
    for _ in range(4):
        for u in units:
            d = t[u]
            pb = d["p"].astype(BF16)
            r = _dot(jnp.concatenate([pb, d["x"].astype(BF16)], 0), pb)
            d["x"] = d["x"] + d["p"] + r[ch:]
            d["p"] = r[:ch]
    for u in units:
        d = t[u]
        d["x"] = d["x"] + d["p"] + _dot(d["x"].astype(BF16), d["p"].astype(BF16))
    for u in units:
        d = t[u]
        sol = d["rhs"] + _dot(d["x"].astype(BF16), d["rhs"].astype(BF16))
        d["u"] = sol[:, :DV_A]
        d["wq"] = jnp.concatenate([sol[:, DV_A:].astype(BF16), d["q_dec"]], 0)
    heads = [(b, h) for b in range(bsz) for h in range(H_A)]
    s = {bh: state_ref[bh[0], bh[1]] for bh in heads}
    for c in range(lt // ch):
        ws = {bh: _dot(t[bh[0], c, bh[1]]["wq"], s[bh].astype(BF16)) for bh in heads}
        v_new = {bh: (t[bh[0], c, bh[1]]["u"] - ws[bh][:ch]).astype(BF16) for bh in heads}
        for bh in heads:
            b, h = bh
            d = t[b, c, h]
            o = ws[bh][ch:] + _dot(d["attn"], v_new[bh])
            o_ref[b, c * ch:(c + 1) * ch, h * DV_A:(h + 1) * DV_A] = o.astype(o_ref.dtype)
        for bh in heads:
            d = t[bh[0], c, bh[1]]
            s[bh] = s[bh] * d["g_tot"] + _dot_tn(d["k_dec"], v_new[bh])
    for bh in heads:
        state_ref[bh[0], bh[1]] = s[bh]

    @pl.when(j == nj - 1)
    def _():
        sout_ref[...] = state_ref[...]


def _gdn_call(q, k, v, small, small_t, s0, *, lt):
    bsz, seq, _ = q.shape
    nj = seq // lt
    tspec = pl.BlockSpec((bsz, lt, W_A), lambda j: (0, j, 0))
    sshape = (bsz, H_A, DK_A, DV_A)
    return pl.pallas_call(
        functools.partial(_gdn_kernel, lt=lt),
        grid=(nj,),
        in_specs=[tspec, tspec, tspec, pl.BlockSpec((bsz, lt, LANES), lambda j: (0, j, 0)),
                  pl.BlockSpec((bsz, 16, lt), lambda j: (0, 0, j)),
                  pl.BlockSpec((1, H_A, DK_A, DV_A), lambda j: (0, 0, 0, 0))],
        out_specs=[tspec, pl.BlockSpec(sshape, lambda j: (0, 0, 0, 0))],
        out_shape=[jax.ShapeDtypeStruct((bsz, seq, W_A), BF16), jax.ShapeDtypeStruct(sshape, F32)],
        scratch_shapes=[pltpu.VMEM(sshape, F32)],
        compiler_params=_cparams(("arbitrary",)),
        name="gdn_chunk",
    )(q, k, v, small, small_t, s0)


def _gdn_step_kernel(q_ref, k_ref, v_ref, sm_ref, s_ref, o_ref, sout_ref):
    ri = lax.broadcasted_iota(jnp.int32, (DK_A, DK_A), 0)
    ci = lax.broadcasted_iota(jnp.int32, (DK_A, DK_A), 1)
    eye = ri == ci

    def column(row):
        return jnp.sum(jnp.where(eye, jnp.broadcast_to(row, (DK_A, DK_A)), 0.0), -1, keepdims=True)

    sm = sm_ref[0]
    for h in range(H_A):
        cols = slice(h * DK_A, (h + 1) * DK_A)
        kcol = column(k_ref[0, :, cols])
        qcol = column(q_ref[0, :, cols])
        s = s_ref[0, h] * jnp.exp(sm[:, L_G + h:L_G + h + 1])
        v_t = (v_ref[0, :, cols] - jnp.sum(kcol * s, 0, keepdims=True)) * sm[:, L_BETA + h:L_BETA + h + 1]
        s = s + kcol * v_t
        sout_ref[0, h] = s
        o_ref[0, :, cols] = jnp.sum(qcol * s, 0, keepdims=True)


def _gdn_step_call(q, k, v, small, state):
    n = q.shape[0]
    row = lambda a: a.reshape(n, 1, a.shape[-1])
    rspec = lambda width: pl.BlockSpec((1, 1, width), lambda b: (b, 0, 0))
    sspec = pl.BlockSpec((1, H_A, DK_A, DV_A), lambda b: (b, 0, 0, 0))
    o, s = pl.pallas_call(
        _gdn_step_kernel,
        grid=(n,),
        in_specs=[rspec(W_A), rspec(W_A), rspec(W_A), rspec(LANES), sspec],
        out_specs=[rspec(W_A), sspec],
        out_shape=[jax.ShapeDtypeStruct((n, 1, W_A), F32), jax.ShapeDtypeStruct(state.shape, F32)],
        compiler_params=_cparams(("arbitrary",)),
        name="gdn_step",
    )(row(q), row(k), row(v), row(small), state)
    return o.reshape(n, W_A), s


def _foxt_kernel(qt_ref, ka_ref, vt_ref, ct_ref, kma_ref, vmt_ref, o_ref, m_sc, acc_sc, *, tq, sub):
    qi = pl.program_id(1)
    qstart = pl.multiple_of(qi * tq, tq)
    heads = range(H_B)
    hs = lambda h: slice(h * DH_B, (h + 1) * DH_B)
    ha = lambda h: slice(2 * h * DH_B, (2 * h + 2) * DH_B)
    hv = lambda h: slice(h * DV_AUG, (h + 1) * DV_AUG)
    ones = jnp.where(lax.broadcasted_iota(jnp.int32, (DH_B, tq), 0) < 3, 1.0, 0.0).astype(BF16)
    qa = [jnp.concatenate([qt_ref[0, hs(h), :], ones], 0) for h in heads]
    c_first = [ct_ref[0, h, :, pl.ds(qstart, LANES)][:, 0:1] for h in heads]

    def step(keys, vt, offs, masks, first):
        parts = range(len(keys[0]))
        s = [[_dot(keys[h][i], qa[h]) for i in parts] for h in heads]
        p, alpha = [], []
        for h in heads:
            sh = [s[h][i] if masks is None else jnp.where(masks[i], s[h][i], NEG_INF) for i in parts]
            blk_max = jnp.max(sh[0], 0, keepdims=True)
            for i in parts[1:]:
                blk_max = jnp.maximum(blk_max, jnp.max(sh[i], 0, keepdims=True))
            blk_max = blk_max + offs[h]
            m_new = blk_max if first else jnp.maximum(m_sc[h], blk_max)
            shift = m_new - offs[h]
            p.append([jnp.exp2(sh[i] - shift).astype(BF16) for i in parts])
            if not first:
                alpha.append(jnp.exp2(m_sc[h] - m_new))
            m_sc[h] = m_new
        for h in heads:
            pv = _dot(vt[h][0], p[h][0])
            for i in parts[1:]:
                pv = pv + _dot(vt[h][i], p[h][i])
            acc_sc[h] = pv if first else alpha[h] * acc_sc[h] + pv

    step([[kma_ref[:, ha(h)]] for h in heads], [[vmt_ref[hv(h), :]] for h in heads],
         [c_first[h] * LOG2E for h in heads], None, True)

    def block(start, masked):
        offs = [(c_first[h] - ct_ref[0, h, :, pl.ds(start, LANES)][:, 0:1]) * LOG2E for h in heads]
        parts = range(tq // sub)
        masks = None
        if masked:
            masks = [lax.broadcasted_iota(jnp.int32, (sub, tq), 0) + i * sub
                     <= lax.broadcasted_iota(jnp.int32, (sub, tq), 1) for i in parts]
        step([[ka_ref[0, pl.ds(start + i * sub, sub), ha(h)] for i in parts] for h in heads],
             [[vt_ref[0, hv(h), pl.ds(start + i * sub, sub)] for i in parts] for h in heads], offs, masks, False)

    def body(jb, carry):
        block(pl.multiple_of(jb * tq, tq), False)
        return carry

    lax.fori_loop(0, qi, body, 0)
    block(qstart, True)
    for h in heads:
        acc = acc_sc[h]
        o_ref[0, :, hs(h)] = (acc[:DH_B] / acc[DH_B:DH_B + 1]).T.astype(o_ref.dtype)


def _foxt_call(fqt, fkaug, fvt, c_t, kmaug, vmt, *, tq):
    bsz, hd, seq = fqt.shape
    hdv = fvt.shape[1]
    once = dict(pipeline_mode=pl.Buffered(1))
    return pl.pallas_call(
        functools.partial(_foxt_kernel, tq=tq, sub=min(256, tq)),
        grid=(bsz, seq // tq),
        in_specs=[pl.BlockSpec((1, hd, tq), lambda b, i: (b, 0, i)),
                  pl.BlockSpec((1, seq, 2 * hd), lambda b, i: (b, 0, 0), **once),
                  pl.BlockSpec((1, hdv, seq), lambda b, i: (b, 0, 0), **once),
                  pl.BlockSpec((1, H_B, 1, seq), lambda b, i: (b, 0, 0, 0)),
                  pl.BlockSpec(kmaug.shape, lambda b, i: (0, 0)), pl.BlockSpec(vmt.shape, lambda b, i: (0, 0))],
        out_specs=pl.BlockSpec((1, tq, hd), lambda b, i: (b, i, 0)),
        out_shape=jax.ShapeDtypeStruct((bsz, seq, hd), BF16),
        scratch_shapes=[pltpu.VMEM((H_B, 1, tq), F32), pltpu.VMEM((H_B, DV_AUG, tq), F32)],
        compiler_params=_cparams(("arbitrary", "arbitrary")),
        name="fox_prompt",
    )(fqt, fkaug, fvt, c_t, kmaug, vmt)


def _fox_sample_kernel(pt_ref, *refs, pps):
    k_refs = refs[:pps]
    v_refs = refs[pps:2 * pps]
    lf_refs = refs[2 * pps:3 * pps]
    q_ref, kn_ref, vn_ref, lfn_ref, wsuf_ref, o_ref, m_sc, l_sc, run_sc, acc_sc = refs[3 * pps:]
    j = pl.program_id(1)
    nj = pl.num_programs(1)
    rows = k_refs[0].shape[1]
    q8 = q_ref[0]

    @pl.when(j == 0)
    def _():
        run_sc[...] = lfn_ref[0]
        m_sc[...] = jnp.broadcast_to(jnp.sum(q8 * kn_ref[0], -1, keepdims=True), (8, LANES))
        l_sc[...] = jnp.ones((8, LANES), F32)
        acc_sc[...] = vn_ref[0]

    qb = q8.astype(BF16)
    lf = jnp.concatenate([lf_refs[i][0] for i in range(pps)], 0)
    within = _split3_dot(lf, wsuf_ref[...])
    tot = jnp.broadcast_to(jnp.sum(lf, -1, keepdims=True), (8 * pps, LANES))
    run = run_sc[...]
    valid = (lax.broadcasted_iota(jnp.int32, (8, rows), 1) % H_B) == lax.broadcasted_iota(jnp.int32, (8, rows), 0)
    scores = []
    for i in range(pps):
        sl = slice(8 * i, 8 * i + 8)
        bias = within[sl] + run[:, 0:1]
        scores.append(jnp.where(valid, _dot_nt(qb, k_refs[i][0].astype(BF16)) + bias, NEG_INF))
        run = run + tot[sl]
    run_sc[...] = run
    s = jnp.concatenate(scores, 1)
    m_prev = m_sc[...][:, 0:1]
    m_new = jnp.maximum(m_prev, jnp.max(s, -1, keepdims=True))
    alpha = jnp.exp(m_prev - m_new)
    p = jnp.exp(s - m_new)
    l_sc[...] = jnp.broadcast_to(alpha * l_sc[...][:, 0:1] + jnp.sum(p, -1, keepdims=True), (8, LANES))
    m_sc[...] = jnp.broadcast_to(m_new, (8, LANES))
    acc = alpha * acc_sc[...]
    for i in range(pps):
        acc = acc + _dot(p[:, i * rows:(i + 1) * rows].astype(BF16), v_refs[i][0].astype(BF16))
    acc_sc[...] = acc

    @pl.when(j == nj - 1)
    def _():
        o_ref[0] = acc / l_sc[...][:, 0:1]


def _fox_sample_call(page_table, cache_k, cache_v, cache_logf, fq, fk, fv, logf_new, *, pps):
    n, n_pages = page_table.shape
    n_pool, page = cache_k.shape[0], cache_k.shape[1]
    rows = page * H_B
    ck = cache_k.reshape(n_pool, rows, DH_B)
    cv = cache_v.reshape(n_pool, rows, DH_B)
    clf = jnp.pad(jnp.swapaxes(cache_logf, 1, 2), ((0, 0), (0, 8 - H_B), (0, 0)))
    tile8 = lambda a: jnp.pad(a.reshape(n, H_B, DH_B), ((0, 0), (0, 8 - H_B), (0, 0)))
    lfn = jnp.broadcast_to(jnp.pad(logf_new, ((0, 0), (0, 8 - H_B)))[:, :, None], (n, 8, LANES))
    wsuf = (lax.broadcasted_iota(jnp.int32, (page, rows), 0)
            > lax.broadcasted_iota(jnp.int32, (page, rows), 1) // H_B).astype(BF16)

    def page_map(i):
        return lambda b, j, pt: (pt[b * n_pages + n_pages - 1 - (j * pps + i)], 0, 0)

    kv_specs = [pl.BlockSpec((1, rows, DH_B), page_map(i)) for i in range(pps)]
    lf_specs = [pl.BlockSpec((1, 8, page), page_map(i)) for i in range(pps)]
    tspec = pl.BlockSpec((1, 8, LANES), lambda b, j, pt: (b, 0, 0))
    full = lambda a: pl.BlockSpec(a.shape, lambda b, j, pt: (0, 0))
    grid_spec = pltpu.PrefetchScalarGridSpec(
        num_scalar_prefetch=1,
        grid=(n, n_pages // pps),
        in_specs=kv_specs + kv_specs + lf_specs + [tspec, tspec, tspec, tspec, full(wsuf)],
        out_specs=tspec,
        scratch_shapes=[pltpu.VMEM((8, LANES), F32)] * 4,
    )
    o = pl.pallas_call(
        functools.partial(_fox_sample_kernel, pps=pps),
        grid_spec=grid_spec,
        out_shape=jax.ShapeDtypeStruct((n, 8, DH_B), F32),
        compiler_params=_cparams(("arbitrary", "arbitrary")),
        name="fox_sample",
    )(page_table.reshape(-1), *([ck] * pps), *([cv] * pps), *([clf] * pps), tile8(fq), tile8(fk), tile8(fv), lfn,
      wsuf)
    return o[:, :H_B].reshape(n, H_B * DH_B)


def _back_kernel(x_ref, oa_ref, z_ref, ob_ref, gates_ref, lng_ref, lnb_ref, nw_ref, wbr_ref, wout_ref,
                 l1g_ref, l1b_ref, wr_ref, br_ref, h32_ref, hb_ref, comb_ref):
    tm = x_ref.shape[0]
    xn = _layer_norm(x_ref[...], lng_ref[...], lnb_ref[...])
    nw = nw_ref[...]
    parts = []
    for h in range(H_A):
        cols = slice(h * DV_A, (h + 1) * DV_A)
        o = oa_ref[:, cols].astype(F32)
        o = o * lax.rsqrt(jnp.mean(o * o, -1, keepdims=True) + RMS_EPS) * nw
        zz = z_ref[:, cols].astype(F32)
        parts.append((o * (zz * _sigmoid(zz))).astype(BF16))
    oa = jnp.concatenate(parts, 1)
    d = wout_ref.shape[0]
    merged = (_sigmoid(gates_ref[:, 0:d].astype(F32)) * _dot(oa, wbr_ref[0])
              + _sigmoid(gates_ref[:, d:2 * d].astype(F32)) * _dot(ob_ref[...].astype(BF16), wbr_ref[1]))
    mix = _dot(merged.astype(BF16), wout_ref[...])
    hh = _layer_norm(DEEP_ALPHA * xn + mix, l1g_ref[...], l1b_ref[...])
    h32_ref[...] = hh
    hb_ref[...] = hh.astype(BF16)

    h_hi = hh.astype(BF16)
    h_mid = (hh - h_hi.astype(F32)).astype(BF16)
    logits = _dot(jnp.concatenate([h_hi, h_mid, h_hi], 1), wr_ref[...]) + br_ref[...]
    lane = lax.broadcasted_iota(jnp.int32, (tm, LANES), 1)
    big = jnp.int32(LANES)
    is_grp = lane < L_EXP
    gl = jnp.where(is_grp, logits, NEG_INF)
    gmax = jnp.max(gl, -1, keepdims=True)
    gsel = jnp.min(jnp.where(jnp.logical_and(is_grp, gl == gmax), lane, big), -1, keepdims=True)
    gw = 1.0 / jnp.sum(jnp.exp(gl - gmax), -1, keepdims=True)
    lo = L_EXP + gsel * EXPERTS_PER_GROUP
    in_grp = jnp.logical_and(lane >= lo, lane < lo + EXPERTS_PER_GROUP)
    el = jnp.where(in_grp, logits, NEG_INF)
    m1 = jnp.max(el, -1, keepdims=True)
    i1 = jnp.min(jnp.where(el == m1, lane, big), -1, keepdims=True)
    el2 = jnp.where(lane == i1, NEG_INF, el)
    m2 = jnp.max(el2, -1, keepdims=True)
    i2 = jnp.min(jnp.where(el2 == m2, lane, big), -1, keepdims=True)
    e2 = jnp.exp(m2 - m1)
    w1 = gw / (1.0 + e2)
    w2 = gw * e2 / (1.0 + e2)
    comb_ref[...] = jnp.where(lane == i1, w1, jnp.where(lane == i2, w2,
                                                        jnp.where(lane == L_GID, gsel.astype(F32), 0.0)))


def _back_call(x, oa, z, ob, gates, lng, lnb, nw, wbr, wout, l1g, l1b, wr, br, *, tm):
    rows, d = x.shape
    tspec = lambda width: pl.BlockSpec((tm, width), lambda i: (i, 0))
    full = lambda a: pl.BlockSpec(a.shape, lambda i: (0,) * a.ndim)
    return pl.pallas_call(
        _back_kernel,
        grid=(rows // tm,),
        in_specs=[tspec(d), tspec(W_A), tspec(W_A), tspec(W_A), tspec(2 * d), full(lng), full(lnb), full(nw),
                  full(wbr), full(wout), full(l1g), full(l1b), full(wr), full(br)],
        out_specs=[tspec(d), tspec(d), tspec(LANES)],
        out_shape=[jax.ShapeDtypeStruct((rows, d), F32), jax.ShapeDtypeStruct((rows, d), BF16),
                   jax.ShapeDtypeStruct((rows, LANES), F32)],
        compiler_params=_cparams(("arbitrary",)),
        name="back",
    )(x, oa, z, ob, gates, lng, lnb, nw, wbr, wout, l1g, l1b, wr, br)


def _moe_kernel(hb_ref, h32_ref, comb_ref, wg_ref, wu_ref, wd_ref, l2g_ref, l2b_ref, y_ref, acc_ref):
    e = pl.program_id(1)
    ne = pl.num_programs(1)

    @pl.when(e == 0)
    def _():
        acc_ref[...] = jnp.zeros_like(acc_ref)

    x = hb_ref[...]
    g = _dot(x, wg_ref[0])
    hid = (g * _sigmoid(g)) * _dot(x, wu_ref[0])
    lane = lax.broadcasted_iota(jnp.int32, comb_ref.shape, 1)
    wgt = jnp.sum(jnp.where(lane == L_EXP + e, comb_ref[...], 0.0), -1, keepdims=True)
    acc_ref[...] += wgt * _dot(hid.astype(BF16), wd_ref[0])

    @pl.when(e == ne - 1)
    def _():
        y_ref[...] = _layer_norm(DEEP_ALPHA * h32_ref[...] + acc_ref[...], l2g_ref[...], l2b_ref[...])


def _moe_call(hb, h32, comb, wg, wu, wd, l2g, l2b, *, tm):
    rows, d = h32.shape
    ne, _, dff = wg.shape
    tspec = lambda width: pl.BlockSpec((tm, width), lambda i, e: (i, 0))
    full = lambda a: pl.BlockSpec(a.shape, lambda i, e: (0,) * a.ndim)
    return pl.pallas_call(
        _moe_kernel,
        grid=(rows // tm, ne),
        in_specs=[tspec(d), tspec(d), tspec(LANES),
                  pl.BlockSpec((1, d, dff), lambda i, e: (e, 0, 0)), pl.BlockSpec((1, d, dff), lambda i, e: (e, 0, 0)),
                  pl.BlockSpec((1, dff, d), lambda i, e: (e, 0, 0)), full(l2g), full(l2b)],
        out_specs=tspec(d),
        out_shape=jax.ShapeDtypeStruct((rows, d), F32),
        scratch_shapes=[pltpu.VMEM((tm, d), F32)],
        compiler_params=_cparams(("arbitrary", "arbitrary")),
        name="moe",
    )(hb, h32, comb, wg, wu, wd, l2g, l2b)


def _moe_grouped_kernel(hb_ref, comb_ref, h32_ref, wg_ref, wu_ref, wd_ref, l2g_ref, l2b_ref, o_ref,
                        xs_ref, cs_ref, acc_ref, pt_ref, rng_ref, *, ts):
    g = pl.program_id(1)
    t = hb_ref.shape[0]

    @pl.when(g == 0)
    def _():
        comb = comb_ref[...]
        lane = lax.broadcasted_iota(jnp.int32, (t, LANES), 1)
        gid = comb[:, L_GID:L_GID + 1].astype(jnp.int32)
        onehot = jnp.where(lane == gid, 1.0, 0.0)
        r = lax.broadcasted_iota(jnp.int32, (t, t), 0)
        c = lax.broadcasted_iota(jnp.int32, (t, t), 1)
        csum = _dot(jnp.where(c <= r, 1.0, 0.0).astype(BF16), onehot.astype(BF16))
        counts = csum[t - 1:t, :]
        before = (lax.broadcasted_iota(jnp.int32, (LANES, LANES), 0)
                  < lax.broadcasted_iota(jnp.int32, (LANES, LANES), 1)).astype(BF16)
        starts = _split3_dot(jnp.broadcast_to(counts, (8, LANES)), before)[0:1]
        pos = (jnp.sum(onehot * (starts + csum), -1, keepdims=True) - 1.0).astype(jnp.int32)
        pos_row = jnp.broadcast_to(pos.astype(F32), (t, LANES)).T[0:1, :].astype(jnp.int32)
        perm = jnp.where(r == pos_row, 1.0, 0.0).astype(BF16)
        xs_ref[...] = _dot(perm, hb_ref[...]).astype(BF16)
        cs_ref[...] = _dot_split3(perm, comb)
        pt_ref[...] = jnp.where(c == pos, 1.0, 0.0).astype(BF16)
        acc_ref[...] = jnp.zeros_like(acc_ref)
        for gg in range(N_GROUPS):
            rng_ref[gg] = starts[0, gg].astype(jnp.int32)
            rng_ref[N_GROUPS + gg] = counts[0, gg].astype(jnp.int32)

    start = rng_ref[g]
    stop = start + rng_ref[N_GROUPS + g]
    lane = lax.broadcasted_iota(jnp.int32, (ts, LANES), 1)

    def sub_tile(j, carry):
        r0 = pl.multiple_of(j * ts, ts)
        x = xs_ref[pl.ds(r0, ts), :]
        cw = cs_ref[pl.ds(r0, ts), :]
        out = None
        for e in range(EXPERTS_PER_GROUP):
            gate = _dot(x, wg_ref[0, e])
            hid = (gate * _sigmoid(gate)) * _dot(x, wu_ref[0, e])
            wgt = jnp.sum(jnp.where(lane == L_EXP + g * EXPERTS_PER_GROUP + e, cw, 0.0), -1, keepdims=True)
            term = wgt * _dot(hid.astype(BF16), wd_ref[0, e])
            out = term if out is None else out + term
        acc_ref[pl.ds(r0, ts), :] += out
        return carry

    lax.fori_loop(start // ts, (stop + ts - 1) // ts, sub_tile, 0)

    @pl.when(g == N_GROUPS - 1)
    def _():
        ffn = _dot(pt_ref[...], acc_ref[...].astype(BF16))
        o_ref[...] = _layer_norm(DEEP_ALPHA * h32_ref[...] + ffn, l2g_ref[...], l2b_ref[...])


def _moe_grouped_call(hb, h32, comb, wg, wu, wd, l2g, l2b, *, tm, ts):
    rows, d = hb.shape
    dff = wg.shape[-1]
    grp = lambda a: a.reshape((N_GROUPS, EXPERTS_PER_GROUP) + a.shape[1:])
    tspec = lambda width, **kw: pl.BlockSpec((tm, width), lambda i, g: (i, 0), **kw)
    wspec = lambda a, b: pl.BlockSpec((1, EXPERTS_PER_GROUP, a, b), lambda i, g: (g, 0, 0, 0))
    full = lambda a: pl.BlockSpec(a.shape, lambda i, g: (0, 0))
    return pl.pallas_call(
        functools.partial(_moe_grouped_kernel, ts=ts),
        grid=(rows // tm, N_GROUPS),
        in_specs=[tspec(d), tspec(LANES), tspec(d, pipeline_mode=pl.Buffered(1)), wspec(d, dff), wspec(d, dff),
                  wspec(dff, d), full(l2g), full(l2b)],
        out_specs=tspec(d),
        out_shape=jax.ShapeDtypeStruct((rows, d), F32),
        scratch_shapes=[pltpu.VMEM((tm, d), BF16), pltpu.VMEM((tm, LANES), F32), pltpu.VMEM((tm, d), F32),
                        pltpu.VMEM((tm, tm), BF16), pltpu.SMEM((2 * N_GROUPS,), jnp.int32)],
        compiler_params=_cparams(("arbitrary", "arbitrary"), vmem=58 * 1024 * 1024),
        name="moe_grouped",
    )(hb, comb, h32, grp(wg), grp(wu), grp(wd), l2g, l2b)


def _pick(n, prefs):
    for t in prefs:
        if n % t == 0:
            return t
    return n


def _prep_w_kernel(w_ref, o_ref):
    big = 4 * W_A
    src_f = big + 2 * H_A
    src_gates = src_f + 3 * W_A + H_B
    o_ref[:, C_QKV:C_QKV + big] = w_ref[:, 0:big]
    o_ref[:, C_FQ:C_FQ + 3 * W_A] = w_ref[:, src_f:src_f + 3 * W_A]
    o_ref[:, C_GATES:C_GATES + 2 * 1024] = w_ref[:, src_gates:src_gates + 2 * 1024]
    ab = w_ref[:, big:big + LANES].astype(F32)
    f_at = (src_f + 3 * W_A) % LANES
    ftile = w_ref[:, src_f + 3 * W_A - f_at:src_f + 3 * W_A - f_at + LANES].astype(F32)
    shifted = lambda to: ftile if (to - f_at) % LANES == 0 else pltpu.roll(ftile, (to - f_at) % LANES, 1)
    f1, f2 = shifted(L_LOGF), shifted(L_C)
    lane = lax.broadcasted_iota(jnp.int32, ab.shape, 1)
    small = jnp.where(lane < L_LOGF, ab, jnp.where(lane < L_C, f1, jnp.where(lane < L_C + H_B, f2, 0.0)))
    o_ref[:, C_SMALL:C_SMALL + LANES] = small.astype(BF16)


def _rearranged_w_in(w_in):
    d = w_in.shape[0]
    return pl.pallas_call(
        _prep_w_kernel,
        out_shape=jax.ShapeDtypeStruct((d, P_COLS), BF16),
        compiler_params=pltpu.CompilerParams(vmem_limit_bytes=VMEM_LIMIT),
        name="prep_w",
    )(w_in)


def kernel(x_prompt, x_sample, cache_k, cache_v, cache_logf, state_gdn, state_conv, page_table, meta_tokens, ln_in_g, ln_in_b, w_in, conv_w, a_log, dt_bias, f_bias, gdn_norm_w, w_branch, w_out, ln1_g, ln1_b, w_router_group, b_router_group, w_router_expert, b_router_expert, w_gate, w_up, w_down, ln2_g, ln2_b):
    bsz, seq, d = x_prompt.shape
    n_dec = x_sample.shape[0]
    row2 = lambda a: a.reshape(1, -1).astype(F32)
    lng, lnb = row2(ln_in_g), row2(ln_in_b)
    w_big = _rearranged_w_in(w_in[0].astype(BF16))
    convw = conv_w[0].astype(F32)
    zpad = lambda a, n: jnp.pad(a.astype(F32), (0, n - a.shape[0]))
    sp = jnp.zeros((8, LANES), F32)
    sp = sp.at[0].set(zpad(a_log[0], LANES)).at[1].set(zpad(dt_bias[0], LANES))
    sp = sp.at[2].set(zpad(jnp.concatenate([jnp.zeros((L_LOGF,), F32), f_bias[0], f_bias[0]]), LANES))

    meta = _front_call(meta_tokens.astype(F32)[None], lng, lnb, w_big, convw, jnp.zeros((1, 8, QKV), F32), sp,
                       jnp.zeros((1, LANES), F32), tm=N_META, emit_t=False)
    pad_rows = CHUNK_A - N_META
    front_pad = lambda a: jnp.pad(a, ((0, 0), (pad_rows, 0), (0, 0)))
    sm_meta = front_pad(meta["small"])
    _, s_meta = _gdn_call(front_pad(meta["qa"]), front_pad(meta["ka"]), front_pad(meta["va"]), sm_meta,
                          jnp.swapaxes(sm_meta, 1, 2)[:, :16], jnp.zeros((1, H_A, DK_A, DV_A), F32), lt=CHUNK_A)
    c_meta = meta["small"][0, :, L_C:L_C + H_B]
    c0 = jnp.zeros((1, LANES), F32).at[0, L_C:L_C + H_B].set(c_meta[-1])
    rnd = lambda a: a.astype(BF16).astype(F32)
    bias = jnp.pad(-c_meta * LOG2E, ((0, LANES - N_META), (0, 0)), constant_values=NEG_INF)
    hi = rnd(bias)
    mid = rnd(bias - hi)
    aug = jnp.zeros((LANES, H_B, DH_B), F32).at[:, :, 0].set(hi).at[:, :, 1].set(mid).at[:, :, 2].set(bias - hi - mid)
    kpad = jnp.pad(meta["fk32"][0], ((0, LANES - N_META), (0, 0))).reshape(LANES, H_B, DH_B)
    kmaug = jnp.concatenate([kpad, aug], -1).reshape(LANES, 2 * H_B * DH_B).astype(BF16)
    vpad = jnp.pad(meta["fv32"][0], ((0, LANES - N_META), (0, 0))).reshape(LANES, H_B, DH_B)
    vones = jnp.zeros((LANES, H_B, DV_AUG - DH_B), F32).at[:, :, 0].set(1.0)
    vmt = jnp.concatenate([vpad, vones], -1).reshape(LANES, H_B * DV_AUG).T.astype(BF16)

    tm = _pick(seq, (512, 256, 128, 64))
    pf = _front_call(x_prompt, lng, lnb, w_big, convw, meta["tail"], sp, c0, tm=tm, emit_t=True)
    o_a, s_prompt = _gdn_call(pf["qa"], pf["ka"], pf["va"], pf["small"], pf["small_t"], s_meta,
                              lt=_pick(seq, (256, 128, 64)))
    c_t = pf["small_t"][:, L_C:L_C + H_B].reshape(bsz, H_B, 1, seq)
    o_b = _foxt_call(pf["fqt"], pf["fkaug"], pf["fvt"], c_t, kmaug, vmt, tq=tm)

    nw = row2(gdn_norm_w[0])
    wbr = w_branch[0].astype(BF16)
    wout = w_out[0].astype(BF16)
    l1g, l1b, l2g, l2b = row2(ln1_g[0]), row2(ln1_b[0]), row2(ln2_g[0]), row2(ln2_b[0])
    wr = jnp.pad(jnp.concatenate([w_router_group[0], w_router_expert[0]], 1).astype(F32),
                 ((0, 0), (0, LANES - N_GROUPS - N_EXPERTS)))
    wr_hi = wr.astype(BF16)
    wr = jnp.concatenate([wr_hi, wr_hi, (wr - wr_hi.astype(F32)).astype(BF16)], 0)
    br = jnp.pad(jnp.concatenate([b_router_group[0], b_router_expert[0]]).astype(F32),
                 (0, LANES - N_GROUPS - N_EXPERTS)).reshape(1, LANES)
    wg, wu, wd = lax.optimization_barrier((w_gate[0].astype(BF16), w_up[0].astype(BF16), w_down[0].astype(BF16)))

    def layer_back(x2, oa2, z2, ob2, gates2, tile, moe_tile):
        h32, hb, comb = _back_call(x2, oa2, z2, ob2, gates2, lng, lnb, nw, wbr, wout, l1g, l1b, wr, br, tm=tile)
        if moe_tile % 256:
            return _moe_call(hb, h32, comb, wg, wu, wd, l2g, l2b, tm=moe_tile)
        return _moe_grouped_call(hb, h32, comb, wg, wu, wd, l2g, l2b, tm=moe_tile, ts=128)

    flat = lambda a: a.reshape(bsz * seq, a.shape[-1])
    n_tok = bsz * seq
    y_prompt = layer_back(flat(x_prompt), flat(o_a), flat(pf["z"]), flat(o_b), flat(pf["gates"]),
                          _pick(n_tok, (512, 256, 128, 64)), _pick(n_tok, (1024, 512, 256, 128, 64))).reshape(bsz, seq, d)

    st = jnp.swapaxes(state_conv[0].astype(F32), 0, 1)
    sf = _sample_front_call(x_sample.reshape(n_dec, d), lng, lnb, w_big, convw, st, sp)
    o_a_s, s_sample = _gdn_step_call(sf["q"], sf["k"], sf["v"], sf["small"], state_gdn[0].astype(F32))
    n_pages = page_table.shape[1]
    o_b_s = _fox_sample_call(page_table, cache_k[0], cache_v[0], cache_logf[0], sf["fq"], sf["fk"], sf["fv"],
                             sf["small"][:, L_LOGF:L_LOGF + H_B], pps=_pick(n_pages, (16, 8, 4, 2, 1)))
    y_sample = layer_back(x_sample.reshape(n_dec, d), o_a_s, sf["z"], o_b_s, sf["gates"], n_dec, n_dec)

    heads = lambda a: a.reshape(a.shape[:-1] + (H_B, DH_B))
    with_meta = lambda m, p: jnp.concatenate([jnp.broadcast_to(m, (bsz,) + m.shape[1:]), p], 1)
    rows = lambda a: a[0].reshape(N_META * H_B, DH_B)
    k_rows, v_rows = _put_meta_rows(pf["kout"], pf["vout"], rows(meta["fk32"]), rows(meta["fv32"]))
    k_prompt = k_rows.reshape(1, bsz, seq + N_META, H_B, DH_B)
    v_prompt = v_rows.reshape(1, bsz, seq + N_META, H_B, DH_B)
    lf_meta = jnp.swapaxes(meta["small"][:, :, L_LOGF:L_LOGF + H_B], 1, 2)
    lf_t = jnp.concatenate([jnp.broadcast_to(lf_meta, (bsz, H_B, N_META)), pf["small_t"][:, L_LOGF:L_LOGF + H_B]], 2)
    logf_prompt = jnp.swapaxes(lf_t, 1, 2)[None]
    conv_prompt = pf["tail"][:, 8 - (CONV_W - 1):][None]
    conv_sample = jnp.concatenate([state_conv[0][:, 1:].astype(F32), sf["new"][:, None]], 1)[None]
    return (y_prompt, y_sample.reshape(n_dec, 1, d), k_prompt, v_prompt, logf_prompt, s_prompt[None], conv_prompt,
            heads(sf["fk"])[None, :, None], heads(sf["fv"])[None, :, None],
            sf["small"][:, L_LOGF:L_LOGF + H_B][None, :, None], s_sample[None], conv_sample)
```

```python
import functools

import jax
import jax.numpy as jnp
from jax import lax
from jax.experimental import pallas as pl
from jax.experimental.pallas import tpu as pltpu

F32 = jnp.float32
BF16 = jnp.bfloat16
HIGHEST = lax.Precision.HIGHEST

N_META = 16
H_A = 4
DK_A = 128
DV_A = 128
CONV_W = 4
CHUNK_A = 64
H_B = 4
DH_B = 128
N_GROUPS = 4
EXPERTS_PER_GROUP = 4
N_EXPERTS = N_GROUPS * EXPERTS_PER_GROUP
D_FF_E = 512
DEPTH = 1
DEEP_ALPHA = (2 * DEPTH) ** 0.25
LN_EPS = 1e-5
RMS_EPS = 1e-6
NEG_INF = -1e30
LOG2E = 1.4426950408889634

LANES = 128
W_A = H_A * DV_A
QKV = 3 * W_A
C_QKV = 0
C_Z = C_QKV + QKV
C_FQ = C_Z + W_A
C_FK = C_FQ + W_A
C_FV = C_FK + W_A
C_GATES = C_FV + W_A
C_SMALL = C_GATES + 2 * 1024
P_COLS = C_SMALL + LANES
L_G, L_BETA, L_LOGF, L_C = 0, 4, 8, 12
L_GRP, L_EXP = 0, 4
DV_AUG = DH_B + 16
L_GID = 32

VMEM_LIMIT = 56 * 1024 * 1024


def _cparams(sem, vmem=VMEM_LIMIT):
    return pltpu.CompilerParams(dimension_semantics=sem, vmem_limit_bytes=vmem)


def _layer_norm(x, g, b):
    mu = jnp.mean(x, -1, keepdims=True)
    xc = x - mu
    var = jnp.mean(xc * xc, -1, keepdims=True)
    return xc * lax.rsqrt(var + LN_EPS) * g + b


def _softplus(x):
    return jnp.maximum(x, 0.0) + jnp.log1p(jnp.exp(-jnp.abs(x)))


def _sigmoid(x):
    return 1.0 / (1.0 + jnp.exp(-x))


def _dot(a, b):
    return jnp.dot(a, b, preferred_element_type=F32)


def _dot_nt(a, b):
    return lax.dot_general(a, b, (((1,), (1,)), ((), ())), preferred_element_type=F32)


def _split3(x):
    hi = x.astype(BF16)
    r1 = x - hi.astype(F32)
    mid = r1.astype(BF16)
    return hi, mid, (r1 - mid.astype(F32)).astype(BF16)


def _dot_split3(a, x):
    n = x.shape[1]
    y = _dot(a, jnp.concatenate(_split3(x), 1))
    return y[:, :n] + y[:, n:2 * n] + y[:, 2 * n:]


def _split3_dot(x, b):
    m = x.shape[0]
    y = _dot(jnp.concatenate(_split3(x), 0), b)
    return y[:m] + y[m:2 * m] + y[2 * m:]


def _dot_tn(a, b):
    return lax.dot_general(a, b, (((0,), (0,)), ((), ())), preferred_element_type=F32)


def _gate_values(sm, sp, rows):
    lane = lax.broadcasted_iota(jnp.int32, (rows, LANES), 1)
    g = -jnp.exp(sp[0:1]) * _softplus(sm + sp[1:2])
    beta = _sigmoid(sm)
    logf = -_softplus(-(sm + sp[2:3]))
    return jnp.where(lane < L_BETA, g, jnp.where(lane < L_LOGF, beta, jnp.where(lane < L_C + 4, logf, 0.0)))


def _front_kernel(x_ref, lng_ref, lnb_ref, w_ref, convw_ref, halo_ref, sp_ref, c0_ref, *rest, names, tm, blk):
    o = dict(zip(names, rest))
    buf_ref, carry_ref = rest[len(names):]
    qa_ref, ka_ref, va_ref, tail_ref, small_ref = o["qa"], o["ka"], o["va"], o["tail"], o["small"]
    j = pl.program_id(1)
    nj = pl.num_programs(1)

    @pl.when(j == 0)
    def _():
        buf_ref[0:8, :] = halo_ref[0]
        carry_ref[...] = c0_ref[...]

    xn = _layer_norm(x_ref[0], lng_ref[...], lnb_ref[...]).astype(BF16)

    def proj(c0, width):
        return _dot(xn, w_ref[:, c0:c0 + width])

    buf_ref[8:8 + tm, :] = proj(C_QKV, QKV)
    fk = proj(C_FK, W_A)
    fv = proj(C_FV, W_A)
    if "fk32" in o:
        o["fk32"][0] = fk
        o["fv32"][0] = fv
    if "kout" in o:
        for h in range(H_B):
            o["kout"][0, pl.ds(h, tm, stride=H_B), :] = fk[:, h * DH_B:(h + 1) * DH_B]
            o["vout"][0, pl.ds(h, tm, stride=H_B), :] = fv[:, h * DH_B:(h + 1) * DH_B]
    if "z" in o:
        o["z"][0] = proj(C_Z, W_A).astype(BF16)
        o["gates"][0] = proj(C_GATES, 2 * 1024).astype(BF16)
        o["fqt"][0] = (proj(C_FQ, W_A) * (DH_B ** -0.5 * LOG2E)).T.astype(BF16)
        fvt = fv.T
        one_row = jnp.where(lax.broadcasted_iota(jnp.int32, (DV_AUG - DH_B, tm), 0) == 0, 1.0, 0.0)
        for h in range(H_B):
            o["fvt"][0, h * DV_AUG:(h + 1) * DV_AUG, :] = jnp.concatenate(
                [fvt[h * DH_B:(h + 1) * DH_B], one_row], 0).astype(BF16)

    vals = _gate_values(proj(C_SMALL, LANES), sp_ref[...], tm)
    span = min(LANES, tm)
    r = lax.broadcasted_iota(jnp.int32, (span, span), 0)
    c = lax.broadcasted_iota(jnp.int32, (span, span), 1)
    same_blk = c >= (r // blk) * blk
    tri = jnp.where(c <= r, jnp.where(same_blk, 1.0, 0.0), 0.0).astype(BF16)
    parts = [_dot_split3(tri, vals[i * span:(i + 1) * span]) for i in range(tm // span)]
    cs = parts[0] if len(parts) == 1 else jnp.concatenate(parts, 0)
    run = carry_ref[...]
    pieces = []
    for b in range(tm // blk):
        piece = cs[b * blk:(b + 1) * blk, :] + run
        pieces.append(piece)
        run = piece[blk - 1:blk, :]
    carry_ref[...] = run
    cfull = pieces[0] if len(pieces) == 1 else jnp.concatenate(pieces, 0)
    lane = lax.broadcasted_iota(jnp.int32, (tm, LANES), 1)
    small = jnp.where(lane < L_BETA, cs, jnp.where(lane < L_C, vals, cfull))
    small_ref[0] = small
    if "small_t" in o:
        o["small_t"][0] = small.T[0:16, :]
        rnd = lambda a: a.astype(BF16).astype(F32)
        bias = (cfull[0:1, :] - cfull) * LOG2E
        hi = rnd(bias)
        mid = rnd(bias - hi)
        lo = bias - hi - mid
        for h in range(H_B):
            col = slice(L_C + h, L_C + h + 1)
            aug = jnp.where(lane == 0, hi[:, col], jnp.where(lane == 1, mid[:, col], jnp.where(lane == 2, lo[:, col], 0.0)))
            o["fkaug"][0, :, 2 * h * DH_B:(2 * h + 1) * DH_B] = fk[:, h * DH_B:(h + 1) * DH_B].astype(BF16)
            o["fkaug"][0, :, (2 * h + 1) * DH_B:(2 * h + 2) * DH_B] = aug.astype(BF16)

    outs = (qa_ref, ka_ref, va_ref)
    for grp in range(QKV // LANES):
        c = grp * LANES
        w = convw_ref[:, c:c + LANES]
        y = (buf_ref[5:5 + tm, c:c + LANES] * w[0:1] + buf_ref[6:6 + tm, c:c + LANES] * w[1:2]
             + buf_ref[7:7 + tm, c:c + LANES] * w[2:3] + buf_ref[8:8 + tm, c:c + LANES] * w[3:4])
        y = y * _sigmoid(y)
        which, head = divmod(grp, H_A)
        if which < 2:
            y = y * lax.rsqrt(jnp.sum(y * y, -1, keepdims=True) + RMS_EPS)
            if which == 0:
                y = y * DK_A ** -0.5
        outs[which][0, :, head * LANES:(head + 1) * LANES] = y.astype(BF16)

    @pl.when(j == nj - 1)
    def _():
        tail_ref[0] = buf_ref[tm:tm + 8, :]

    buf_ref[0:8, :] = buf_ref[tm:tm + 8, :]


def _front_call(x, lng, lnb, w_big, conv_w, halo, sp, c0, *, tm, emit_t):
    bsz, seq, d = x.shape
    blk = min(CHUNK_A, tm)
    nj = seq // tm
    full2 = lambda a: pl.BlockSpec(a.shape, lambda b, j: (0, 0))
    tok = lambda width, dt: (jax.ShapeDtypeStruct((bsz, seq, width), dt),
                             pl.BlockSpec((1, tm, width), lambda b, j: (b, j, 0)))
    tok_t = lambda rows, dt: (jax.ShapeDtypeStruct((bsz, rows, seq), dt),
                              pl.BlockSpec((1, rows, tm), lambda b, j: (b, 0, j)))
    outs = dict(qa=tok(W_A, BF16), ka=tok(W_A, BF16), va=tok(W_A, BF16), small=tok(LANES, F32),
                tail=(jax.ShapeDtypeStruct((bsz, 8, QKV), F32), pl.BlockSpec((1, 8, QKV), lambda b, j: (b, 0, 0))))
    if emit_t:
        skip = N_META * H_B
        kv = (jax.ShapeDtypeStruct((bsz, skip + seq * H_B, DH_B), F32),
              pl.BlockSpec((pl.Element(1), pl.Element(tm * H_B), pl.Element(DH_B)),
                           lambda b, j: (b, pl.multiple_of(skip + j * (tm * H_B), skip), 0)))
        outs.update(kout=kv, vout=kv, z=tok(W_A, BF16), gates=tok(2 * 1024, BF16), small_t=tok_t(16, F32),
                    fqt=tok_t(W_A, BF16), fvt=tok_t(H_B * DV_AUG, BF16), fkaug=tok(2 * W_A, BF16))
    else:
        outs.update(fk32=tok(W_A, F32), fv32=tok(W_A, F32))
    names = tuple(outs)

    res = pl.pallas_call(
        functools.partial(_front_kernel, names=names, tm=tm, blk=blk),
        grid=(bsz, nj),
        in_specs=[pl.BlockSpec((1, tm, d), lambda b, j: (b, j, 0)), full2(lng), full2(lnb),
                  pl.BlockSpec(w_big.shape, lambda b, j: (0, 0), pipeline_mode=pl.Buffered(1)),
                  full2(conv_w), pl.BlockSpec((1, 8, QKV), lambda b, j: (0, 0, 0)), full2(sp), full2(c0)],
        out_specs=[outs[n][1] for n in names],
        out_shape=[outs[n][0] for n in names],
        scratch_shapes=[pltpu.VMEM((tm + 8, QKV), F32), pltpu.VMEM((1, LANES), F32)],
        compiler_params=_cparams(("arbitrary", "arbitrary")),
        name="front",
    )(x, lng, lnb, w_big, conv_w, halo, sp, c0)
    return dict(zip(names, res))


def _put_meta_rows_kernel(k_any, v_any, km_ref, vm_ref, kout_ref, vout_ref):
    del k_any, v_any
    kout_ref[0] = km_ref[...]
    vout_ref[0] = vm_ref[...]


def _put_meta_rows(kout, vout, km, vm):
    bsz = kout.shape[0]
    head = pl.BlockSpec((1,) + km.shape, lambda b: (b, 0, 0))
    small = pl.BlockSpec(km.shape, lambda b: (0, 0))
    anyspec = pl.BlockSpec(memory_space=pl.ANY)
    return pl.pallas_call(
        _put_meta_rows_kernel,
        grid=(bsz,),
        in_specs=[anyspec, anyspec, small, small],
        out_specs=[head, head],
        out_shape=[jax.ShapeDtypeStruct(kout.shape, kout.dtype), jax.ShapeDtypeStruct(vout.shape, vout.dtype)],
        input_output_aliases={0: 0, 1: 1},
        compiler_params=_cparams(("arbitrary",)),
        name="put_meta_rows",
    )(kout, vout, km, vm)


def _sample_front_kernel(x_ref, lng_ref, lnb_ref, w_ref, convw_ref, st_ref, sp_ref,
                         q_ref, k_ref, v_ref, z_ref, fq_ref, fk_ref, fv_ref, gates_ref, small_ref, new_ref):
    rows = x_ref.shape[0]
    xn = _layer_norm(x_ref[...], lng_ref[...], lnb_ref[...]).astype(BF16)

    def proj(c0, width):
        return _dot(xn, w_ref[:, c0:c0 + width])

    new = proj(C_QKV, QKV)
    new_ref[...] = new
    w = convw_ref[...]
    y = st_ref[0] * w[0:1] + st_ref[1] * w[1:2] + st_ref[2] * w[2:3] + new * w[3:4]
    y = y * _sigmoid(y)
    outs = (q_ref, k_ref, v_ref)
    for grp in range(QKV // LANES):
        which, head = divmod(grp, H_A)
        yy = y[:, grp * LANES:(grp + 1) * LANES]
        if which < 2:
            yy = yy * lax.rsqrt(jnp.sum(yy * yy, -1, keepdims=True) + RMS_EPS)
            if which == 0:
                yy = yy * DK_A ** -0.5
        outs[which][:, head * LANES:(head + 1) * LANES] = yy
    z_ref[...] = proj(C_Z, W_A)
    fq_ref[...] = proj(C_FQ, W_A) * DH_B ** -0.5
    fk_ref[...] = proj(C_FK, W_A)
    fv_ref[...] = proj(C_FV, W_A)
    gates_ref[...] = proj(C_GATES, 2 * 1024)
    small_ref[...] = _gate_values(proj(C_SMALL, LANES), sp_ref[...], rows)


def _sample_front_call(x, lng, lnb, w_big, conv_w, st, sp):
    rows = x.shape[0]
    o = lambda width: jax.ShapeDtypeStruct((rows, width), F32)
    res = pl.pallas_call(
        _sample_front_kernel,
        out_shape=[o(W_A), o(W_A), o(W_A), o(W_A), o(W_A), o(W_A), o(W_A), o(2 * 1024), o(LANES), o(QKV)],
        compiler_params=pltpu.CompilerParams(vmem_limit_bytes=VMEM_LIMIT),
        name="sample_front",
    )(x, lng, lnb, w_big, conv_w, st, sp)
    return dict(zip(["q", "k", "v", "z", "fq", "fk", "fv", "gates", "small", "new"], res))


def _gdn_kernel(q_ref, k_ref, v_ref, sm_ref, smt_ref, s0_ref, o_ref, sout_ref, state_ref, *, lt):
    j = pl.program_id(0)
    nj = pl.num_programs(0)
    ch = CHUNK_A
    bsz = q_ref.shape[0]

    @pl.when(j == 0)
    def _():
        for b in range(bsz):
            state_ref[b] = s0_ref[0]

    ri = lax.broadcasted_iota(jnp.int32, (ch, ch), 0)
    ci = lax.broadcasted_iota(jnp.int32, (ch, ch), 1)
    incl = ri >= ci
    strict = ri > ci
    units = [(b, c, h) for b in range(bsz) for c in range(lt // ch) for h in range(H_A)]
    t = {}
    for u in units:
        b, c, h = u
        rows = slice(c * ch, (c + 1) * ch)
        cols = slice(h * DK_A, (h + 1) * DK_A)
        kc = k_ref[b, rows, cols]
        qc = q_ref[b, rows, cols]
        gc_col = sm_ref[b, rows, L_G + h:L_G + h + 1]
        beta = sm_ref[b, rows, L_BETA + h:L_BETA + h + 1]
        gc_row = smt_ref[b, L_G + h:L_G + h + 1, rows]
        gc_last = sm_ref[b, c * ch + ch - 1:c * ch + ch, L_G + h:L_G + h + 1]
        eg = jnp.exp(gc_col)
        kf = kc.astype(F32)
        kb = kf * beta
        t[u] = dict(kc=kc, decay=jnp.exp(jnp.where(incl, gc_col - gc_row, NEG_INF)),
                    lhs=jnp.concatenate([kb.astype(BF16), qc], 0),
                    rhs=jnp.concatenate([v_ref[b, rows, cols].astype(F32) * beta, kb * eg], 1),
                    q_dec=(qc.astype(F32) * eg).astype(BF16),
                    k_dec=(kf * jnp.exp(gc_last - gc_col)).astype(BF16), g_tot=jnp.exp(gc_last))
    for u in units:
        d = t[u]
        a = _dot_nt(d["lhs"], d["kc"])
        d["m"] = jnp.where(strict, a[:ch] * d["decay"], 0.0)
        d["attn"] = (a[ch:] * d["decay"]).astype(BF16)
    for u in units:
        d = t[u]
        mb = d["m"].astype(BF16)
        d["x"] = -d["m"]
        d["p"] = _dot(mb, mb)
    for _ in range(4):
        for u in units:
            d = t[u]
            pb = d["p"].astype(BF16)
            r = _dot(jnp.concatenate([pb, d["x"].astype(BF16)], 0), pb)
            d["x"] = d["x"] + d["p"] + r[ch:]
            d["p"] = r[:ch]
    for u in units:
        d = t[u]
        d["x"] = d["x"] + d["p"] + _dot(d["x"].astype(BF16), d["p"].astype(BF16))
    for u in units:
        d = t[u]
        sol = d["rhs"] + _dot(d["x"].astype(BF16), d["rhs"].astype(BF16))
        d["u"] = sol[:, :DV_A]
        d["wq"] = jnp.concatenate([sol[:, DV_A:].astype(BF16), d["q_dec"]], 0)
    heads = [(b, h) for b in range(bsz) for h in range(H_A)]
    s = {bh: state_ref[bh[0], bh[1]] for bh in heads}
    for c in range(lt // ch):
        ws = {bh: _dot(t[bh[0], c, bh[1]]["wq"], s[bh].astype(BF16)) for bh in heads}
        v_new = {bh: (t[bh[0], c, bh[1]]["u"] - ws[bh][:ch]).astype(BF16) for bh in heads}
        for bh in heads:
            b, h = bh
            d = t[b, c, h]
            o = ws[bh][ch:] + _dot(d["attn"], v_new[bh])
            o_ref[b, c * ch:(c + 1) * ch, h * DV_A:(h + 1) * DV_A] = o.astype(o_ref.dtype)
        for bh in heads:
            d = t[bh[0], c, bh[1]]
            s[bh] = s[bh] * d["g_tot"] + _dot_tn(d["k_dec"], v_new[bh])
    for bh in heads:
        state_ref[bh[0], bh[1]] = s[bh]

    @pl.when(j == nj - 1)
    def _():
        sout_ref[...] = state_ref[...]


def _gdn_call(q, k, v, small, small_t, s0, *, lt):
    bsz, seq, _ = q.shape
    nj = seq // lt
    tspec = pl.BlockSpec((bsz, lt, W_A), lambda j: (0, j, 0))
    sshape = (bsz, H_A, DK_A, DV_A)
    return pl.pallas_call(
        functools.partial(_gdn_kernel, lt=lt),
        grid=(nj,),
        in_specs=[tspec, tspec, tspec, pl.BlockSpec((bsz, lt, LANES), lambda j: (0, j, 0)),
                  pl.BlockSpec((bsz, 16, lt), lambda j: (0, 0, j)),
                  pl.BlockSpec((1, H_A, DK_A, DV_A), lambda j: (0, 0, 0, 0))],
        out_specs=[tspec, pl.BlockSpec(sshape, lambda j: (0, 0, 0, 0))],
        out_shape=[jax.ShapeDtypeStruct((bsz, seq, W_A), BF16), jax.ShapeDtypeStruct(sshape, F32)],
        scratch_shapes=[pltpu.VMEM(sshape, F32)],
        compiler_params=_cparams(("arbitrary",)),
        name="gdn_chunk",
    )(q, k, v, small, small_t, s0)


def _gdn_step_kernel(q_ref, k_ref, v_ref, sm_ref, s_ref, o_ref, sout_ref):
    ri = lax.broadcasted_iota(jnp.int32, (DK_A, DK_A), 0)
    ci = lax.broadcasted_iota(jnp.int32, (DK_A, DK_A), 1)
    eye = ri == ci

    def column(row):
        return jnp.sum(jnp.where(eye, jnp.broadcast_to(row, (DK_A, DK_A)), 0.0), -1, keepdims=True)

    for i in range(q_ref.shape[0]):
        sm = sm_ref[i]
        for h in range(H_A):
            cols = slice(h * DK_A, (h + 1) * DK_A)
            kcol = column(k_ref[i, :, cols])
            qcol = column(q_ref[i, :, cols])
            s = s_ref[i, h] * jnp.exp(sm[:, L_G + h:L_G + h + 1])
            v_t = (v_ref[i, :, cols] - jnp.sum(kcol * s, 0, keepdims=True)) * sm[:, L_BETA + h:L_BETA + h + 1]
            s = s + kcol * v_t
            sout_ref[i, h] = s
            o_ref[i, :, cols] = jnp.sum(qcol * s, 0, keepdims=True)


def _gdn_step_call(q, k, v, small, state):
    n = q.shape[0]
    per = _pick(n, (4, 2, 1))
    row = lambda a: a.reshape(n, 1, a.shape[-1])
    rspec = lambda width: pl.BlockSpec((per, 1, width), lambda b: (b, 0, 0))
    sspec = pl.BlockSpec((per, H_A, DK_A, DV_A), lambda b: (b, 0, 0, 0))
    o, s = pl.pallas_call(
        _gdn_step_kernel,
        grid=(n // per,),
        in_specs=[rspec(W_A), rspec(W_A), rspec(W_A), rspec(LANES), sspec],
        out_specs=[rspec(W_A), sspec],
        out_shape=[jax.ShapeDtypeStruct((n, 1, W_A), F32), jax.ShapeDtypeStruct(state.shape, F32)],
        compiler_params=_cparams(("arbitrary",)),
        name="gdn_step",
    )(row(q), row(k), row(v), row(small), state)
    return o.reshape(n, W_A), s


def _foxt_kernel(qt_ref, ka_ref, vt_ref, ct_ref, kma_ref, vmt_ref, o_ref, m_sc, acc_sc, *, tq, sub):
    qi = pl.program_id(1)
    qstart = pl.multiple_of(qi * tq, tq)
    heads = range(H_B)
    hs = lambda h: slice(h * DH_B, (h + 1) * DH_B)
    ha = lambda h: slice(2 * h * DH_B, (2 * h + 2) * DH_B)
    hv = lambda h: slice(h * DV_AUG, (h + 1) * DV_AUG)
    ones = jnp.where(lax.broadcasted_iota(jnp.int32, (DH_B, tq), 0) < 3, 1.0, 0.0).astype(BF16)
    qa = [jnp.concatenate([qt_ref[0, hs(h), :], ones], 0) for h in heads]
    c_first = [ct_ref[0, h, :, pl.ds(qstart, LANES)][:, 0:1] for h in heads]

    def step(keys, vt, offs, masks, first):
        parts = range(len(keys[0]))
        s = [[_dot(keys[h][i], qa[h]) for i in parts] for h in heads]
        p, alpha = [], []
        for h in heads:
            sh = [s[h][i] if masks is None else jnp.where(masks[i], s[h][i], NEG_INF) for i in parts]
            blk_max = jnp.max(sh[0], 0, keepdims=True)
            for i in parts[1:]:
                blk_max = jnp.maximum(blk_max, jnp.max(sh[i], 0, keepdims=True))
            blk_max = blk_max + offs[h]
            m_new = blk_max if first else jnp.maximum(m_sc[h], blk_max)
            shift = m_new - offs[h]
            p.append([jnp.exp2(sh[i] - shift).astype(BF16) for i in parts])
            if not first:
                alpha.append(jnp.exp2(m_sc[h] - m_new))
            m_sc[h] = m_new
        for h in heads:
            pv = _dot(vt[h][0], p[h][0])
            for i in parts[1:]:
                pv = pv + _dot(vt[h][i], p[h][i])
            acc_sc[h] = pv if first else alpha[h] * acc_sc[h] + pv

    step([[kma_ref[:, ha(h)]] for h in heads], [[vmt_ref[hv(h), :]] for h in heads],
         [c_first[h] * LOG2E for h in heads], None, True)

    def block(start, masked):
        offs = [(c_first[h] - ct_ref[0, h, :, pl.ds(start, LANES)][:, 0:1]) * LOG2E for h in heads]
        parts = range(tq // sub)
        masks = None
        if masked:
            masks = [lax.broadcasted_iota(jnp.int32, (sub, tq), 0) + i * sub
                     <= lax.broadcasted_iota(jnp.int32, (sub, tq), 1) for i in parts]
        step([[ka_ref[0, pl.ds(start + i * sub, sub), ha(h)] for i in parts] for h in heads],
             [[vt_ref[0, hv(h), pl.ds(start + i * sub, sub)] for i in parts] for h in heads], offs, masks, False)

    def body(jb, carry):
        block(pl.multiple_of(jb * tq, tq), False)
        return carry

    lax.fori_loop(0, qi, body, 0)
    block(qstart, True)
    for h in heads:
        acc = acc_sc[h]
        o_ref[0, :, hs(h)] = (acc[:DH_B] / acc[DH_B:DH_B + 1]).T.astype(o_ref.dtype)


def _foxt_call(fqt, fkaug, fvt, c_t, kmaug, vmt, *, tq):
    bsz, hd, seq = fqt.shape
    hdv = fvt.shape[1]
    once = dict(pipeline_mode=pl.Buffered(1))
    return pl.pallas_call(
        functools.partial(_foxt_kernel, tq=tq, sub=min(256, tq)),
        grid=(bsz, seq // tq),
        in_specs=[pl.BlockSpec((1, hd, tq), lambda b, i: (b, 0, i)),
                  pl.BlockSpec((1, seq, 2 * hd), lambda b, i: (b, 0, 0), **once),
                  pl.BlockSpec((1, hdv, seq), lambda b, i: (b, 0, 0), **once),
                  pl.BlockSpec((1, H_B, 1, seq), lambda b, i: (b, 0, 0, 0)),
                  pl.BlockSpec(kmaug.shape, lambda b, i: (0, 0)), pl.BlockSpec(vmt.shape, lambda b, i: (0, 0))],
        out_specs=pl.BlockSpec((1, tq, hd), lambda b, i: (b, i, 0)),
        out_shape=jax.ShapeDtypeStruct((bsz, seq, hd), BF16),
        scratch_shapes=[pltpu.VMEM((H_B, 1, tq), F32), pltpu.VMEM((H_B, DV_AUG, tq), F32)],
        compiler_params=_cparams(("arbitrary", "arbitrary")),
        name="fox_prompt",
    )(fqt, fkaug, fvt, c_t, kmaug, vmt)


def _fox_sample_kernel(pt_ref, *refs, pps):
    k_refs = refs[:pps]
    v_refs = refs[pps:2 * pps]
    lf_refs = refs[2 * pps:3 * pps]
    q_ref, kn_ref, vn_ref, lfn_ref, wsuf_ref, o_ref, m_sc, l_sc, run_sc, acc_sc = refs[3 * pps:]
    j = pl.program_id(1)
    nj = pl.num_programs(1)
    rows = k_refs[0].shape[1]
    q8 = q_ref[0]

    @pl.when(j == 0)
    def _():
        run_sc[...] = lfn_ref[0]
        m_sc[...] = jnp.broadcast_to(jnp.sum(q8 * kn_ref[0], -1, keepdims=True), (8, LANES))
        l_sc[...] = jnp.ones((8, LANES), F32)
        acc_sc[...] = vn_ref[0]

    qb = q8.astype(BF16)
    lf = jnp.concatenate([lf_refs[i][0] for i in range(pps)], 0)
    within = _split3_dot(lf, wsuf_ref[...])
    tot = jnp.broadcast_to(jnp.sum(lf, -1, keepdims=True), (8 * pps, LANES))
    run = run_sc[...]
    valid = (lax.broadcasted_iota(jnp.int32, (8, rows), 1) % H_B) == lax.broadcasted_iota(jnp.int32, (8, rows), 0)
    scores = []
    for i in range(pps):
        sl = slice(8 * i, 8 * i + 8)
        bias = within[sl] + run[:, 0:1]
        scores.append(jnp.where(valid, _dot_nt(qb, k_refs[i][0].astype(BF16)) + bias, NEG_INF))
        run = run + tot[sl]
    run_sc[...] = run
    s = jnp.concatenate(scores, 1)
    m_prev = m_sc[...][:, 0:1]
    m_new = jnp.maximum(m_prev, jnp.max(s, -1, keepdims=True))
    alpha = jnp.exp(m_prev - m_new)
    p = jnp.exp(s - m_new)
    l_sc[...] = jnp.broadcast_to(alpha * l_sc[...][:, 0:1] + jnp.sum(p, -1, keepdims=True), (8, LANES))
    m_sc[...] = jnp.broadcast_to(m_new, (8, LANES))
    acc = alpha * acc_sc[...]
    for i in range(pps):
        acc = acc + _dot(p[:, i * rows:(i + 1) * rows].astype(BF16), v_refs[i][0].astype(BF16))
    acc_sc[...] = acc

    @pl.when(j == nj - 1)
    def _():
        o_ref[0] = acc / l_sc[...][:, 0:1]


def _fox_sample_call(page_table, cache_k, cache_v, cache_logf, fq, fk, fv, logf_new, *, pps):
    n, n_pages = page_table.shape
    n_pool, page = cache_k.shape[0], cache_k.shape[1]
    rows = page * H_B
    ck = cache_k.reshape(n_pool, rows, DH_B)
    cv = cache_v.reshape(n_pool, rows, DH_B)
    clf = jnp.pad(jnp.swapaxes(cache_logf, 1, 2), ((0, 0), (0, 8 - H_B), (0, 0)))
    tile8 = lambda a: jnp.pad(a.reshape(n, H_B, DH_B), ((0, 0), (0, 8 - H_B), (0, 0)))
    lfn = jnp.broadcast_to(jnp.pad(logf_new, ((0, 0), (0, 8 - H_B)))[:, :, None], (n, 8, LANES))
    wsuf = (lax.broadcasted_iota(jnp.int32, (page, rows), 0)
            > lax.broadcasted_iota(jnp.int32, (page, rows), 1) // H_B).astype(BF16)

    def page_map(i):
        return lambda b, j, pt: (pt[b * n_pages + n_pages - 1 - (j * pps + i)], 0, 0)

    kv_specs = [pl.BlockSpec((1, rows, DH_B), page_map(i)) for i in range(pps)]
    lf_specs = [pl.BlockSpec((1, 8, page), page_map(i)) for i in range(pps)]
    tspec = pl.BlockSpec((1, 8, LANES), lambda b, j, pt: (b, 0, 0))
    full = lambda a: pl.BlockSpec(a.shape, lambda b, j, pt: (0, 0))
    grid_spec = pltpu.PrefetchScalarGridSpec(
        num_scalar_prefetch=1,
        grid=(n, n_pages // pps),
        in_specs=kv_specs + kv_specs + lf_specs + [tspec, tspec, tspec, tspec, full(wsuf)],
        out_specs=tspec,
        scratch_shapes=[pltpu.VMEM((8, LANES), F32)] * 4,
    )
    o = pl.pallas_call(
        functools.partial(_fox_sample_kernel, pps=pps),
        grid_spec=grid_spec,
        out_shape=jax.ShapeDtypeStruct((n, 8, DH_B), F32),
        compiler_params=_cparams(("arbitrary", "arbitrary")),
        name="fox_sample",
    )(page_table.reshape(-1), *([ck] * pps), *([cv] * pps), *([clf] * pps), tile8(fq), tile8(fk), tile8(fv), lfn,
      wsuf)
    return o[:, :H_B].reshape(n, H_B * DH_B)


def _back_kernel(x_ref, oa_ref, z_ref, ob_ref, gates_ref, lng_ref, lnb_ref, nw_ref, wbr_ref, wout_ref,
                 l1g_ref, l1b_ref, wr_ref, br_ref, h32_ref, hb_ref, comb_ref):
    tm = x_ref.shape[0]
    xn = _layer_norm(x_ref[...], lng_ref[...], lnb_ref[...])
    nw = nw_ref[...]
    parts = []
    for h in range(H_A):
        cols = slice(h * DV_A, (h + 1) * DV_A)
        o = oa_ref[:, cols].astype(F32)
        o = o * lax.rsqrt(jnp.mean(o * o, -1, keepdims=True) + RMS_EPS) * nw
        zz = z_ref[:, cols].astype(F32)
        parts.append((o * (zz * _sigmoid(zz))).astype(BF16))
    oa = jnp.concatenate(parts, 1)
    d = wout_ref.shape[0]
    merged = (_sigmoid(gates_ref[:, 0:d].astype(F32)) * _dot(oa, wbr_ref[0])
              + _sigmoid(gates_ref[:, d:2 * d].astype(F32)) * _dot(ob_ref[...].astype(BF16), wbr_ref[1]))
    mix = _dot(merged.astype(BF16), wout_ref[...])
    hh = _layer_norm(DEEP_ALPHA * xn + mix, l1g_ref[...], l1b_ref[...])
    h32_ref[...] = hh
    hb_ref[...] = hh.astype(BF16)

    h_hi = hh.astype(BF16)
    h_mid = (hh - h_hi.astype(F32)).astype(BF16)
    logits = _dot(jnp.concatenate([h_hi, h_mid, h_hi], 1), wr_ref[...]) + br_ref[...]
    lane = lax.broadcasted_iota(jnp.int32, (tm, LANES), 1)
    big = jnp.int32(LANES)
    is_grp = lane < L_EXP
    gl = jnp.where(is_grp, logits, NEG_INF)
    gmax = jnp.max(gl, -1, keepdims=True)
    gsel = jnp.min(jnp.where(jnp.logical_and(is_grp, gl == gmax), lane, big), -1, keepdims=True)
    gw = 1.0 / jnp.sum(jnp.exp(gl - gmax), -1, keepdims=True)
    lo = L_EXP + gsel * EXPERTS_PER_GROUP
    in_grp = jnp.logical_and(lane >= lo, lane < lo + EXPERTS_PER_GROUP)
    el = jnp.where(in_grp, logits, NEG_INF)
    m1 = jnp.max(el, -1, keepdims=True)
    i1 = jnp.min(jnp.where(el == m1, lane, big), -1, keepdims=True)
    el2 = jnp.where(lane == i1, NEG_INF, el)
    m2 = jnp.max(el2, -1, keepdims=True)
    i2 = jnp.min(jnp.where(el2 == m2, lane, big), -1, keepdims=True)
    e2 = jnp.exp(m2 - m1)
    w1 = gw / (1.0 + e2)
    w2 = gw * e2 / (1.0 + e2)
    comb_ref[...] = jnp.where(lane == i1, w1, jnp.where(lane == i2, w2,
                                                        jnp.where(lane == L_GID, gsel.astype(F32), 0.0)))


def _back_call(x, oa, z, ob, gates, lng, lnb, nw, wbr, wout, l1g, l1b, wr, br, *, tm):
    rows, d = x.shape
    tspec = lambda width: pl.BlockSpec((tm, width), lambda i: (i, 0))
    full = lambda a: pl.BlockSpec(a.shape, lambda i: (0,) * a.ndim)
    return pl.pallas_call(
        _back_kernel,
        grid=(rows // tm,),
        in_specs=[tspec(d), tspec(W_A), tspec(W_A), tspec(W_A), tspec(2 * d), full(lng), full(lnb), full(nw),
                  full(wbr), full(wout), full(l1g), full(l1b), full(wr), full(br)],
        out_specs=[tspec(d), tspec(d), tspec(LANES)],
        out_shape=[jax.ShapeDtypeStruct((rows, d), F32), jax.ShapeDtypeStruct((rows, d), BF16),
                   jax.ShapeDtypeStruct((rows, LANES), F32)],
        compiler_params=_cparams(("arbitrary",)),
        name="back",
    )(x, oa, z, ob, gates, lng, lnb, nw, wbr, wout, l1g, l1b, wr, br)


def _moe_kernel(hb_ref, h32_ref, comb_ref, wg_ref, wu_ref, wd_ref, l2g_ref, l2b_ref, y_ref, acc_ref):
    e = pl.program_id(1)
    ne = pl.num_programs(1)

    @pl.when(e == 0)
    def _():
        acc_ref[...] = jnp.zeros_like(acc_ref)

    x = hb_ref[...]
    g = _dot(x, wg_ref[0])
    hid = (g * _sigmoid(g)) * _dot(x, wu_ref[0])
    lane = lax.broadcasted_iota(jnp.int32, comb_ref.shape, 1)
    wgt = jnp.sum(jnp.where(lane == L_EXP + e, comb_ref[...], 0.0), -1, keepdims=True)
    acc_ref[...] += wgt * _dot(hid.astype(BF16), wd_ref[0])

    @pl.when(e == ne - 1)
    def _():
        y_ref[...] = _layer_norm(DEEP_ALPHA * h32_ref[...] + acc_ref[...], l2g_ref[...], l2b_ref[...])


def _moe_call(hb, h32, comb, wg, wu, wd, l2g, l2b, *, tm):
    rows, d = h32.shape
    ne, _, dff = wg.shape
    tspec = lambda width: pl.BlockSpec((tm, width), lambda i, e: (i, 0))
    full = lambda a: pl.BlockSpec(a.shape, lambda i, e: (0,) * a.ndim)
    return pl.pallas_call(
        _moe_kernel,
        grid=(rows // tm, ne),
        in_specs=[tspec(d), tspec(d), tspec(LANES),
                  pl.BlockSpec((1, d, dff), lambda i, e: (e, 0, 0)), pl.BlockSpec((1, d, dff), lambda i, e: (e, 0, 0)),
                  pl.BlockSpec((1, dff, d), lambda i, e: (e, 0, 0)), full(l2g), full(l2b)],
        out_specs=tspec(d),
        out_shape=jax.ShapeDtypeStruct((rows, d), F32),
        scratch_shapes=[pltpu.VMEM((tm, d), F32)],
        compiler_params=_cparams(("arbitrary", "arbitrary")),
        name="moe",
    )(hb, h32, comb, wg, wu, wd, l2g, l2b)


def _moe_grouped_kernel(hb_ref, comb_ref, h32_ref, wg_ref, wu_ref, wd_ref, l2g_ref, l2b_ref, o_ref,
                        xs_ref, cs_ref, acc_ref, pt_ref, rng_ref, *, ts):
    g = pl.program_id(1)
    t = hb_ref.shape[0]

    @pl.when(g == 0)
    def _():
        comb = comb_ref[...]
        lane = lax.broadcasted_iota(jnp.int32, (t, LANES), 1)
        gid = comb[:, L_GID:L_GID + 1].astype(jnp.int32)
        onehot = jnp.where(lane == gid, 1.0, 0.0)
        r = lax.broadcasted_iota(jnp.int32, (t, t), 0)
        c = lax.broadcasted_iota(jnp.int32, (t, t), 1)
        csum = _dot(jnp.where(c <= r, 1.0, 0.0).astype(BF16), onehot.astype(BF16))
        counts = csum[t - 1:t, :]
        before = (lax.broadcasted_iota(jnp.int32, (LANES, LANES), 0)
                  < lax.broadcasted_iota(jnp.int32, (LANES, LANES), 1)).astype(BF16)
        starts = _split3_dot(jnp.broadcast_to(counts, (8, LANES)), before)[0:1]
        pos = (jnp.sum(onehot * (starts + csum), -1, keepdims=True) - 1.0).astype(jnp.int32)
        pos_row = jnp.broadcast_to(pos.astype(F32), (t, LANES)).T[0:1, :].astype(jnp.int32)
        perm = jnp.where(r == pos_row, 1.0, 0.0).astype(BF16)
        xs_ref[...] = _dot(perm, hb_ref[...]).astype(BF16)
        cs_ref[...] = _dot_split3(perm, comb)
        pt_ref[...] = jnp.where(c == pos, 1.0, 0.0).astype(BF16)
        acc_ref[...] = jnp.zeros_like(acc_ref)
        for gg in range(N_GROUPS):
            rng_ref[gg] = starts[0, gg].astype(jnp.int32)
            rng_ref[N_GROUPS + gg] = counts[0, gg].astype(jnp.int32)

    start = rng_ref[g]
    stop = start + rng_ref[N_GROUPS + g]
    lane = lax.broadcasted_iota(jnp.int32, (ts, LANES), 1)

    def sub_tile(j, carry):
        r0 = pl.multiple_of(j * ts, ts)
        x = xs_ref[pl.ds(r0, ts), :]
        cw = cs_ref[pl.ds(r0, ts), :]
        out = None
        for e in range(EXPERTS_PER_GROUP):
            gate = _dot(x, wg_ref[0, e])
            hid = (gate * _sigmoid(gate)) * _dot(x, wu_ref[0, e])
            wgt = jnp.sum(jnp.where(lane == L_EXP + g * EXPERTS_PER_GROUP + e, cw, 0.0), -1, keepdims=True)
            term = wgt * _dot(hid.astype(BF16), wd_ref[0, e])
            out = term if out is None else out + term
        acc_ref[pl.ds(r0, ts), :] += out
        return carry

    lax.fori_loop(start // ts, (stop + ts - 1) // ts, sub_tile, 0)

    @pl.when(g == N_GROUPS - 1)
    def _():
        ffn = _dot(pt_ref[...], acc_ref[...].astype(BF16))
        o_ref[...] = _layer_norm(DEEP_ALPHA * h32_ref[...] + ffn, l2g_ref[...], l2b_ref[...])


def _moe_grouped_call(hb, h32, comb, wg, wu, wd, l2g, l2b, *, tm, ts):
    rows, d = hb.shape
    dff = wg.shape[-1]
    grp = lambda a: a.reshape((N_GROUPS, EXPERTS_PER_GROUP) + a.shape[1:])
    tspec = lambda width, **kw: pl.BlockSpec((tm, width), lambda i, g: (i, 0), **kw)
    wspec = lambda a, b: pl.BlockSpec((1, EXPERTS_PER_GROUP, a, b), lambda i, g: (g, 0, 0, 0))
    full = lambda a: pl.BlockSpec(a.shape, lambda i, g: (0, 0))
    return pl.pallas_call(
        functools.partial(_moe_grouped_kernel, ts=ts),
        grid=(rows // tm, N_GROUPS),
        in_specs=[tspec(d), tspec(LANES), tspec(d, pipeline_mode=pl.Buffered(1)), wspec(d, dff), wspec(d, dff),
                  wspec(dff, d), full(l2g), full(l2b)],
        out_specs=tspec(d),
        out_shape=jax.ShapeDtypeStruct((rows, d), F32),
        scratch_shapes=[pltpu.VMEM((tm, d), BF16), pltpu.VMEM((tm, LANES), F32), pltpu.VMEM((tm, d), F32),
                        pltpu.VMEM((tm, tm), BF16), pltpu.SMEM((2 * N_GROUPS,), jnp.int32)],
        compiler_params=_cparams(("arbitrary", "arbitrary"), vmem=58 * 1024 * 1024),
        name="moe_grouped",
    )(hb, comb, h32, grp(wg), grp(wu), grp(wd), l2g, l2b)


def _pick(n, prefs):
    for t in prefs:
        if n % t == 0:
            return t
    return n


def _prep_w_kernel(w_ref, o_ref):
    big = 4 * W_A
    src_f = big + 2 * H_A
    src_gates = src_f + 3 * W_A + H_B
    o_ref[:, C_QKV:C_QKV + big] = w_ref[:, 0:big]
    o_ref[:, C_FQ:C_FQ + 3 * W_A] = w_ref[:, src_f:src_f + 3 * W_A]
    o_ref[:, C_GATES:C_GATES + 2 * 1024] = w_ref[:, src_gates:src_gates + 2 * 1024]
    ab = w_ref[:, big:big + LANES].astype(F32)
    f_at = (src_f + 3 * W_A) % LANES
    ftile = w_ref[:, src_f + 3 * W_A - f_at:src_f + 3 * W_A - f_at + LANES].astype(F32)
    shifted = lambda to: ftile if (to - f_at) % LANES == 0 else pltpu.roll(ftile, (to - f_at) % LANES, 1)
    f1, f2 = shifted(L_LOGF), shifted(L_C)
    lane = lax.broadcasted_iota(jnp.int32, ab.shape, 1)
    small = jnp.where(lane < L_LOGF, ab, jnp.where(lane < L_C, f1, jnp.where(lane < L_C + H_B, f2, 0.0)))
    o_ref[:, C_SMALL:C_SMALL + LANES] = small.astype(BF16)


def _rearranged_w_in(w_in):
    d = w_in.shape[0]
    return pl.pallas_call(
        _prep_w_kernel,
        out_shape=jax.ShapeDtypeStruct((d, P_COLS), BF16),
        compiler_params=pltpu.CompilerParams(vmem_limit_bytes=VMEM_LIMIT),
        name="prep_w",
    )(w_in)


def kernel(x_prompt, x_sample, cache_k, cache_v, cache_logf, state_gdn, state_conv, page_table, meta_tokens, ln_in_g, ln_in_b, w_in, conv_w, a_log, dt_bias, f_bias, gdn_norm_w, w_branch, w_out, ln1_g, ln1_b, w_router_group, b_router_group, w_router_expert, b_router_expert, w_gate, w_up, w_down, ln2_g, ln2_b):
    bsz, seq, d = x_prompt.shape
    n_dec = x_sample.shape[0]
    row2 = lambda a: a.reshape(1, -1).astype(F32)
    lng, lnb = row2(ln_in_g), row2(ln_in_b)
    w_big = _rearranged_w_in(w_in[0].astype(BF16))
    convw = conv_w[0].astype(F32)
    zpad = lambda a, n: jnp.pad(a.astype(F32), (0, n - a.shape[0]))
    sp = jnp.zeros((8, LANES), F32)
    sp = sp.at[0].set(zpad(a_log[0], LANES)).at[1].set(zpad(dt_bias[0], LANES))
    sp = sp.at[2].set(zpad(jnp.concatenate([jnp.zeros((L_LOGF,), F32), f_bias[0], f_bias[0]]), LANES))

    meta = _front_call(meta_tokens.astype(F32)[None], lng, lnb, w_big, convw, jnp.zeros((1, 8, QKV), F32), sp,
                       jnp.zeros((1, LANES), F32), tm=N_META, emit_t=False)
    pad_rows = CHUNK_A - N_META
    front_pad = lambda a: jnp.pad(a, ((0, 0), (pad_rows, 0), (0, 0)))
    sm_meta = front_pad(meta["small"])
    _, s_meta = _gdn_call(front_pad(meta["qa"]), front_pad(meta["ka"]), front_pad(meta["va"]), sm_meta,
                          jnp.swapaxes(sm_meta, 1, 2)[:, :16], jnp.zeros((1, H_A, DK_A, DV_A), F32), lt=CHUNK_A)
    c_meta = meta["small"][0, :, L_C:L_C + H_B]
    c0 = jnp.zeros((1, LANES), F32).at[0, L_C:L_C + H_B].set(c_meta[-1])
    rnd = lambda a: a.astype(BF16).astype(F32)
    bias = jnp.pad(-c_meta * LOG2E, ((0, LANES - N_META), (0, 0)), constant_values=NEG_INF)
    hi = rnd(bias)
    mid = rnd(bias - hi)
    aug = jnp.zeros((LANES, H_B, DH_B), F32).at[:, :, 0].set(hi).at[:, :, 1].set(mid).at[:, :, 2].set(bias - hi - mid)
    kpad = jnp.pad(meta["fk32"][0], ((0, LANES - N_META), (0, 0))).reshape(LANES, H_B, DH_B)
    kmaug = jnp.concatenate([kpad, aug], -1).reshape(LANES, 2 * H_B * DH_B).astype(BF16)
    vpad = jnp.pad(meta["fv32"][0], ((0, LANES - N_META), (0, 0))).reshape(LANES, H_B, DH_B)
    vones = jnp.zeros((LANES, H_B, DV_AUG - DH_B), F32).at[:, :, 0].set(1.0)
    vmt = jnp.concatenate([vpad, vones], -1).reshape(LANES, H_B * DV_AUG).T.astype(BF16)

    tm = _pick(seq, (512, 256, 128, 64))
    pf = _front_call(x_prompt, lng, lnb, w_big, convw, meta["tail"], sp, c0, tm=tm, emit_t=True)
    o_a, s_prompt = _gdn_call(pf["qa"], pf["ka"], pf["va"], pf["small"], pf["small_t"], s_meta,
                              lt=_pick(seq, (256, 128, 64)))
    c_t = pf["small_t"][:, L_C:L_C + H_B].reshape(bsz, H_B, 1, seq)
    o_b = _foxt_call(pf["fqt"], pf["fkaug"], pf["fvt"], c_t, kmaug, vmt, tq=tm)

    nw = row2(gdn_norm_w[0])
    wbr = w_branch[0].astype(BF16)
    wout = w_out[0].astype(BF16)
    l1g, l1b, l2g, l2b = row2(ln1_g[0]), row2(ln1_b[0]), row2(ln2_g[0]), row2(ln2_b[0])
    wr = jnp.pad(jnp.concatenate([w_router_group[0], w_router_expert[0]], 1).astype(F32),
                 ((0, 0), (0, LANES - N_GROUPS - N_EXPERTS)))
    wr_hi = wr.astype(BF16)
    wr = jnp.concatenate([wr_hi, wr_hi, (wr - wr_hi.astype(F32)).astype(BF16)], 0)
    br = jnp.pad(jnp.concatenate([b_router_group[0], b_router_expert[0]]).astype(F32),
                 (0, LANES - N_GROUPS - N_EXPERTS)).reshape(1, LANES)
    wg, wu, wd = lax.optimization_barrier((w_gate[0].astype(BF16), w_up[0].astype(BF16), w_down[0].astype(BF16)))

    def layer_back(x2, oa2, z2, ob2, gates2, tile, moe_tile):
        h32, hb, comb = _back_call(x2, oa2, z2, ob2, gates2, lng, lnb, nw, wbr, wout, l1g, l1b, wr, br, tm=tile)
        if moe_tile % 256:
            return _moe_call(hb, h32, comb, wg, wu, wd, l2g, l2b, tm=moe_tile)
        return _moe_grouped_call(hb, h32, comb, wg, wu, wd, l2g, l2b, tm=moe_tile, ts=256)

    flat = lambda a: a.reshape(bsz * seq, a.shape[-1])
    n_tok = bsz * seq
    y_prompt = layer_back(flat(x_prompt), flat(o_a), flat(pf["z"]), flat(o_b), flat(pf["gates"]),
                          _pick(n_tok, (512, 256, 128, 64)), _pick(n_tok, (1024, 512, 256, 128, 64))).reshape(bsz, seq, d)

    st = jnp.swapaxes(state_conv[0].astype(F32), 0, 1)
    sf = _sample_front_call(x_sample.reshape(n_dec, d), lng, lnb, w_big, convw, st, sp)
    o_a_s, s_sample = _gdn_step_call(sf["q"], sf["k"], sf["v"], sf["small"], state_gdn[0].astype(F32))
    n_pages = page_table.shape[1]
    o_b_s = _fox_sample_call(page_table, cache_k[0], cache_v[0], cache_logf[0], sf["fq"], sf["fk"], sf["fv"],
                             sf["small"][:, L_LOGF:L_LOGF + H_B], pps=_pick(n_pages, (16, 8, 4, 2, 1)))
    y_sample = layer_back(x_sample.reshape(n_dec, d), o_a_s, sf["z"], o_b_s, sf["gates"], n_dec, n_dec)

    heads = lambda a: a.reshape(a.shape[:-1] + (H_B, DH_B))
    with_meta = lambda m, p: jnp.concatenate([jnp.broadcast_to(m, (bsz,) + m.shape[1:]), p], 1)
    rows = lambda a: a[0].reshape(N_META * H_B, DH_B)
    k_rows, v_rows = _put_meta_rows(pf["kout"], pf["vout"], rows(meta["fk32"]), rows(meta["fv32"]))
    k_prompt = k_rows.reshape(1, bsz, seq + N_META, H_B, DH_B)
    v_prompt = v_rows.reshape(1, bsz, seq + N_META, H_B, DH_B)
    lf_meta = jnp.swapaxes(meta["small"][:, :, L_LOGF:L_LOGF + H_B], 1, 2)
    lf_t = jnp.concatenate([jnp.broadcast_to(lf_meta, (bsz, H_B, N_META)), pf["small_t"][:, L_LOGF:L_LOGF + H_B]], 2)
    logf_prompt = jnp.swapaxes(lf_t, 1, 2)[None]
    conv_prompt = pf["tail"][:, 8 - (CONV_W - 1):][None]
    conv_sample = jnp.concatenate([state_conv[0][:, 1:].astype(F32), sf["new"][:, None]], 1)[None]
    return (y_prompt, y_sample.reshape(n_dec, 1, d), k_prompt, v_prompt, logf_prompt, s_prompt[None], conv_prompt,
            heads(sf["fk"])[None, :, None], heads(sf["fv"])[None, :, None],
            sf["small"][:, L_LOGF:L_LOGF + H_B][None, :, None], s_sample[None], conv_sample)
```
